```python
import math
import jax, jax.numpy as jnp
from jax import lax
import numpy as np

D_MODEL = 1024
BATCH = 2
SEQ = 8192
DEPTH = 4

N_A = DEPTH // 2
N_B = DEPTH - N_A
HEAD_DIM = 64
MIX_WIDTH = D_MODEL
MEM_HEADS = 4
MEM_WIDTH = MEM_HEADS * HEAD_DIM
MEM_LEN = 256
CONV_CH = MIX_WIDTH - MEM_WIDTH
CONV_K = 3
NSA_HEADS = CONV_CH // HEAD_DIM
NSA_KV_GROUPS = 2
NSA_HPG = NSA_HEADS // NSA_KV_GROUPS
N_BRANCH = 3
NSA_Q_WIDTH = NSA_HEADS * HEAD_DIM
NSA_GATE_WIDTH = NSA_HEADS * N_BRANCH
CMP_LEN = 32
CMP_STRIDE = 16
CMP_HIDDEN = 256
SEL_BLOCK = 64
SEL_TOPK = 16
N_LOCAL_SEL = 2
WINDOW = 512
Q_BLOCK = 128
REL_BUCKETS = 32
REL_MAX_DIST = 1024
FF_DIM = 4 * D_MODEL
EPS = 1e-6
NEG = -1e30
FORCE_SCORE = 1e4

kernel_name = "yoco_shortconv_nsa_hybrid"


def rmsnorm(x, g):
    xf = x.astype(jnp.float32)
    y = xf * lax.rsqrt(jnp.mean(xf * xf, axis=-1, keepdims=True) + EPS)
    return (y * g.astype(jnp.float32)).astype(x.dtype)


def rel_bucket(dist):
    n = jnp.maximum(dist, 0)
    max_exact = REL_BUCKETS // 2
    nf = jnp.maximum(n, 1).astype(jnp.float32)
    large = max_exact + (jnp.log(nf / max_exact) / math.log(REL_MAX_DIST / max_exact)
                         * (REL_BUCKETS - max_exact)).astype(jnp.int32)
    large = jnp.minimum(large, REL_BUCKETS - 1)
    return jnp.where(n < max_exact, n, large)


def masked_softmax(logits, mask):
    p = jax.nn.softmax(jnp.where(mask, logits, NEG), axis=-1)
    return jnp.where(mask, p, 0.0)


def selection_overlap(n_cmp, n_sel):
    c0 = np.arange(n_cmp)[:, None] * CMP_STRIDE
    s0 = np.arange(n_sel)[None, :] * SEL_BLOCK
    ov = np.minimum(c0 + CMP_LEN, s0 + SEL_BLOCK) - np.maximum(c0, s0)
    return np.clip(ov, 0, None).astype(np.float32) / CMP_LEN


def short_conv_mixer(z, conv_w):
    gate_b, gate_c, u = jnp.split(z, 3, axis=-1)
    v = gate_c * u
    y = lax.conv_general_dilated(
        v, conv_w[:, None, :].astype(v.dtype), window_strides=(1,),
        padding=[(CONV_K - 1, 0)], dimension_numbers=('NWC', 'WIO', 'NWC'),
        feature_group_count=CONV_CH)
    return gate_b * y


def memory_attention(q_raw, mem_k, mem_v, q_gain):
    b, s, _ = q_raw.shape
    q = rmsnorm(q_raw.reshape(b, s, MEM_HEADS, HEAD_DIM), q_gain) * (HEAD_DIM ** -0.5)
    lg = jnp.einsum('bshd,bmhd->bhsm', q, mem_k).astype(jnp.float32)
    p = jax.nn.softmax(lg, axis=-1).astype(mem_v.dtype)
    return jnp.einsum('bhsm,bmhd->bshd', p, mem_v).reshape(b, s, MEM_WIDTH)


def compress(u, pos, w1, w2):
    b, s, g, d = u.shape
    ch = u.reshape(b, s // CMP_STRIDE, CMP_STRIDE, g, d)
    blk = jnp.concatenate([ch[:, :-1], ch[:, 1:]], axis=2) + pos[:, None, :]
    flat = blk.transpose(0, 1, 3, 2, 4).reshape(b, -1, g, CMP_LEN * d)
    return jax.nn.gelu(flat @ w1) @ w2


def build_shared_kv(x, kv_norm, w_kv_shared, k_gain, cmp_pos, cmp_w1, cmp_w2):
    b, s, _ = x.shape
    h = rmsnorm(x, kv_norm)
    kv = (h @ w_kv_shared).reshape(b, s, 6, NSA_KV_GROUPS, HEAD_DIM)
    kc = rmsnorm(compress(kv[:, :, 0], cmp_pos[0], cmp_w1[0], cmp_w2[0]), k_gain[0])
    vc = compress(kv[:, :, 1], cmp_pos[1], cmp_w1[1], cmp_w2[1])
    n_sel = s // SEL_BLOCK
    def to_blocks(u):
        return u.reshape(b, n_sel, SEL_BLOCK, NSA_KV_GROUPS, HEAD_DIM).transpose(0, 3, 1, 2, 4)
    ks = to_blocks(rmsnorm(kv[:, :, 2], k_gain[1]))
    vs = to_blocks(kv[:, :, 3])
    def pad_front(u):
        return jnp.pad(u, ((0, 0), (WINDOW, 0), (0, 0), (0, 0)))
    kw = pad_front(rmsnorm(kv[:, :, 4], k_gain[2]))
    vw = pad_front(kv[:, :, 5])
    return (kc, vc, ks, vs, kw, vw)


def nsa_mixer(q_raw, gate_logits, q_gain, kc, vc, ks, vs, kw, vw, rel_bias):
    b, s, _ = q_raw.shape
    G, Hg, hd = NSA_KV_GROUPS, NSA_HPG, HEAD_DIM
    q = rmsnorm(q_raw.reshape(b, s, G, Hg, hd), q_gain) * (hd ** -0.5)
    gates = jax.nn.sigmoid(gate_logits.reshape(b, s, G, Hg, N_BRANCH))
    n_cmp = kc.shape[1]
    n_sel = ks.shape[2]
    topk = min(SEL_TOPK, n_sel)
    overlap = jnp.asarray(selection_overlap(n_cmp, n_sel), dtype=jnp.float32)
    cmp_end = jnp.arange(n_cmp) * CMP_STRIDE + CMP_LEN - 1
    sel_idx = jnp.arange(n_sel)
    table = rel_bias.astype(jnp.float32)
    table_g = table.reshape(REL_BUCKETS, G, Hg).transpose(1, 0, 2)
    bi = jnp.arange(b)[:, None, None, None]
    gi = jnp.arange(G)[None, :, None, None]
    kw_len = WINDOW + Q_BLOCK

    def block(i):
        s0 = i * Q_BLOCK
        tq = s0 + jnp.arange(Q_BLOCK)
        qb = lax.dynamic_slice_in_dim(q, s0, Q_BLOCK, axis=1)
        gb = lax.dynamic_slice_in_dim(gates, s0, Q_BLOCK, axis=1)
        d_c = tq[:, None] - cmp_end[None, :]
        bias_c = table[rel_bucket(d_c)].reshape(Q_BLOCK, n_cmp, G, Hg).transpose(2, 3, 0, 1)
        lg = jnp.einsum('bqghd,bcgd->bghqc', qb, kc).astype(jnp.float32) + bias_c
        p_c = masked_softmax(lg, d_c >= 0)
        o_c = jnp.einsum('bghqc,bcgd->bqghd', p_c.astype(vc.dtype), vc)
        imp = jnp.einsum('bghqc,cj->bgqj', p_c, overlap)
        back = (tq // SEL_BLOCK)[:, None] - sel_idx[None, :]
        forced = (sel_idx[None, :] == 0) | ((back >= 0) & (back < N_LOCAL_SEL))
        score = jnp.where(back >= 0, jnp.where(forced, FORCE_SCORE, imp), NEG)
        _, idx = lax.top_k(score, topk)
        k_sel = ks[bi, gi, idx].reshape(b, G, Q_BLOCK, topk * SEL_BLOCK, hd)
        v_sel = vs[bi, gi, idx].reshape(b, G, Q_BLOCK, topk * SEL_BLOCK, hd)
        pos_s = (idx[..., None] * SEL_BLOCK + jnp.arange(SEL_BLOCK)).reshape(b, G, Q_BLOCK, topk * SEL_BLOCK)
        d_s = tq[None, None, :, None] - pos_s
        bias_s = table_g[gi, rel_bucket(d_s)].transpose(0, 1, 4, 2, 3)
        lg = jnp.einsum('bqghd,bgqkd->bghqk', qb, k_sel).astype(jnp.float32) + bias_s
        p_s = masked_softmax(lg, (d_s >= 0)[:, :, None])
        o_s = jnp.einsum('bghqk,bgqkd->bqghd', p_s.astype(v_sel.dtype), v_sel)
        k_w = lax.dynamic_slice_in_dim(kw, s0, kw_len, axis=1)
        v_w = lax.dynamic_slice_in_dim(vw, s0, kw_len, axis=1)
        pos_w = s0 - WINDOW + jnp.arange(kw_len)
        d_w = tq[:, None] - pos_w[None, :]
        mask_w = (d_w >= 0) & (d_w < WINDOW) & (pos_w[None, :] >= 0)
        bias_w = table[rel_bucket(d_w)].reshape(Q_BLOCK, kw_len, G, Hg).transpose(2, 3, 0, 1)
        lg = jnp.einsum('bqghd,bkgd->bghqk', qb, k_w).astype(jnp.float32) + bias_w
        p_w = masked_softmax(lg, mask_w)
        o_w = jnp.einsum('bghqk,bkgd->bqghd', p_w.astype(v_w.dtype), v_w)
        return gb[..., 0:1] * o_c + gb[..., 1:2] * o_s + gb[..., 2:3] * o_w

    out = lax.map(block, jnp.arange(s // Q_BLOCK))
    return jnp.moveaxis(out, 0, 1).reshape(b, s, NSA_Q_WIDTH)


def setup_inputs(seed: int = 0) -> dict:
    key = jax.random.key(seed)
    ks = jax.random.split(key, 22)
    def nrm(k, shape, scale):
        return jax.random.normal(k, shape, jnp.float32) * scale
    def gain(k, shape):
        return 1.0 + 0.1 * jax.random.normal(k, shape, jnp.float32)
    a_in = 3 * CONV_CH + MEM_WIDTH
    b_in = NSA_Q_WIDTH + NSA_GATE_WIDTH + MEM_WIDTH
    return {
        "x": nrm(ks[0], (BATCH, SEQ, D_MODEL), 1.0),
        "mem": nrm(ks[1], (BATCH, MEM_LEN, D_MODEL), 1.0),
        "mix_norm": gain(ks[2], (DEPTH, D_MODEL)),
        "a_w_in": nrm(ks[3], (N_A, D_MODEL, a_in), D_MODEL ** -0.5),
        "a_conv_w": nrm(ks[4], (N_A, CONV_K, CONV_CH), CONV_K ** -0.5),
        "b_w_in": nrm(ks[5], (N_B, D_MODEL, b_in), D_MODEL ** -0.5),
        "b_q_gain": gain(ks[6], (N_B, HEAD_DIM)),
        "kv_norm": gain(ks[7], (D_MODEL,)),
        "w_kv_shared": nrm(ks[8], (D_MODEL, 6 * NSA_KV_GROUPS * HEAD_DIM), D_MODEL ** -0.5),
        "k_gain": gain(ks[9], (N_BRANCH, HEAD_DIM)),
        "cmp_pos": nrm(ks[10], (2, CMP_LEN, HEAD_DIM), 0.1),
        "cmp_w1": nrm(ks[11], (2, CMP_LEN * HEAD_DIM, CMP_HIDDEN), (CMP_LEN * HEAD_DIM) ** -0.5),
        "cmp_w2": nrm(ks[12], (2, CMP_HIDDEN, HEAD_DIM), CMP_HIDDEN ** -0.5),
        "rel_bias": nrm(ks[13], (REL_BUCKETS, NSA_HEADS), 0.5),
        "mem_norm": gain(ks[14], (D_MODEL,)),
        "mem_w_kv": nrm(ks[15], (DEPTH, D_MODEL, 2 * MEM_WIDTH), D_MODEL ** -0.5),
        "mem_q_gain": gain(ks[16], (DEPTH, HEAD_DIM)),
        "mem_k_gain": gain(ks[17], (DEPTH, HEAD_DIM)),
        "w_out": nrm(ks[18], (DEPTH, MIX_WIDTH, D_MODEL), MIX_WIDTH ** -0.5),
        "mlp_norm": gain(ks[19], (DEPTH, D_MODEL)),
        "w_up": nrm(ks[20], (DEPTH, D_MODEL, FF_DIM), D_MODEL ** -0.5),
        "w_down": nrm(ks[21], (DEPTH, FF_DIM, D_MODEL), FF_DIM ** -0.5),
    }


def reference(x, mem, mix_norm, a_w_in, a_conv_w, b_w_in, b_q_gain, kv_norm, w_kv_shared,
              k_gain, cmp_pos, cmp_w1, cmp_w2, rel_bias, mem_norm, mem_w_kv, mem_q_gain,
              mem_k_gain, w_out, mlp_norm, w_up, w_down):
    b, m, _ = mem.shape
    mem_h = rmsnorm(mem, mem_norm)
    shared = None
    for layer in range(DEPTH):
        h = rmsnorm(x, mix_norm[layer])
        mkv = (mem_h @ mem_w_kv[layer]).reshape(b, m, 2, MEM_HEADS, HEAD_DIM)
        mem_k = rmsnorm(mkv[:, :, 0], mem_k_gain[layer])
        mem_v = mkv[:, :, 1]
        if layer < N_A:
            z = h @ a_w_in[layer]
            y_tok = short_conv_mixer(z[..., :3 * CONV_CH], a_conv_w[layer])
            q_mem = z[..., 3 * CONV_CH:]
        else:
            j = layer - N_A
            z = h @ b_w_in[j]
            y_tok = nsa_mixer(z[..., :NSA_Q_WIDTH],
                              z[..., NSA_Q_WIDTH:NSA_Q_WIDTH + NSA_GATE_WIDTH],
                              b_q_gain[j], *shared, rel_bias)
            q_mem = z[..., NSA_Q_WIDTH + NSA_GATE_WIDTH:]
        y_mem = memory_attention(q_mem, mem_k, mem_v, mem_q_gain[layer])
        x = x + jnp.concatenate([y_tok, y_mem], axis=-1) @ w_out[layer]
        hm = rmsnorm(x, mlp_norm[layer])
        x = x + jnp.square(jax.nn.relu(hm @ w_up[layer])) @ w_down[layer]
        if layer == N_A - 1:
            shared = build_shared_kv(x, kv_norm, w_kv_shared, k_gain, cmp_pos, cmp_w1, cmp_w2)
    return x
```

```python
import functools
import math

import numpy as np
import jax
import jax.numpy as jnp
from jax import lax
from jax.experimental import pallas as pl
from jax.experimental.pallas import tpu as pltpu

F32 = jnp.float32
BF16 = jnp.bfloat16

HEAD_DIM = 64
MEM_HEADS = 4
MEM_WIDTH = MEM_HEADS * HEAD_DIM
CONV_CH = 768
NSA_HEADS = 12
KV_GROUPS = 2
HPG = NSA_HEADS // KV_GROUPS
N_BRANCH = 3
CMP_LEN = 32
CMP_STRIDE = 16
SEL_BLOCK = 64
SEL_TOPK = 16
N_LOCAL_SEL = 2
WINDOW = 512
REL_BUCKETS = 32
REL_MAX_DIST = 1024
EPS = 1e-6
NEG = -1e30
FORCE_SCORE = 1e4

LANES = 128
VMEM_LIMIT_BYTES = 56 * 1024 * 1024

TQ = 128
TK_SEL = 256
TK_WIN = 128
SEL_LANES = 128
TM = 512

N_SEL_NEAR = 8
SEL_CONST = N_SEL_NEAR
SEL_MASKED = N_SEL_NEAR + 1
N_SEL_TILES = N_SEL_NEAR + 2
N_WIN_TILES = WINDOW // TK_WIN + 1
N_CMP_NEAR = 23
CMP_CONST = N_CMP_NEAR
CMP_MASKED = N_CMP_NEAR + 1
N_CMP_TILES = N_CMP_NEAR + 2
WIN_BASE = N_SEL_TILES
CMP_BASE = N_SEL_TILES + N_WIN_TILES
N_TILES = CMP_BASE + N_CMP_TILES


def _rms(xf, g):
    ms = jnp.mean(xf * xf, axis=-1, keepdims=True)
    return xf * lax.rsqrt(ms + EPS) * g


def _dot(a, b):
    return jnp.dot(a, b, preferred_element_type=F32)


def _dot_nt(a, b):
    return lax.dot_general(a, b, (((1,), (1,)), ((), ())), preferred_element_type=F32)


def _params(sem):
    return pltpu.CompilerParams(dimension_semantics=sem, vmem_limit_bytes=VMEM_LIMIT_BYTES)


def _mem_kv_kernel(mem_ref, mnorm_ref, w_ref, kg_ref, mk_ref, mv_ref, *, nb, ml):
    mh = _rms(mem_ref[...], mnorm_ref[...]).astype(BF16)
    kv = _dot(mh, w_ref[...])
    for h in range(MEM_HEADS):
        kh = _rms(kv[:, h * HEAD_DIM:(h + 1) * HEAD_DIM], kg_ref[...]).astype(BF16)
        vh = kv[:, MEM_WIDTH + h * HEAD_DIM:MEM_WIDTH + (h + 1) * HEAD_DIM].astype(BF16)
        for b in range(nb):
            mk_ref[b, h] = kh[b * ml:(b + 1) * ml]
            mv_ref[b, h] = vh[b * ml:(b + 1) * ml]


def _mem_kv(mem, mem_norm, mem_w_kv, mem_k_gain):
    nb, ml, d = mem.shape
    depth = mem_w_kv.shape[0]
    out = jax.ShapeDtypeStruct((depth, nb, MEM_HEADS, ml, HEAD_DIM), BF16)
    return pl.pallas_call(
        functools.partial(_mem_kv_kernel, nb=nb, ml=ml),
        out_shape=(out, out),
        grid=(depth,),
        in_specs=[
            pl.BlockSpec((nb * ml, d), lambda l: (0, 0)),
            pl.BlockSpec((1, d), lambda l: (0, 0)),
            pl.BlockSpec((None, d, 2 * MEM_WIDTH), lambda l: (l, 0, 0)),
            pl.BlockSpec((None, 1, HEAD_DIM), lambda l: (l, 0, 0)),
        ],
        out_specs=(
            pl.BlockSpec((None, nb, MEM_HEADS, ml, HEAD_DIM), lambda l: (l, 0, 0, 0, 0)),
            pl.BlockSpec((None, nb, MEM_HEADS, ml, HEAD_DIM), lambda l: (l, 0, 0, 0, 0)),
        ),
        compiler_params=_params(("arbitrary",)),
        name="mem_kv",
    )(mem.reshape(nb * ml, d), mem_norm.reshape(1, d), mem_w_kv.astype(BF16),
      mem_k_gain.reshape(depth, 1, HEAD_DIM))


def _mem_attn(qm, mk_ref, mv_ref, qg, ymem_ref):
    for h in range(MEM_HEADS):
        qh = _rms(qm[:, h * HEAD_DIM:(h + 1) * HEAD_DIM], qg) * (HEAD_DIM ** -0.5)
        lg = _dot_nt(qh.astype(BF16), mk_ref[h])
        m = jnp.max(lg, axis=-1, keepdims=True)
        e = jnp.exp(lg - m)
        l = jnp.sum(e, axis=-1, keepdims=True)
        o = _dot(e.astype(BF16), mv_ref[h]) / l
        ymem_ref[:, h * HEAD_DIM:(h + 1) * HEAD_DIM] = o.astype(ymem_ref.dtype)


def _bucket_np(d):
    n = np.maximum(d, 0)
    max_exact = REL_BUCKETS // 2
    nf = np.maximum(n, 1).astype(np.float64)
    large = max_exact + (np.log(nf / max_exact) / math.log(REL_MAX_DIST / max_exact)
                         * (REL_BUCKETS - max_exact)).astype(np.int32)
    large = np.minimum(large, REL_BUCKETS - 1)
    return np.where(n < max_exact, n, large).astype(np.int32)


def _bucket_tiles():
    r = np.arange(TQ)[:, None]
    k = np.arange(LANES)[None, :]
    tiles = []
    for delta in range(N_SEL_NEAR):
        d = delta * LANES + r - k
        tiles.append(np.where(d >= 0, _bucket_np(d), -1))
    tiles.append(np.full((TQ, LANES), REL_BUCKETS - 1))
    tiles.append(np.full((TQ, LANES), -1))
    for delta in range(N_WIN_TILES):
        d = delta * LANES + r - k
        tiles.append(np.where((d >= 0) & (d < WINDOW), _bucket_np(d), -1))
    for delta in range(N_CMP_NEAR):
        d = delta * LANES + r - CMP_STRIDE * k - (CMP_LEN - 1)
        tiles.append(np.where(d >= 0, _bucket_np(d), -1))
    tiles.append(np.full((TQ, LANES), REL_BUCKETS - 1))
    tiles.append(np.full((TQ, LANES), -1))
    return np.stack(tiles).astype(np.int32)


def _bias_tables_kernel(tab_ref, ids_ref, out_ref):
    ids = ids_ref[...]
    for h in range(NSA_HEADS):
        acc = jnp.zeros(ids.shape, F32)
        for b in range(REL_BUCKETS):
            acc = jnp.where(ids == b, tab_ref[b, h], acc)
        out_ref[h] = jnp.where(ids < 0, NEG, acc)


def _bias_tables(rel_bias):
    ids = jnp.asarray(_bucket_tiles())
    return pl.pallas_call(
        _bias_tables_kernel,
        out_shape=jax.ShapeDtypeStruct((N_TILES, NSA_HEADS, TQ, LANES), F32),
        grid=(N_TILES,),
        in_specs=[
            pl.BlockSpec(memory_space=pltpu.SMEM),
            pl.BlockSpec((None, TQ, LANES), lambda t: (t, 0, 0)),
        ],
        out_specs=pl.BlockSpec((None, NSA_HEADS, TQ, LANES), lambda t: (t, 0, 0, 0)),
        compiler_params=_params(("arbitrary",)),
        name="bias_tables",
    )(rel_bias.astype(F32), ids)


def _mixer_a_kernel(x_ref, g_ref, w_ref, cw_ref, mk_ref, mv_ref, qg_ref,
                    ytok_ref, ymem_ref, carry_ref, *, tm):
    @pl.when(pl.program_id(1) == 0)
    def _():
        carry_ref[...] = jnp.zeros_like(carry_ref)

    h = _rms(x_ref[...], g_ref[...]).astype(BF16)
    z = _dot(h, w_ref[...])
    gate_b = z[:, :CONV_CH]
    v = z[:, CONV_CH:2 * CONV_CH] * z[:, 2 * CONV_CH:3 * CONV_CH]
    prev = carry_ref[...]
    row = lax.broadcasted_iota(jnp.int32, (tm, CONV_CH), 0)
    v1 = jnp.where(row == 0, prev[7:8, :], pltpu.roll(v, 1, 0))
    v2 = jnp.where(row == 0, prev[6:7, :], jnp.where(row == 1, prev[7:8, :], pltpu.roll(v, 2, 0)))
    cw = cw_ref[...]
    y = gate_b * (cw[0:1, :] * v2 + cw[1:2, :] * v1 + cw[2:3, :] * v)
    carry_ref[...] = v[tm - 8:, :]
    ytok_ref[...] = y.astype(ytok_ref.dtype)
    _mem_attn(z[:, 3 * CONV_CH:], mk_ref, mv_ref, qg_ref[...], ymem_ref)


def _mixer_a(x, norm_g, w_in, conv_w, mk, mv, mem_q_gain):
    nb, s, d = x.shape
    ml = mk.shape[2]
    n_in = w_in.shape[1]
    return pl.pallas_call(
        functools.partial(_mixer_a_kernel, tm=TM),
        out_shape=(jax.ShapeDtypeStruct((nb, s, CONV_CH), BF16),
                   jax.ShapeDtypeStruct((nb, s, MEM_WIDTH), BF16)),
        grid=(nb, s // TM),
        in_specs=[
            pl.BlockSpec((None, TM, d), lambda b, i: (b, i, 0)),
            pl.BlockSpec((1, d), lambda b, i: (0, 0)),
            pl.BlockSpec((d, n_in), lambda b, i: (0, 0)),
            pl.BlockSpec((3, CONV_CH), lambda b, i: (0, 0)),
            pl.BlockSpec((None, MEM_HEADS, ml, HEAD_DIM), lambda b, i: (b, 0, 0, 0)),
            pl.BlockSpec((None, MEM_HEADS, ml, HEAD_DIM), lambda b, i: (b, 0, 0, 0)),
            pl.BlockSpec((1, HEAD_DIM), lambda b, i: (0, 0)),
        ],
        out_specs=(
            pl.BlockSpec((None, TM, CONV_CH), lambda b, i: (b, i, 0)),
            pl.BlockSpec((None, TM, MEM_WIDTH), lambda b, i: (b, i, 0)),
        ),
        scratch_shapes=[pltpu.VMEM((8, CONV_CH), F32)],
        compiler_params=_params(("arbitrary", "arbitrary")),
        name="mixer_a",
    )(x, norm_g.reshape(1, d), w_in.astype(BF16), conv_w, mk, mv, mem_q_gain.reshape(1, HEAD_DIM))


B_GATE0 = NSA_HEADS * HEAD_DIM
B_QMEM0 = B_GATE0 + KV_GROUPS * LANES
B_WIDTH = B_QMEM0 + MEM_WIDTH


def _b_inproj_kernel(x_ref, g_ref, w_ref, qg_ref, mk_ref, mv_ref, mqg_ref,
                     q_ref, gates_ref, ymem_ref):
    h = _rms(x_ref[...], g_ref[...]).astype(BF16)
    z = _dot(h, w_ref[...])
    qg = qg_ref[...]
    for hh in range(NSA_HEADS):
        qh = _rms(z[:, hh * HEAD_DIM:(hh + 1) * HEAD_DIM], qg) * (HEAD_DIM ** -0.5)
        q_ref[hh] = qh.astype(q_ref.dtype)
    for g in range(KV_GROUPS):
        gl = z[:, B_GATE0 + g * LANES:B_GATE0 + (g + 1) * LANES]
        gates_ref[g] = 1.0 / (1.0 + jnp.exp(-gl))
    _mem_attn(z[:, B_QMEM0:], mk_ref, mv_ref, mqg_ref[...], ymem_ref)


def _b_inproj(x, norm_g, w_in, q_gain, mk, mv, mem_q_gain):
    nb, s, d = x.shape
    ml = mk.shape[2]
    nq = NSA_HEADS * HEAD_DIM
    ng = HPG * N_BRANCH
    pad = jnp.zeros((d, LANES - ng), w_in.dtype)
    w = jnp.concatenate([w_in[:, :nq], w_in[:, nq:nq + ng], pad,
                         w_in[:, nq + ng:nq + 2 * ng], pad, w_in[:, nq + 2 * ng:]], axis=1)
    return pl.pallas_call(
        _b_inproj_kernel,
        out_shape=(jax.ShapeDtypeStruct((nb, NSA_HEADS, s, HEAD_DIM), BF16),
                   jax.ShapeDtypeStruct((nb, KV_GROUPS, s, LANES), F32),
                   jax.ShapeDtypeStruct((nb, s, MEM_WIDTH), BF16)),
        grid=(nb, s // TM),
        in_specs=[
            pl.BlockSpec((None, TM, d), lambda b, i: (b, i, 0)),
            pl.BlockSpec((1, d), lambda b, i: (0, 0)),
            pl.BlockSpec((d, B_WIDTH), lambda b, i: (0, 0)),
            pl.BlockSpec((1, HEAD_DIM), lambda b, i: (0, 0)),
            pl.BlockSpec((None, MEM_HEADS, ml, HEAD_DIM), lambda b, i: (b, 0, 0, 0)),
            pl.BlockSpec((None, MEM_HEADS, ml, HEAD_DIM), lambda b, i: (b, 0, 0, 0)),
            pl.BlockSpec((1, HEAD_DIM), lambda b, i: (0, 0)),
        ],
        out_specs=(
            pl.BlockSpec((None, NSA_HEADS, TM, HEAD_DIM), lambda b, i: (b, 0, i, 0)),
            pl.BlockSpec((None, KV_GROUPS, TM, LANES), lambda b, i: (b, 0, i, 0)),
            pl.BlockSpec((None, TM, MEM_WIDTH), lambda b, i: (b, i, 0)),
        ),
        compiler_params=_params(("arbitrary", "arbitrary")),
        name="b_inproj",
    )(x, norm_g.reshape(1, d), w.astype(BF16), q_gain.reshape(1, HEAD_DIM), mk, mv,
      mem_q_gain.reshape(1, HEAD_DIM))


FF_CHUNK = 1024
TM_POST = 1024


def _post_kernel(x_ref, ytok_ref, ymem_ref, woa_ref, wob_ref, g_ref, wup_ref, wdn_ref, o_ref, hm_ref):
    @pl.when(pl.program_id(1) == 0)
    def _():
        x1 = x_ref[...] + _dot(ytok_ref[...], woa_ref[...]) + _dot(ymem_ref[...], wob_ref[...])
        o_ref[...] = x1
        hm_ref[...] = _rms(x1, g_ref[...]).astype(hm_ref.dtype)

    a = _dot(hm_ref[...], wup_ref[...])
    a = jnp.square(jnp.maximum(a, 0.0)).astype(BF16)
    o_ref[...] += _dot(a, wdn_ref[...])


def _post(x, y_tok, y_mem, w_out, norm_g, w_up, w_down):
    n, d = x.shape
    ff = w_up.shape[1]
    nt = y_tok.shape[1]
    tm = min(TM_POST, n)
    const = lambda i, c: (0, 0)
    rows = lambda i, c: (i, 0)
    return pl.pallas_call(
        _post_kernel,
        out_shape=jax.ShapeDtypeStruct((n, d), F32),
        grid=(n // tm, ff // FF_CHUNK),
        in_specs=[
            pl.BlockSpec((tm, d), rows),
            pl.BlockSpec((tm, nt), rows),
            pl.BlockSpec((tm, MEM_WIDTH), rows),
            pl.BlockSpec((nt, d), const),
            pl.BlockSpec((MEM_WIDTH, d), const),
            pl.BlockSpec((1, d), const),
            pl.BlockSpec((d, FF_CHUNK), lambda i, c: (0, c)),
            pl.BlockSpec((FF_CHUNK, d), lambda i, c: (c, 0)),
        ],
        out_specs=pl.BlockSpec((tm, d), rows),
        scratch_shapes=[pltpu.VMEM((tm, d), BF16)],
        compiler_params=_params(("arbitrary", "arbitrary")),
        name="post",
    )(x, y_tok, y_mem, w_out[:nt].astype(BF16), w_out[nt:].astype(BF16), norm_g.reshape(1, d),
      w_up.astype(BF16), w_down.astype(BF16))


def _build_kv_kernel(x_ref, g_ref, w_ref, kg_ref, ucmp_ref, ks_ref, vs_ref, kw_ref, vw_ref):
    h = _rms(x_ref[...], g_ref[...]).astype(BF16)
    kv = _dot(h, w_ref[...])
    gw = KV_GROUPS * HEAD_DIM
    ucmp_ref[...] = kv[:, :2 * gw].astype(ucmp_ref.dtype)
    kg = kg_ref[...]
    for g in range(KV_GROUPS):
        sl = lambda t: kv[:, t * gw + g * HEAD_DIM:t * gw + (g + 1) * HEAD_DIM]
        ks_ref[g] = _rms(sl(2), kg[1:2, :]).astype(ks_ref.dtype)
        vs_ref[g] = sl(3).astype(vs_ref.dtype)
        kw_ref[g] = _rms(sl(4), kg[2:3, :]).astype(kw_ref.dtype)
        vw_ref[g] = sl(5).astype(vw_ref.dtype)


def _build_kv(x, kv_norm, w_kv, k_gain):
    nb, s, d = x.shape
    gw = KV_GROUPS * HEAD_DIM
    kvs = jax.ShapeDtypeStruct((nb, KV_GROUPS, s, HEAD_DIM), BF16)
    kv_spec = pl.BlockSpec((None, KV_GROUPS, TM, HEAD_DIM), lambda b, i: (b, 0, i, 0))
    return pl.pallas_call(
        _build_kv_kernel,
        out_shape=(jax.ShapeDtypeStruct((nb, s, 2 * gw), BF16), kvs, kvs, kvs, kvs),
        grid=(nb, s // TM),
        in_specs=[
            pl.BlockSpec((None, TM, d), lambda b, i: (b, i, 0)),
            pl.BlockSpec((1, d), lambda b, i: (0, 0)),
            pl.BlockSpec((d, 6 * gw), lambda b, i: (0, 0)),
            pl.BlockSpec((N_BRANCH, HEAD_DIM), lambda b, i: (0, 0)),
        ],
        out_specs=(pl.BlockSpec((None, TM, 2 * gw), lambda b, i: (b, i, 0)),
                   kv_spec, kv_spec, kv_spec, kv_spec),
        compiler_params=_params(("arbitrary", "arbitrary")),
        name="build_kv",
    )(x, kv_norm.reshape(1, d), w_kv.astype(BF16), k_gain)


def _compress_kernel(u_ref, w1_ref, w2_ref, pos_ref, kg_ref, out_ref, *, nc):
    half = CMP_STRIDE * HEAD_DIM
    u = u_ref[...]
    lo = _dot(u, w1_ref[:half, :])
    hi = _dot(u, w1_ref[half:, :])
    posb = _dot(jnp.broadcast_to(pos_ref[...], (8, 2 * half)).astype(BF16), w1_ref[...])[0:1, :]
    h1 = lo + pltpu.roll(hi, nc - 1, 0) + posb
    hid = 0.5 * h1 * (1.0 + jnp.tanh(math.sqrt(2.0 / math.pi) * (h1 + 0.044715 * (h1 * h1 * h1))))
    c = _dot(hid.astype(BF16), w2_ref[...])
    is_key = pl.program_id(0) == 0
    out_ref[...] = jnp.where(is_key, _rms(c, kg_ref[...]), c).astype(out_ref.dtype)


def _compress(ucmp, cmp_pos, cmp_w1, cmp_w2, k_gain0):
    nb, s, _ = ucmp.shape
    nc = s // CMP_STRIDE
    width = CMP_STRIDE * HEAD_DIM
    u = ucmp.reshape(nb, nc, CMP_STRIDE, 2, KV_GROUPS, HEAD_DIM)
    u = u.transpose(3, 0, 4, 1, 2, 5).reshape(2, nb, KV_GROUPS, nc, width)
    hidden = cmp_w1.shape[2]
    return pl.pallas_call(
        functools.partial(_compress_kernel, nc=nc),
        out_shape=jax.ShapeDtypeStruct((2, nb, KV_GROUPS, nc, HEAD_DIM), BF16),
        grid=(2, nb, KV_GROUPS),
        in_specs=[
            pl.BlockSpec((None, None, None, nc, width), lambda t, b, g: (t, b, g, 0, 0)),
            pl.BlockSpec((None, 2 * width, hidden), lambda t, b, g: (t, 0, 0)),
            pl.BlockSpec((None, hidden, HEAD_DIM), lambda t, b, g: (t, 0, 0)),
            pl.BlockSpec((None, 1, 2 * width), lambda t, b, g: (t, 0, 0)),
            pl.BlockSpec((1, HEAD_DIM), lambda t, b, g: (0, 0)),
        ],
        out_specs=pl.BlockSpec((None, None, None, nc, HEAD_DIM), lambda t, b, g: (t, b, g, 0, 0)),
        compiler_params=_params(("arbitrary", "arbitrary", "arbitrary")),
        name="compress",
    )(u, cmp_w1.astype(BF16), cmp_w2.astype(BF16), cmp_pos.reshape(2, 1, 2 * width),
      k_gain0.reshape(1, HEAD_DIM))


def _overlap_np(nc, n_sel):
    c0 = np.arange(nc)[:, None] * CMP_STRIDE
    s0 = np.arange(SEL_LANES)[None, :] * SEL_BLOCK
    ov = np.minimum(c0 + CMP_LEN, s0 + SEL_BLOCK) - np.maximum(c0, s0)
    ov = np.clip(ov, 0, None).astype(np.float32) / CMP_LEN
    ov[nc - 1:, :] = 0.0
    ov[:, n_sel:] = 0.0
    return ov


def _select_topk(score, ntop):
    lane = lax.broadcasted_iota(jnp.int32, score.shape, 1).astype(F32)

    def body(_, carry):
        sc, sel = carry
        m = jnp.max(sc, axis=-1, keepdims=True)
        first = jnp.min(jnp.where(sc == m, lane, float(SEL_LANES)), axis=-1, keepdims=True)
        pick = lane == first
        return jnp.where(pick, -jnp.inf, sc), jnp.where(pick, 1.0, sel)

    _, sel = lax.fori_loop(0, ntop, body, (score, jnp.zeros(score.shape, F32)))
    return sel


def _cmp_attn_kernel(*refs, nj, n_sel):
    q_ref, kc_ref, vc_ref, ov_ref = refs[:4]
    tab_refs = refs[4:4 + nj]
    oc_ref, sel_ref = refs[4 + nj:]
    i = pl.program_id(2)
    nc = kc_ref.shape[0]
    qs = q_ref[...].reshape(HPG * TQ, HEAD_DIM)
    s = _dot_nt(qs, kc_ref[...]).reshape(HPG, TQ, nc)
    if nj == 1:
        bias = tab_refs[0][...]
    else:
        bias = jnp.concatenate([t[...] for t in tab_refs], axis=-1)
    s = s + bias
    m = jnp.maximum(jnp.max(s, axis=-1, keepdims=True), 0.5 * NEG)
    e = jnp.exp(s - m)
    l = jnp.sum(e, axis=-1, keepdims=True)
    p = e / jnp.where(l == 0.0, 1.0, l)
    oc_ref[...] = _dot(p.reshape(HPG * TQ, nc).astype(BF16), vc_ref[...]).reshape(HPG, TQ, HEAD_DIM)

    psum = jnp.sum(p, axis=0)
    p_hi = psum.astype(BF16)
    p_lo = (psum - p_hi.astype(F32)).astype(BF16)
    imp = _dot(p_hi, ov_ref[...]) + _dot(p_lo, ov_ref[...])

    tq = i * TQ + lax.broadcasted_iota(jnp.int32, (TQ, SEL_LANES), 0)
    j = lax.broadcasted_iota(jnp.int32, (TQ, SEL_LANES), 1)
    back = jnp.right_shift(tq, 6) - j
    forced = (j == 0) | ((back >= 0) & (back < N_LOCAL_SEL))
    score = jnp.where(back >= 0, jnp.where(forced, FORCE_SCORE, imp), NEG)
    sel = _select_topk(score, min(SEL_TOPK, n_sel))
    sel_ref[...] = jnp.where(back >= 0, sel, 0.0).astype(sel_ref.dtype)


def _cmp_attn(q, kvc, tables):
    nb, _, s, _ = q.shape
    nc = s // CMP_STRIDE
    nj = nc // LANES
    n_sel = s // SEL_BLOCK
    ov = jnp.asarray(_overlap_np(nc, n_sel), dtype=BF16)

    def tab_map(jt):
        def index_map(b, g, i):
            didx = i - (LANES * CMP_STRIDE // TQ) * jt
            row = jnp.where(didx < 0, CMP_MASKED, jnp.minimum(didx, CMP_CONST))
            return (CMP_BASE + row, g, 0, 0)
        return index_map

    kv_spec = lambda t: pl.BlockSpec((None, None, None, nc, HEAD_DIM), lambda b, g, i: (t, b, g, 0, 0))
    return pl.pallas_call(
        functools.partial(_cmp_attn_kernel, nj=nj, n_sel=n_sel),
        out_shape=(jax.ShapeDtypeStruct((nb, NSA_HEADS, s, HEAD_DIM), F32),
                   jax.ShapeDtypeStruct((nb, KV_GROUPS, s, SEL_LANES), BF16)),
        grid=(nb, KV_GROUPS, s // TQ),
        in_specs=[
            pl.BlockSpec((None, HPG, TQ, HEAD_DIM), lambda b, g, i: (b, g, i, 0)),
            kv_spec(0), kv_spec(1),
            pl.BlockSpec((nc, SEL_LANES), lambda b, g, i: (0, 0)),
        ] + [pl.BlockSpec((None, HPG, TQ, LANES), tab_map(jt)) for jt in range(nj)],
        out_specs=(
            pl.BlockSpec((None, HPG, TQ, HEAD_DIM), lambda b, g, i: (b, g, i, 0)),
            pl.BlockSpec((None, None, TQ, SEL_LANES), lambda b, g, i: (b, g, i, 0)),
        ),
        compiler_params=_params(("arbitrary", "arbitrary", "arbitrary")),
        name="cmp_attn",
    )(q, kvc, kvc, ov, *([tables] * nj))


def _flash(qs, k_ref, v_ref, t_lo, t_hi, tk, bias_fn):
    rows = HPG * TQ

    def body(t, carry):
        m, l, acc = carry
        k0 = pl.multiple_of(t * tk, tk)
        kt = k_ref[pl.ds(k0, tk), :]
        vt = v_ref[pl.ds(k0, tk), :]
        s = _dot_nt(qs, kt).reshape(HPG, TQ, tk) + bias_fn(t)
        m_new = jnp.maximum(m, jnp.max(s, axis=-1, keepdims=True))
        alpha = jnp.exp(m - m_new)
        p = jnp.exp(s - m_new)
        l = alpha * l + jnp.sum(p, axis=-1, keepdims=True)
        pv = _dot(p.reshape(rows, tk).astype(BF16), vt)
        acc = alpha.reshape(rows, 1) * acc + pv
        return m_new, l, acc

    init = (jnp.full((HPG, TQ, 1), NEG, F32), jnp.zeros((HPG, TQ, 1), F32),
            jnp.zeros((rows, HEAD_DIM), F32))
    _, l, acc = lax.fori_loop(t_lo, t_hi, body, init)
    return (acc / l.reshape(rows, 1)).reshape(HPG, TQ, HEAD_DIM)


def _sel_win_kernel(q_ref, ks_ref, vs_ref, kw_ref, vw_ref, sel_ref, ts_ref, tw_ref,
                    oc_ref, gates_ref, y_ref):
    i = pl.program_id(2)
    qs = q_ref[...].reshape(HPG * TQ, HEAD_DIM)
    sel = sel_ref[...]
    tiles_per_tk = TK_SEL // LANES
    blocks_per_tk = TK_SEL // SEL_BLOCK

    def sel_bias(t):
        parts = []
        for part in range(tiles_per_tk):
            delta = i - tiles_per_tk * t - part
            row = jnp.where(delta < 0, SEL_MASKED, jnp.minimum(delta, SEL_CONST))
            parts.append(ts_ref[row])
        bias = jnp.concatenate(parts, axis=-1)
        jrow = lax.broadcasted_iota(jnp.int32, (SEL_LANES, TK_SEL), 0)
        blk = blocks_per_tk * t + jnp.right_shift(
            lax.broadcasted_iota(jnp.int32, (SEL_LANES, TK_SEL), 1), 6)
        expand = (jrow == blk).astype(BF16)
        chosen = _dot(sel, expand)
        return bias + ((chosen - 1.0) * (-NEG))[None]

    n_sel_tiles = (i * TQ + TQ - 1) // TK_SEL + 1
    o_s = _flash(qs, ks_ref, vs_ref, 0, n_sel_tiles, TK_SEL, sel_bias)

    win_lo = jnp.maximum(i - WINDOW // TK_WIN, 0)
    o_w = _flash(qs, kw_ref, vw_ref, win_lo, i + 1, TK_WIN, lambda t: tw_ref[i - t])

    oc = oc_ref[...]
    gt = gates_ref[...]
    for hg in range(HPG):
        c0 = N_BRANCH * hg
        out = (gt[:, c0:c0 + 1] * oc[hg] + gt[:, c0 + 1:c0 + 2] * o_s[hg]
               + gt[:, c0 + 2:c0 + 3] * o_w[hg])
        y_ref[:, hg * HEAD_DIM:(hg + 1) * HEAD_DIM] = out.astype(y_ref.dtype)


def _sel_win_attn(q, ks, vs, kw, vw, sel, tables, o_c, gates):
    nb, _, s, _ = q.shape
    kv_spec = pl.BlockSpec((None, None, s, HEAD_DIM), lambda b, g, i: (b, g, 0, 0))
    gw = HPG * HEAD_DIM
    return pl.pallas_call(
        _sel_win_kernel,
        out_shape=jax.ShapeDtypeStruct((nb, s, NSA_HEADS * HEAD_DIM), BF16),
        grid=(nb, KV_GROUPS, s // TQ),
        in_specs=[
            pl.BlockSpec((None, HPG, TQ, HEAD_DIM), lambda b, g, i: (b, g, i, 0)),
            kv_spec, kv_spec, kv_spec, kv_spec,
            pl.BlockSpec((None, None, TQ, SEL_LANES), lambda b, g, i: (b, g, i, 0)),
            pl.BlockSpec((N_SEL_TILES, HPG, TQ, LANES), lambda b, g, i: (0, g, 0, 0)),
            pl.BlockSpec((N_WIN_TILES, HPG, TQ, LANES),
                         lambda b, g, i: (WIN_BASE // N_WIN_TILES, g, 0, 0)),
            pl.BlockSpec((None, HPG, TQ, HEAD_DIM), lambda b, g, i: (b, g, i, 0)),
            pl.BlockSpec((None, None, TQ, LANES), lambda b, g, i: (b, g, i, 0)),
        ],
        out_specs=pl.BlockSpec((None, TQ, gw), lambda b, g, i: (b, i, g)),
        compiler_params=_params(("arbitrary", "arbitrary", "arbitrary")),
        name="sel_win_attn",
    )(q, ks, vs, kw, vw, sel, tables, tables, o_c, gates)


def kernel(x, mem, mix_norm, a_w_in, a_conv_w, b_w_in, b_q_gain, kv_norm, w_kv_shared, k_gain,
           cmp_pos, cmp_w1, cmp_w2, rel_bias, mem_norm, mem_w_kv, mem_q_gain, mem_k_gain, w_out,
           mlp_norm, w_up, w_down):
    nb, s, d = x.shape
    depth = mix_norm.shape[0]
    n_a = a_w_in.shape[0]
    assert s % (LANES * CMP_STRIDE) == 0 and s // SEL_BLOCK <= SEL_LANES
    assert WIN_BASE % N_WIN_TILES == 0

    mk, mv = _mem_kv(mem, mem_norm, mem_w_kv, mem_k_gain)
    tables = _bias_tables(rel_bias)
    shared = None
    for layer in range(depth):
        if layer < n_a:
            y_tok, y_mem = _mixer_a(x, mix_norm[layer], a_w_in[layer], a_conv_w[layer],
                                    mk[layer], mv[layer], mem_q_gain[layer])
        else:
            j = layer - n_a
            ks, vs, kw, vw, kvc = shared
            q, gates, y_mem = _b_inproj(x, mix_norm[layer], b_w_in[j], b_q_gain[j],
                                        mk[layer], mv[layer], mem_q_gain[layer])
            o_c, sel = _cmp_attn(q, kvc, tables)
            y_tok = _sel_win_attn(q, ks, vs, kw, vw, sel, tables, o_c, gates)
        x = _post(x.reshape(nb * s, d), y_tok.reshape(nb * s, -1), y_mem.reshape(nb * s, -1),
                  w_out[layer], mlp_norm[layer], w_up[layer], w_down[layer]).reshape(nb, s, d)
        if layer == n_a - 1:
            ucmp, ks, vs, kw, vw = _build_kv(x, kv_norm, w_kv_shared, k_gain)
            kvc = _compress(ucmp, cmp_pos, cmp_w1, cmp_w2, k_gain[0])
            shared = (ks, vs, kw, vw, kvc)
    return x
```

```python
import functools
import math

import numpy as np
import jax
import jax.numpy as jnp
from jax import lax
from jax.experimental import pallas as pl
from jax.experimental.pallas import tpu as pltpu

F32 = jnp.float32
BF16 = jnp.bfloat16

HEAD_DIM = 64
MEM_HEADS = 4
MEM_WIDTH = MEM_HEADS * HEAD_DIM
CONV_CH = 768
NSA_HEADS = 12
KV_GROUPS = 2
HPG = NSA_HEADS // KV_GROUPS
N_BRANCH = 3
CMP_LEN = 32
CMP_STRIDE = 16
SEL_BLOCK = 64
SEL_TOPK = 16
N_LOCAL_SEL = 2
WINDOW = 512
REL_BUCKETS = 32
REL_MAX_DIST = 1024
EPS = 1e-6
NEG = -1e30
FORCE_SCORE = 1e4
LOG2E = math.log2(math.e)
MASK_BIG = 2.0 ** 100

LANES = 128
VMEM_LIMIT_BYTES = 56 * 1024 * 1024

TQ = 128
TK_SEL = 256
SEL_LANES = 128
TM = 512

N_SEL_NEAR = 8
SEL_CONST = N_SEL_NEAR
SEL_MASKED = N_SEL_NEAR + 1
WIN_BASE = N_SEL_NEAR + 2
N_WIN_NEAR = WINDOW // LANES + 1
CMP_BASE = 16
N_CMP_NEAR = 23
CMP_CONST = N_CMP_NEAR
CMP_MASKED = N_CMP_NEAR + 1
N_CMP_TILES = N_CMP_NEAR + 2
N_TILES = CMP_BASE + N_CMP_TILES


def _rms(xf, g):
    ms = jnp.mean(xf * xf, axis=-1, keepdims=True)
    return xf * lax.rsqrt(ms + EPS) * g


def _dot(a, b):
    return jnp.dot(a, b, preferred_element_type=F32)


def _dot_nt(a, b):
    return lax.dot_general(a, b, (((1,), (1,)), ((), ())), preferred_element_type=F32)


def _params(sem):
    return pltpu.CompilerParams(dimension_semantics=sem, vmem_limit_bytes=VMEM_LIMIT_BYTES)


def _mem_kv_kernel(mem_ref, mnorm_ref, w_ref, kg_ref, mk_ref, mv_ref, *, nb, ml):
    mh = _rms(mem_ref[...], mnorm_ref[...]).astype(BF16)
    kv = _dot(mh, w_ref[...])
    for h in range(MEM_HEADS):
        kh = _rms(kv[:, h * HEAD_DIM:(h + 1) * HEAD_DIM], kg_ref[...]).astype(BF16)
        vh = kv[:, MEM_WIDTH + h * HEAD_DIM:MEM_WIDTH + (h + 1) * HEAD_DIM].astype(BF16)
        for b in range(nb):
            mk_ref[b, h] = kh[b * ml:(b + 1) * ml]
            mv_ref[b, h] = vh[b * ml:(b + 1) * ml]


def _mem_kv(mem, mem_norm, mem_w_kv, mem_k_gain):
    nb, ml, d = mem.shape
    depth = mem_w_kv.shape[0]
    out = jax.ShapeDtypeStruct((depth, nb, MEM_HEADS, ml, HEAD_DIM), BF16)
    return pl.pallas_call(
        functools.partial(_mem_kv_kernel, nb=nb, ml=ml),
        out_shape=(out, out),
        grid=(depth,),
        in_specs=[
            pl.BlockSpec((nb * ml, d), lambda l: (0, 0)),
            pl.BlockSpec((1, d), lambda l: (0, 0)),
            pl.BlockSpec((None, d, 2 * MEM_WIDTH), lambda l: (l, 0, 0)),
            pl.BlockSpec((None, 1, HEAD_DIM), lambda l: (l, 0, 0)),
        ],
        out_specs=(
            pl.BlockSpec((None, nb, MEM_HEADS, ml, HEAD_DIM), lambda l: (l, 0, 0, 0, 0)),
            pl.BlockSpec((None, nb, MEM_HEADS, ml, HEAD_DIM), lambda l: (l, 0, 0, 0, 0)),
        ),
        compiler_params=_params(("arbitrary",)),
        name="mem_kv",
    )(mem.reshape(nb * ml, d), mem_norm.reshape(1, d), mem_w_kv.astype(BF16),
      mem_k_gain.reshape(depth, 1, HEAD_DIM))


def _mem_attn(qm, mk_ref, mv_ref, qg, ymem_ref):
    for h in range(MEM_HEADS):
        qh = _rms(qm[:, h * HEAD_DIM:(h + 1) * HEAD_DIM], qg) * (HEAD_DIM ** -0.5)
        lg = _dot_nt(qh.astype(BF16), mk_ref[h])
        m = jnp.max(lg, axis=-1, keepdims=True)
        e = jnp.exp(lg - m)
        l = jnp.sum(e, axis=-1, keepdims=True)
        o = _dot(e.astype(BF16), mv_ref[h]) / l
        ymem_ref[:, h * HEAD_DIM:(h + 1) * HEAD_DIM] = o.astype(ymem_ref.dtype)


def _bucket_np(d):
    n = np.maximum(d, 0)
    max_exact = REL_BUCKETS // 2
    nf = np.maximum(n, 1).astype(np.float64)
    large = max_exact + (np.log(nf / max_exact) / math.log(REL_MAX_DIST / max_exact)
                         * (REL_BUCKETS - max_exact)).astype(np.int32)
    large = np.minimum(large, REL_BUCKETS - 1)
    return np.where(n < max_exact, n, large).astype(np.int32)


def _bucket_tiles():
    r = np.arange(TQ)[:, None]
    k = np.arange(LANES)[None, :]
    tiles = []
    for delta in range(N_SEL_NEAR):
        d = delta * LANES + r - k
        tiles.append(np.where(d >= 0, _bucket_np(d), -1).T)
    tiles.append(np.full((TQ, LANES), REL_BUCKETS - 1))
    tiles.append(np.full((TQ, LANES), -1))
    for delta in range(N_WIN_NEAR):
        d = delta * LANES + r - k
        tiles.append(np.where((d >= 0) & (d < WINDOW), _bucket_np(d), -1).T)
    while len(tiles) < CMP_BASE:
        tiles.append(np.full((TQ, LANES), -1))
    for delta in range(N_CMP_NEAR):
        d = delta * LANES + r - CMP_STRIDE * k - (CMP_LEN - 1)
        tiles.append(np.where(d >= 0, _bucket_np(d), -1))
    tiles.append(np.full((TQ, LANES), REL_BUCKETS - 1))
    tiles.append(np.full((TQ, LANES), -1))
    return np.stack(tiles).astype(np.int32)


def _bias_tables_kernel(tab_ref, ids_ref, out_ref):
    ids = ids_ref[...]
    relative = pl.program_id(0) < CMP_BASE
    for h in range(NSA_HEADS):
        acc = jnp.zeros(ids.shape, F32)
        for b in range(REL_BUCKETS):
            acc = jnp.where(ids == b, tab_ref[b, h], acc)
        acc = acc - jnp.where(relative, tab_ref[REL_BUCKETS - 1, h], 0.0)
        out_ref[h] = jnp.where(ids < 0, NEG, acc * LOG2E)


def _bias_tables(rel_bias):
    ids = jnp.asarray(_bucket_tiles())
    return pl.pallas_call(
        _bias_tables_kernel,
        out_shape=jax.ShapeDtypeStruct((N_TILES, NSA_HEADS, TQ, LANES), F32),
        grid=(N_TILES,),
        in_specs=[
            pl.BlockSpec(memory_space=pltpu.SMEM),
            pl.BlockSpec((None, TQ, LANES), lambda t: (t, 0, 0)),
        ],
        out_specs=pl.BlockSpec((None, NSA_HEADS, TQ, LANES), lambda t: (t, 0, 0, 0)),
        compiler_params=_params(("arbitrary",)),
        name="bias_tables",
    )(rel_bias.astype(F32), ids)


def _mixer_a_kernel(x_ref, g_ref, w_ref, cw_ref, mk_ref, mv_ref, qg_ref,
                    ytok_ref, ymem_ref, carry_ref, *, tm):
    @pl.when(pl.program_id(1) == 0)
    def _():
        carry_ref[...] = jnp.zeros_like(carry_ref)

    h = _rms(x_ref[...], g_ref[...]).astype(BF16)
    z = _dot(h, w_ref[...])
    gate_b = z[:, :CONV_CH]
    v = z[:, CONV_CH:2 * CONV_CH] * z[:, 2 * CONV_CH:3 * CONV_CH]
    prev = carry_ref[...]
    row = lax.broadcasted_iota(jnp.int32, (tm, CONV_CH), 0)
    v1 = jnp.where(row == 0, prev[7:8, :], pltpu.roll(v, 1, 0))
    v2 = jnp.where(row == 0, prev[6:7, :], jnp.where(row == 1, prev[7:8, :], pltpu.roll(v, 2, 0)))
    cw = cw_ref[...]
    y = gate_b * (cw[0:1, :] * v2 + cw[1:2, :] * v1 + cw[2:3, :] * v)
    carry_ref[...] = v[tm - 8:, :]
    ytok_ref[...] = y.astype(ytok_ref.dtype)
    _mem_attn(z[:, 3 * CONV_CH:], mk_ref, mv_ref, qg_ref[...], ymem_ref)


def _mixer_a(x, norm_g, w_in, conv_w, mk, mv, mem_q_gain):
    nb, s, d = x.shape
    ml = mk.shape[2]
    n_in = w_in.shape[1]
    return pl.pallas_call(
        functools.partial(_mixer_a_kernel, tm=TM),
        out_shape=(jax.ShapeDtypeStruct((nb, s, CONV_CH), BF16),
                   jax.ShapeDtypeStruct((nb, s, MEM_WIDTH), BF16)),
        grid=(nb, s // TM),
        in_specs=[
            pl.BlockSpec((None, TM, d), lambda b, i: (b, i, 0)),
            pl.BlockSpec((1, d), lambda b, i: (0, 0)),
            pl.BlockSpec((d, n_in), lambda b, i: (0, 0)),
            pl.BlockSpec((3, CONV_CH), lambda b, i: (0, 0)),
            pl.BlockSpec((None, MEM_HEADS, ml, HEAD_DIM), lambda b, i: (b, 0, 0, 0)),
            pl.BlockSpec((None, MEM_HEADS, ml, HEAD_DIM), lambda b, i: (b, 0, 0, 0)),
            pl.BlockSpec((1, HEAD_DIM), lambda b, i: (0, 0)),
        ],
        out_specs=(
            pl.BlockSpec((None, TM, CONV_CH), lambda b, i: (b, i, 0)),
            pl.BlockSpec((None, TM, MEM_WIDTH), lambda b, i: (b, i, 0)),
        ),
        scratch_shapes=[pltpu.VMEM((8, CONV_CH), F32)],
        compiler_params=_params(("arbitrary", "arbitrary")),
        name="mixer_a",
    )(x, norm_g.reshape(1, d), w_in.astype(BF16), conv_w, mk, mv, mem_q_gain.reshape(1, HEAD_DIM))


B_GATE0 = NSA_HEADS * HEAD_DIM
B_QMEM0 = B_GATE0 + KV_GROUPS * LANES
B_WIDTH = B_QMEM0 + MEM_WIDTH


def _b_inproj_kernel(x_ref, g_ref, w_ref, qg_ref, mk_ref, mv_ref, mqg_ref,
                     q_ref, gates_ref, ymem_ref):
    h = _rms(x_ref[...], g_ref[...]).astype(BF16)
    z = _dot(h, w_ref[...])
    qg = qg_ref[...]
    for hh in range(NSA_HEADS):
        qh = _rms(z[:, hh * HEAD_DIM:(hh + 1) * HEAD_DIM], qg) * (HEAD_DIM ** -0.5 * LOG2E)
        q_ref[hh] = qh.astype(q_ref.dtype)
    for g in range(KV_GROUPS):
        gl = z[:, B_GATE0 + g * LANES:B_GATE0 + (g + 1) * LANES]
        gates_ref[g] = 1.0 / (1.0 + jnp.exp(-gl))
    _mem_attn(z[:, B_QMEM0:], mk_ref, mv_ref, mqg_ref[...], ymem_ref)


def _b_inproj(x, norm_g, w_in, q_gain, mk, mv, mem_q_gain):
    nb, s, d = x.shape
    ml = mk.shape[2]
    nq = NSA_HEADS * HEAD_DIM
    ng = HPG * N_BRANCH
    pad = jnp.zeros((d, LANES - ng), w_in.dtype)
    w = jnp.concatenate([w_in[:, :nq], w_in[:, nq:nq + ng], pad,
                         w_in[:, nq + ng:nq + 2 * ng], pad, w_in[:, nq + 2 * ng:]], axis=1)
    return pl.pallas_call(
        _b_inproj_kernel,
        out_shape=(jax.ShapeDtypeStruct((nb, NSA_HEADS, s, HEAD_DIM), BF16),
                   jax.ShapeDtypeStruct((nb, KV_GROUPS, s, LANES), F32),
                   jax.ShapeDtypeStruct((nb, s, MEM_WIDTH), BF16)),
        grid=(nb, s // TM),
        in_specs=[
            pl.BlockSpec((None, TM, d), lambda b, i: (b, i, 0)),
            pl.BlockSpec((1, d), lambda b, i: (0, 0)),
            pl.BlockSpec((d, B_WIDTH), lambda b, i: (0, 0)),
            pl.BlockSpec((1, HEAD_DIM), lambda b, i: (0, 0)),
            pl.BlockSpec((None, MEM_HEADS, ml, HEAD_DIM), lambda b, i: (b, 0, 0, 0)),
            pl.BlockSpec((None, MEM_HEADS, ml, HEAD_DIM), lambda b, i: (b, 0, 0, 0)),
            pl.BlockSpec((1, HEAD_DIM), lambda b, i: (0, 0)),
        ],
        out_specs=(
            pl.BlockSpec((None, NSA_HEADS, TM, HEAD_DIM), lambda b, i: (b, 0, i, 0)),
            pl.BlockSpec((None, KV_GROUPS, TM, LANES), lambda b, i: (b, 0, i, 0)),
            pl.BlockSpec((None, TM, MEM_WIDTH), lambda b, i: (b, i, 0)),
        ),
        compiler_params=_params(("arbitrary", "arbitrary")),
        name="b_inproj",
    )(x, norm_g.reshape(1, d), w.astype(BF16), q_gain.reshape(1, HEAD_DIM), mk, mv,
      mem_q_gain.reshape(1, HEAD_DIM))


FF_CHUNK = 1024
TM_POST = 1024


def _post_kernel(x_ref, ytok_ref, ymem_ref, woa_ref, wob_ref, g_ref, wup_ref, wdn_ref, o_ref, hm_ref):
    @pl.when(pl.program_id(1) == 0)
    def _():
        x1 = x_ref[...] + _dot(ytok_ref[...], woa_ref[...]) + _dot(ymem_ref[...], wob_ref[...])
        o_ref[...] = x1
        hm_ref[...] = _rms(x1, g_ref[...]).astype(hm_ref.dtype)

    a = _dot(hm_ref[...], wup_ref[...])
    a = jnp.square(jnp.maximum(a, 0.0)).astype(BF16)
    o_ref[...] += _dot(a, wdn_ref[...])


def _post(x, y_tok, y_mem, w_out, norm_g, w_up, w_down):
    n, d = x.shape
    ff = w_up.shape[1]
    nt = y_tok.shape[1]
    tm = min(TM_POST, n)
    const = lambda i, c: (0, 0)
    rows = lambda i, c: (i, 0)
    return pl.pallas_call(
        _post_kernel,
        out_shape=jax.ShapeDtypeStruct((n, d), F32),
        grid=(n // tm, ff // FF_CHUNK),
        in_specs=[
            pl.BlockSpec((tm, d), rows),
            pl.BlockSpec((tm, nt), rows),
            pl.BlockSpec((tm, MEM_WIDTH), rows),
            pl.BlockSpec((nt, d), const),
            pl.BlockSpec((MEM_WIDTH, d), const),
            pl.BlockSpec((1, d), const),
            pl.BlockSpec((d, FF_CHUNK), lambda i, c: (0, c)),
            pl.BlockSpec((FF_CHUNK, d), lambda i, c: (c, 0)),
        ],
        out_specs=pl.BlockSpec((tm, d), rows),
        scratch_shapes=[pltpu.VMEM((tm, d), BF16)],
        compiler_params=_params(("arbitrary", "arbitrary")),
        name="post",
    )(x, y_tok, y_mem, w_out[:nt].astype(BF16), w_out[nt:].astype(BF16), norm_g.reshape(1, d),
      w_up.astype(BF16), w_down.astype(BF16))


K_AUG = 2 * HEAD_DIM
KS_AUG = K_AUG + SEL_LANES
V_AUG = HEAD_DIM + 16


def _build_kv_kernel(x_ref, g_ref, w_ref, kg_ref, ucmp_ref, ks_ref, vs_ref, kw_ref, vw_ref):
    h = _rms(x_ref[...], g_ref[...]).astype(BF16)
    kv = _dot(h, w_ref[...])
    gw = KV_GROUPS * HEAD_DIM
    tm = kv.shape[0]
    ucmp_ref[...] = kv[:, :2 * gw].astype(ucmp_ref.dtype)
    kg = kg_ref[...]
    ones = (lax.broadcasted_iota(jnp.int32, (tm, K_AUG - HEAD_DIM), 1) < 2).astype(ks_ref.dtype)
    pos = pl.program_id(1) * tm + lax.broadcasted_iota(jnp.int32, (tm, SEL_LANES), 0)
    blk = lax.broadcasted_iota(jnp.int32, (tm, SEL_LANES), 1)
    onehot = jnp.where(blk == jnp.right_shift(pos, 6), MASK_BIG, 0.0).astype(ks_ref.dtype)
    ones_row = (lax.broadcasted_iota(jnp.int32, (V_AUG - HEAD_DIM, tm), 0) == 0).astype(vs_ref.dtype)
    for g in range(KV_GROUPS):
        sl = lambda t: kv[:, t * gw + g * HEAD_DIM:t * gw + (g + 1) * HEAD_DIM]
        ks_ref[g, :, :HEAD_DIM] = _rms(sl(2), kg[1:2, :]).astype(ks_ref.dtype)
        ks_ref[g, :, HEAD_DIM:K_AUG] = ones
        ks_ref[g, :, K_AUG:] = onehot
        vs_ref[g, :HEAD_DIM, :] = jnp.transpose(sl(3)).astype(vs_ref.dtype)
        vs_ref[g, HEAD_DIM:, :] = ones_row
        kw_ref[g, :, :HEAD_DIM] = _rms(sl(4), kg[2:3, :]).astype(kw_ref.dtype)
        kw_ref[g, :, HEAD_DIM:] = ones
        vw_ref[g, :HEAD_DIM, :] = jnp.transpose(sl(5)).astype(vw_ref.dtype)
        vw_ref[g, HEAD_DIM:, :] = ones_row


def _build_kv(x, kv_norm, w_kv, k_gain):
    nb, s, d = x.shape
    gw = KV_GROUPS * HEAD_DIM
    ks = jax.ShapeDtypeStruct((nb, KV_GROUPS, s, KS_AUG), BF16)
    kw = jax.ShapeDtypeStruct((nb, KV_GROUPS, s, K_AUG), BF16)
    vt = jax.ShapeDtypeStruct((nb, KV_GROUPS, V_AUG, s), BF16)
    ks_spec = pl.BlockSpec((None, KV_GROUPS, TM, KS_AUG), lambda b, i: (b, 0, i, 0))
    kw_spec = pl.BlockSpec((None, KV_GROUPS, TM, K_AUG), lambda b, i: (b, 0, i, 0))
    v_spec = pl.BlockSpec((None, KV_GROUPS, V_AUG, TM), lambda b, i: (b, 0, 0, i))
    return pl.pallas_call(
        _build_kv_kernel,
        out_shape=(jax.ShapeDtypeStruct((nb, s, 2 * gw), BF16), ks, vt, kw, vt),
        grid=(nb, s // TM),
        in_specs=[
            pl.BlockSpec((None, TM, d), lambda b, i: (b, i, 0)),
            pl.BlockSpec((1, d), lambda b, i: (0, 0)),
            pl.BlockSpec((d, 6 * gw), lambda b, i: (0, 0)),
            pl.BlockSpec((N_BRANCH, HEAD_DIM), lambda b, i: (0, 0)),
        ],
        out_specs=(pl.BlockSpec((None, TM, 2 * gw), lambda b, i: (b, i, 0)),
                   ks_spec, v_spec, kw_spec, v_spec),
        compiler_params=_params(("arbitrary", "arbitrary")),
        name="build_kv",
    )(x, kv_norm.reshape(1, d), w_kv.astype(BF16), k_gain)


def _compress_kernel(u_ref, w1_ref, w2_ref, pos_ref, kg_ref, out_ref, *, nc):
    half = CMP_STRIDE * HEAD_DIM
    u = u_ref[...]
    lo = _dot(u, w1_ref[:half, :])
    hi = _dot(u, w1_ref[half:, :])
    posb = _dot(jnp.broadcast_to(pos_ref[...], (8, 2 * half)).astype(BF16), w1_ref[...])[0:1, :]
    h1 = lo + pltpu.roll(hi, nc - 1, 0) + posb
    hid = 0.5 * h1 * (1.0 + jnp.tanh(math.sqrt(2.0 / math.pi) * (h1 + 0.044715 * (h1 * h1 * h1))))
    c = _dot(hid.astype(BF16), w2_ref[...])
    is_key = pl.program_id(0) == 0
    out_ref[...] = jnp.where(is_key, _rms(c, kg_ref[...]), c).astype(out_ref.dtype)


def _compress(ucmp, cmp_pos, cmp_w1, cmp_w2, k_gain0):
    nb, s, _ = ucmp.shape
    nc = s // CMP_STRIDE
    width = CMP_STRIDE * HEAD_DIM
    u = ucmp.reshape(nb, nc, CMP_STRIDE, 2, KV_GROUPS, HEAD_DIM)
    u = u.transpose(3, 0, 4, 1, 2, 5).reshape(2, nb, KV_GROUPS, nc, width)
    hidden = cmp_w1.shape[2]
    return pl.pallas_call(
        functools.partial(_compress_kernel, nc=nc),
        out_shape=jax.ShapeDtypeStruct((2, nb, KV_GROUPS, nc, HEAD_DIM), BF16),
        grid=(2, nb, KV_GROUPS),
        in_specs=[
            pl.BlockSpec((None, None, None, nc, width), lambda t, b, g: (t, b, g, 0, 0)),
            pl.BlockSpec((None, 2 * width, hidden), lambda t, b, g: (t, 0, 0)),
            pl.BlockSpec((None, hidden, HEAD_DIM), lambda t, b, g: (t, 0, 0)),
            pl.BlockSpec((None, 1, 2 * width), lambda t, b, g: (t, 0, 0)),
            pl.BlockSpec((1, HEAD_DIM), lambda t, b, g: (0, 0)),
        ],
        out_specs=pl.BlockSpec((None, None, None, nc, HEAD_DIM), lambda t, b, g: (t, b, g, 0, 0)),
        compiler_params=_params(("arbitrary", "arbitrary", "arbitrary")),
        name="compress",
    )(u, cmp_w1.astype(BF16), cmp_w2.astype(BF16), cmp_pos.reshape(2, 1, 2 * width),
      k_gain0.reshape(1, HEAD_DIM))


def _overlap_np(nc, n_sel):
    c0 = np.arange(nc)[:, None] * CMP_STRIDE
    s0 = np.arange(SEL_LANES)[None, :] * SEL_BLOCK
    ov = np.minimum(c0 + CMP_LEN, s0 + SEL_BLOCK) - np.maximum(c0, s0)
    ov = np.clip(ov, 0, None).astype(np.float32) / CMP_LEN
    ov[nc - 1:, :] = 0.0
    ov[:, n_sel:] = 0.0
    return ov


def _select_topk(score, ntop):
    lane = lax.broadcasted_iota(jnp.int32, score.shape, 1).astype(F32)

    def body(_, carry):
        sc, sel = carry
        m = jnp.max(sc, axis=-1, keepdims=True)
        first = jnp.min(jnp.where(sc == m, lane, float(SEL_LANES)), axis=-1, keepdims=True)
        pick = lane == first
        return jnp.where(pick, -jnp.inf, sc), jnp.where(pick, 1.0, sel)

    _, sel = lax.fori_loop(0, ntop, body, (score, jnp.zeros(score.shape, F32)))
    return sel


def _cmp_attn_kernel(*refs, nj, n_sel):
    q_ref, kc_ref, vc_ref, ov_ref = refs[:4]
    tab_refs = refs[4:4 + nj]
    oc_ref, sel_ref = refs[4 + nj:]
    i = pl.program_id(2)
    nc = kc_ref.shape[0]
    qs = q_ref[...].reshape(HPG * TQ, HEAD_DIM)
    s = _dot_nt(qs, kc_ref[...]).reshape(HPG, TQ, nc)
    if nj == 1:
        bias = tab_refs[0][...]
    else:
        bias = jnp.concatenate([t[...] for t in tab_refs], axis=-1)
    s = s + bias
    m = jnp.maximum(jnp.max(s, axis=-1, keepdims=True), 0.5 * NEG)
    e = jnp.exp2(s - m)
    l = jnp.sum(e, axis=-1, keepdims=True)
    p = e / jnp.where(l == 0.0, 1.0, l)
    oc_ref[...] = _dot(p.reshape(HPG * TQ, nc).astype(BF16), vc_ref[...]).reshape(HPG, TQ, HEAD_DIM)

    psum = jnp.sum(p, axis=0)
    p_hi = psum.astype(BF16)
    p_lo = (psum - p_hi.astype(F32)).astype(BF16)
    imp = _dot(p_hi, ov_ref[...]) + _dot(p_lo, ov_ref[...])

    tq = i * TQ + lax.broadcasted_iota(jnp.int32, (TQ, SEL_LANES), 0)
    j = lax.broadcasted_iota(jnp.int32, (TQ, SEL_LANES), 1)
    back = jnp.right_shift(tq, 6) - j
    forced = (j == 0) | ((back >= 0) & (back < N_LOCAL_SEL))
    score = jnp.where(back >= 0, jnp.where(forced, FORCE_SCORE, imp), NEG)
    sel = _select_topk(score, min(SEL_TOPK, n_sel))
    sel_ref[...] = jnp.where(back >= 0, sel, 0.0).astype(sel_ref.dtype)


def _cmp_attn(q, kvc, tables):
    nb, _, s, _ = q.shape
    nc = s // CMP_STRIDE
    nj = nc // LANES
    n_sel = s // SEL_BLOCK
    ov = jnp.asarray(_overlap_np(nc, n_sel), dtype=BF16)

    def tab_map(jt):
        def index_map(b, g, i):
            didx = i - (LANES * CMP_STRIDE // TQ) * jt
            row = jnp.where(didx < 0, CMP_MASKED, jnp.minimum(didx, CMP_CONST))
            return (CMP_BASE + row, g, 0, 0)
        return index_map

    kv_spec = lambda t: pl.BlockSpec((None, None, None, nc, HEAD_DIM), lambda b, g, i: (t, b, g, 0, 0))
    return pl.pallas_call(
        functools.partial(_cmp_attn_kernel, nj=nj, n_sel=n_sel),
        out_shape=(jax.ShapeDtypeStruct((nb, NSA_HEADS, s, HEAD_DIM), F32),
                   jax.ShapeDtypeStruct((nb, KV_GROUPS, s, SEL_LANES), BF16)),
        grid=(nb, KV_GROUPS, s // TQ),
        in_specs=[
            pl.BlockSpec((None, HPG, TQ, HEAD_DIM), lambda b, g, i: (b, g, i, 0)),
            kv_spec(0), kv_spec(1),
            pl.BlockSpec((nc, SEL_LANES), lambda b, g, i: (0, 0)),
        ] + [pl.BlockSpec((None, HPG, TQ, LANES), tab_map(jt)) for jt in range(nj)],
        out_specs=(
            pl.BlockSpec((None, HPG, TQ, HEAD_DIM), lambda b, g, i: (b, g, i, 0)),
            pl.BlockSpec((None, None, TQ, SEL_LANES), lambda b, g, i: (b, g, i, 0)),
        ),
        compiler_params=_params(("arbitrary", "arbitrary", "arbitrary")),
        name="cmp_attn",
    )(q, kvc, kvc, ov, *([tables] * nj))


def _flash_t(q_aug, k_ref, vt_ref, t_lo, t_hi, carry, bias_fn):
    def body(t, carry):
        m, acc = carry
        k0 = pl.multiple_of(t * TK_SEL, TK_SEL)
        s = _dot(k_ref[pl.ds(k0, TK_SEL), :], q_aug)
        if bias_fn is not None:
            s = bias_fn(t, s)
        m_new = jnp.maximum(m, jnp.max(s, axis=0, keepdims=True))
        p = jnp.exp2(s - m_new).astype(BF16)
        acc = jnp.exp2(m - m_new) * acc + _dot(vt_ref[:, pl.ds(k0, TK_SEL)], p)
        return m_new, acc

    return lax.fori_loop(t_lo, t_hi, body, carry)


def _flash_init():
    lanes = HPG * TQ
    return (jnp.full((1, lanes), NEG, F32), jnp.zeros((V_AUG, lanes), F32))


def _flash_out(acc):
    return acc[:HEAD_DIM, :] / acc[HEAD_DIM:HEAD_DIM + 1, :]


def _add_tiles(s, tile_fn):
    rows = []
    for part in range(TK_SEL // LANES):
        cols = [s[part * LANES:(part + 1) * LANES, h * TQ:(h + 1) * TQ] + tile_fn(part, h)
                for h in range(HPG)]
        rows.append(jnp.concatenate(cols, axis=1))
    return jnp.concatenate(rows, axis=0)


def _sel_win_kernel(q_ref, ks_ref, vs_ref, kw_ref, vw_ref, sel_ref, tab_ref, cvec_ref,
                    oc_ref, gates_ref, y_ref):
    i = pl.program_id(2)
    parts = TK_SEL // LANES

    q_t = jnp.concatenate([jnp.transpose(q_ref[h].astype(F32)) for h in range(HPG)], axis=1)
    c = cvec_ref[...]
    c_hi = c.astype(BF16).astype(F32)
    c_lo = c - c_hi
    arow = lax.broadcasted_iota(jnp.int32, (K_AUG - HEAD_DIM, HPG * TQ), 0)
    aug = jnp.where(arow == 0, c_hi, jnp.where(arow == 1, c_lo, 0.0))
    q_win = jnp.concatenate([q_t, aug], axis=0).astype(BF16)
    unsel_t = jnp.transpose(sel_ref[...].astype(F32)) - 1.0
    q_sel = jnp.concatenate([q_win, jnp.concatenate([unsel_t.astype(BF16)] * HPG, axis=1)], axis=0)

    def near_bias(t, s):
        def tile(part, h):
            delta = i - parts * t - part
            row = jnp.where(delta < 0, SEL_MASKED, jnp.minimum(delta, SEL_CONST))
            return tab_ref[row, h]

        return _add_tiles(s, tile)

    def win_bias(t, s):
        def tile(part, h):
            delta = i - parts * t - part
            row = jnp.where((delta < 0) | (delta >= N_WIN_NEAR), SEL_MASKED, WIN_BASE + delta)
            return tab_ref[row, h]

        return _add_tiles(s, tile)

    n_tiles = i // parts + 1
    n_far = jnp.maximum((i - (N_SEL_NEAR - 1)) // parts, 0)
    carry = _flash_t(q_sel, ks_ref, vs_ref, 0, n_far, _flash_init(), None)
    _, acc_s = _flash_t(q_sel, ks_ref, vs_ref, n_far, n_tiles, carry, near_bias)
    win_lo = jnp.maximum(i // parts - WINDOW // TK_SEL, 0)
    _, acc_w = _flash_t(q_win, kw_ref, vw_ref, win_lo, n_tiles, _flash_init(), win_bias)
    os_t = _flash_out(acc_s)
    ow_t = _flash_out(acc_w)

    oc = oc_ref[...]
    gt = gates_ref[...]
    for hg in range(HPG):
        c0 = N_BRANCH * hg
        o_s = jnp.transpose(os_t[:, hg * TQ:(hg + 1) * TQ])
        o_w = jnp.transpose(ow_t[:, hg * TQ:(hg + 1) * TQ])
        out = gt[:, c0:c0 + 1] * oc[hg] + gt[:, c0 + 1:c0 + 2] * o_s + gt[:, c0 + 2:c0 + 3] * o_w
        y_ref[:, hg * HEAD_DIM:(hg + 1) * HEAD_DIM] = out.astype(y_ref.dtype)


def _sel_win_attn(q, ks, vs_t, kw, vw_t, sel, tables, cvec, o_c, gates):
    nb, _, s, _ = q.shape
    ks_spec = pl.BlockSpec((None, None, s, KS_AUG), lambda b, g, i: (b, g, 0, 0))
    kw_spec = pl.BlockSpec((None, None, s, K_AUG), lambda b, g, i: (b, g, 0, 0))
    v_spec = pl.BlockSpec((None, None, V_AUG, s), lambda b, g, i: (b, g, 0, 0))
    gw = HPG * HEAD_DIM
    return pl.pallas_call(
        _sel_win_kernel,
        out_shape=jax.ShapeDtypeStruct((nb, s, NSA_HEADS * HEAD_DIM), BF16),
        grid=(nb, KV_GROUPS, s // TQ),
        in_specs=[
            pl.BlockSpec((None, HPG, TQ, HEAD_DIM), lambda b, g, i: (b, g, i, 0)),
            ks_spec, v_spec, kw_spec, v_spec,
            pl.BlockSpec((None, None, TQ, SEL_LANES), lambda b, g, i: (b, g, i, 0)),
            pl.BlockSpec((CMP_BASE, HPG, LANES, TQ), lambda b, g, i: (0, g, 0, 0)),
            pl.BlockSpec((None, 1, HPG * TQ), lambda b, g, i: (g, 0, 0)),
            pl.BlockSpec((None, HPG, TQ, HEAD_DIM), lambda b, g, i: (b, g, i, 0)),
            pl.BlockSpec((None, None, TQ, LANES), lambda b, g, i: (b, g, i, 0)),
        ],
        out_specs=pl.BlockSpec((None, TQ, gw), lambda b, g, i: (b, i, g)),
        compiler_params=_params(("arbitrary", "arbitrary", "arbitrary")),
        name="sel_win_attn",
    )(q, ks, vs_t, kw, vw_t, sel, tables, cvec, o_c, gates)


def kernel(x, mem, mix_norm, a_w_in, a_conv_w, b_w_in, b_q_gain, kv_norm, w_kv_shared, k_gain,
           cmp_pos, cmp_w1, cmp_w2, rel_bias, mem_norm, mem_w_kv, mem_q_gain, mem_k_gain, w_out,
           mlp_norm, w_up, w_down):
    nb, s, d = x.shape
    depth = mix_norm.shape[0]
    n_a = a_w_in.shape[0]
    assert s % (LANES * CMP_STRIDE) == 0 and s // SEL_BLOCK <= SEL_LANES
    assert WIN_BASE + N_WIN_NEAR <= CMP_BASE

    mk, mv = _mem_kv(mem, mem_norm, mem_w_kv, mem_k_gain)
    tables = _bias_tables(rel_bias)
    cvec = jnp.repeat(rel_bias[REL_BUCKETS - 1].astype(F32) * LOG2E, TQ).reshape(KV_GROUPS, 1, HPG * TQ)
    shared = None
    for layer in range(depth):
        if layer < n_a:
            y_tok, y_mem = _mixer_a(x, mix_norm[layer], a_w_in[layer], a_conv_w[layer],
                                    mk[layer], mv[layer], mem_q_gain[layer])
        else:
            j = layer - n_a
            ks, vs, kw, vw, kvc = shared
            q, gates, y_mem = _b_inproj(x, mix_norm[layer], b_w_in[j], b_q_gain[j],
                                        mk[layer], mv[layer], mem_q_gain[layer])
            o_c, sel = _cmp_attn(q, kvc, tables)
            y_tok = _sel_win_attn(q, ks, vs, kw, vw, sel, tables, cvec, o_c, gates)
        x = _post(x.reshape(nb * s, d), y_tok.reshape(nb * s, -1), y_mem.reshape(nb * s, -1),
                  w_out[layer], mlp_norm[layer], w_up[layer], w_down[layer]).reshape(nb, s, d)
        if layer == n_a - 1:
            ucmp, ks, vs, kw, vw = _build_kv(x, kv_norm, w_kv_shared, k_gain)
            kvc = _compress(ucmp, cmp_pos, cmp_w1, cmp_w2, k_gain[0])
            shared = (ks, vs, kw, vw, kvc)
    return x
```

```python
import functools
import math

import numpy as np
import jax
import jax.numpy as jnp
from jax import lax
from jax.experimental import pallas as pl
from jax.experimental.pallas import tpu as pltpu

F32 = jnp.float32
BF16 = jnp.bfloat16

HEAD_DIM = 64
MEM_HEADS = 4
MEM_WIDTH = MEM_HEADS * HEAD_DIM
CONV_CH = 768
NSA_HEADS = 12
KV_GROUPS = 2
HPG = NSA_HEADS // KV_GROUPS
N_BRANCH = 3
CMP_LEN = 32
CMP_STRIDE = 16
SEL_BLOCK = 64
SEL_TOPK = 16
N_LOCAL_SEL = 2
WINDOW = 512
REL_BUCKETS = 32
REL_MAX_DIST = 1024
EPS = 1e-6
NEG = -1e30
FORCE_SCORE = 1e4
LOG2E = math.log2(math.e)
MASK_BIG = 2.0 ** 100

LANES = 128
VMEM_LIMIT_BYTES = 56 * 1024 * 1024

TQ = 128
TK_SEL = 256
SEL_LANES = 128
TM = 512

N_SEL_NEAR = 8
SEL_CONST = N_SEL_NEAR
SEL_MASKED = N_SEL_NEAR + 1
WIN_BASE = N_SEL_NEAR + 2
N_WIN_NEAR = WINDOW // LANES + 1
CMP_BASE = 16
N_CMP_NEAR = 23
CMP_CONST = N_CMP_NEAR
CMP_MASKED = N_CMP_NEAR + 1
N_CMP_TILES = N_CMP_NEAR + 2
N_TILES = CMP_BASE + N_CMP_TILES


def _rms(xf, g):
    ms = jnp.mean(xf * xf, axis=-1, keepdims=True)
    return xf * lax.rsqrt(ms + EPS) * g


def _dot(a, b):
    return jnp.dot(a, b, preferred_element_type=F32)


def _dot_nt(a, b):
    return lax.dot_general(a, b, (((1,), (1,)), ((), ())), preferred_element_type=F32)


def _params(sem):
    return pltpu.CompilerParams(dimension_semantics=sem, vmem_limit_bytes=VMEM_LIMIT_BYTES)


def _mem_kv_kernel(mem_ref, mnorm_ref, w_ref, kg_ref, mk_ref, mv_ref, *, nb, ml):
    mh = _rms(mem_ref[...], mnorm_ref[...]).astype(BF16)
    kv = _dot(mh, w_ref[...])
    for h in range(MEM_HEADS):
        kh = _rms(kv[:, h * HEAD_DIM:(h + 1) * HEAD_DIM], kg_ref[...]).astype(BF16)
        vh = kv[:, MEM_WIDTH + h * HEAD_DIM:MEM_WIDTH + (h + 1) * HEAD_DIM].astype(BF16)
        for b in range(nb):
            mk_ref[b, h] = kh[b * ml:(b + 1) * ml]
            mv_ref[b, h] = vh[b * ml:(b + 1) * ml]


def _mem_kv(mem, mem_norm, mem_w_kv, mem_k_gain):
    nb, ml, d = mem.shape
    depth = mem_w_kv.shape[0]
    out = jax.ShapeDtypeStruct((depth, nb, MEM_HEADS, ml, HEAD_DIM), BF16)
    return pl.pallas_call(
        functools.partial(_mem_kv_kernel, nb=nb, ml=ml),
        out_shape=(out, out),
        grid=(depth,),
        in_specs=[
            pl.BlockSpec((nb * ml, d), lambda l: (0, 0)),
            pl.BlockSpec((1, d), lambda l: (0, 0)),
            pl.BlockSpec((None, d, 2 * MEM_WIDTH), lambda l: (l, 0, 0)),
            pl.BlockSpec((None, 1, HEAD_DIM), lambda l: (l, 0, 0)),
        ],
        out_specs=(
            pl.BlockSpec((None, nb, MEM_HEADS, ml, HEAD_DIM), lambda l: (l, 0, 0, 0, 0)),
            pl.BlockSpec((None, nb, MEM_HEADS, ml, HEAD_DIM), lambda l: (l, 0, 0, 0, 0)),
        ),
        compiler_params=_params(("arbitrary",)),
        name="mem_kv",
    )(mem.reshape(nb * ml, d), mem_norm.reshape(1, d), mem_w_kv.astype(BF16),
      mem_k_gain.reshape(depth, 1, HEAD_DIM))


def _mem_attn(qm, mk_ref, mv_ref, qg, ymem_ref):
    for h in range(MEM_HEADS):
        qh = _rms(qm[:, h * HEAD_DIM:(h + 1) * HEAD_DIM], qg) * (HEAD_DIM ** -0.5)
        lg = _dot_nt(qh.astype(BF16), mk_ref[h])
        m = jnp.max(lg, axis=-1, keepdims=True)
        e = jnp.exp(lg - m)
        l = jnp.sum(e, axis=-1, keepdims=True)
        o = _dot(e.astype(BF16), mv_ref[h]) / l
        ymem_ref[:, h * HEAD_DIM:(h + 1) * HEAD_DIM] = o.astype(ymem_ref.dtype)


def _bucket_np(d):
    n = np.maximum(d, 0)
    max_exact = REL_BUCKETS // 2
    nf = np.maximum(n, 1).astype(np.float64)
    large = max_exact + (np.log(nf / max_exact) / math.log(REL_MAX_DIST / max_exact)
                         * (REL_BUCKETS - max_exact)).astype(np.int32)
    large = np.minimum(large, REL_BUCKETS - 1)
    return np.where(n < max_exact, n, large).astype(np.int32)


def _bucket_tiles():
    r = np.arange(TQ)[:, None]
    k = np.arange(LANES)[None, :]
    tiles = []
    for delta in range(N_SEL_NEAR):
        d = delta * LANES + r - k
        tiles.append(np.where(d >= 0, _bucket_np(d), -1).T)
    tiles.append(np.full((TQ, LANES), REL_BUCKETS - 1))
    tiles.append(np.full((TQ, LANES), -1))
    for delta in range(N_WIN_NEAR):
        d = delta * LANES + r - k
        tiles.append(np.where((d >= 0) & (d < WINDOW), _bucket_np(d), -1).T)
    while len(tiles) < CMP_BASE:
        tiles.append(np.full((TQ, LANES), -1))
    for delta in range(N_CMP_NEAR):
        d = delta * LANES + r - CMP_STRIDE * k - (CMP_LEN - 1)
        tiles.append(np.where(d >= 0, _bucket_np(d), -1))
    tiles.append(np.full((TQ, LANES), REL_BUCKETS - 1))
    tiles.append(np.full((TQ, LANES), -1))
    return np.stack(tiles).astype(np.int32)


def _bias_tables_kernel(tab_ref, ids_ref, out_ref):
    ids = ids_ref[...]
    relative = pl.program_id(0) < CMP_BASE
    for h in range(NSA_HEADS):
        acc = jnp.zeros(ids.shape, F32)
        for b in range(REL_BUCKETS):
            acc = jnp.where(ids == b, tab_ref[b, h], acc)
        acc = acc - jnp.where(relative, tab_ref[REL_BUCKETS - 1, h], 0.0)
        out_ref[h] = jnp.where(ids < 0, NEG, acc * LOG2E)


def _bias_tables(rel_bias):
    ids = jnp.asarray(_bucket_tiles())
    return pl.pallas_call(
        _bias_tables_kernel,
        out_shape=jax.ShapeDtypeStruct((N_TILES, NSA_HEADS, TQ, LANES), F32),
        grid=(N_TILES,),
        in_specs=[
            pl.BlockSpec(memory_space=pltpu.SMEM),
            pl.BlockSpec((None, TQ, LANES), lambda t: (t, 0, 0)),
        ],
        out_specs=pl.BlockSpec((None, NSA_HEADS, TQ, LANES), lambda t: (t, 0, 0, 0)),
        compiler_params=_params(("arbitrary",)),
        name="bias_tables",
    )(rel_bias.astype(F32), ids)


def _mixer_a_kernel(x_ref, g_ref, w_ref, cw_ref, mk_ref, mv_ref, qg_ref,
                    ytok_ref, ymem_ref, carry_ref, *, tm):
    @pl.when(pl.program_id(1) == 0)
    def _():
        carry_ref[...] = jnp.zeros_like(carry_ref)

    h = _rms(x_ref[...], g_ref[...]).astype(BF16)
    z = _dot(h, w_ref[...])
    gate_b = z[:, :CONV_CH]
    v = z[:, CONV_CH:2 * CONV_CH] * z[:, 2 * CONV_CH:3 * CONV_CH]
    prev = carry_ref[...]
    row = lax.broadcasted_iota(jnp.int32, (tm, CONV_CH), 0)
    v1 = jnp.where(row == 0, prev[7:8, :], pltpu.roll(v, 1, 0))
    v2 = jnp.where(row == 0, prev[6:7, :], jnp.where(row == 1, prev[7:8, :], pltpu.roll(v, 2, 0)))
    cw = cw_ref[...]
    y = gate_b * (cw[0:1, :] * v2 + cw[1:2, :] * v1 + cw[2:3, :] * v)
    carry_ref[...] = v[tm - 8:, :]
    ytok_ref[...] = y.astype(ytok_ref.dtype)
    _mem_attn(z[:, 3 * CONV_CH:], mk_ref, mv_ref, qg_ref[...], ymem_ref)


def _mixer_a(x, norm_g, w_in, conv_w, mk, mv, mem_q_gain):
    nb, s, d = x.shape
    ml = mk.shape[2]
    n_in = w_in.shape[1]
    return pl.pallas_call(
        functools.partial(_mixer_a_kernel, tm=TM),
        out_shape=(jax.ShapeDtypeStruct((nb, s, CONV_CH), BF16),
                   jax.ShapeDtypeStruct((nb, s, MEM_WIDTH), BF16)),
        grid=(nb, s // TM),
        in_specs=[
            pl.BlockSpec((None, TM, d), lambda b, i: (b, i, 0)),
            pl.BlockSpec((1, d), lambda b, i: (0, 0)),
            pl.BlockSpec((d, n_in), lambda b, i: (0, 0)),
            pl.BlockSpec((3, CONV_CH), lambda b, i: (0, 0)),
            pl.BlockSpec((None, MEM_HEADS, ml, HEAD_DIM), lambda b, i: (b, 0, 0, 0)),
            pl.BlockSpec((None, MEM_HEADS, ml, HEAD_DIM), lambda b, i: (b, 0, 0, 0)),
            pl.BlockSpec((1, HEAD_DIM), lambda b, i: (0, 0)),
        ],
        out_specs=(
            pl.BlockSpec((None, TM, CONV_CH), lambda b, i: (b, i, 0)),
            pl.BlockSpec((None, TM, MEM_WIDTH), lambda b, i: (b, i, 0)),
        ),
        scratch_shapes=[pltpu.VMEM((8, CONV_CH), F32)],
        compiler_params=_params(("arbitrary", "arbitrary")),
        name="mixer_a",
    )(x, norm_g.reshape(1, d), w_in.astype(BF16), conv_w, mk, mv, mem_q_gain.reshape(1, HEAD_DIM))


B_GATE0 = NSA_HEADS * HEAD_DIM
B_QMEM0 = B_GATE0 + KV_GROUPS * LANES
B_WIDTH = B_QMEM0 + MEM_WIDTH


def _b_inproj_kernel(x_ref, g_ref, w_ref, qg_ref, mk_ref, mv_ref, mqg_ref,
                     q_ref, gates_ref, ymem_ref):
    h = _rms(x_ref[...], g_ref[...]).astype(BF16)
    z = _dot(h, w_ref[...])
    qg = qg_ref[...]
    for hh in range(NSA_HEADS):
        qh = _rms(z[:, hh * HEAD_DIM:(hh + 1) * HEAD_DIM], qg) * (HEAD_DIM ** -0.5 * LOG2E)
        q_ref[hh] = qh.astype(q_ref.dtype)
    for g in range(KV_GROUPS):
        gl = z[:, B_GATE0 + g * LANES:B_GATE0 + (g + 1) * LANES]
        gates_ref[g] = 1.0 / (1.0 + jnp.exp(-gl))
    _mem_attn(z[:, B_QMEM0:], mk_ref, mv_ref, mqg_ref[...], ymem_ref)


def _b_inproj(x, norm_g, w_in, q_gain, mk, mv, mem_q_gain):
    nb, s, d = x.shape
    ml = mk.shape[2]
    nq = NSA_HEADS * HEAD_DIM
    ng = HPG * N_BRANCH
    pad = jnp.zeros((d, LANES - ng), w_in.dtype)
    w = jnp.concatenate([w_in[:, :nq], w_in[:, nq:nq + ng], pad,
                         w_in[:, nq + ng:nq + 2 * ng], pad, w_in[:, nq + 2 * ng:]], axis=1)
    return pl.pallas_call(
        _b_inproj_kernel,
        out_shape=(jax.ShapeDtypeStruct((nb, NSA_HEADS, s, HEAD_DIM), BF16),
                   jax.ShapeDtypeStruct((nb, KV_GROUPS, s, LANES), F32),
                   jax.ShapeDtypeStruct((nb, s, MEM_WIDTH), BF16)),
        grid=(nb, s // TM),
        in_specs=[
            pl.BlockSpec((None, TM, d), lambda b, i: (b, i, 0)),
            pl.BlockSpec((1, d), lambda b, i: (0, 0)),
            pl.BlockSpec((d, B_WIDTH), lambda b, i: (0, 0)),
            pl.BlockSpec((1, HEAD_DIM), lambda b, i: (0, 0)),
            pl.BlockSpec((None, MEM_HEADS, ml, HEAD_DIM), lambda b, i: (b, 0, 0, 0)),
            pl.BlockSpec((None, MEM_HEADS, ml, HEAD_DIM), lambda b, i: (b, 0, 0, 0)),
            pl.BlockSpec((1, HEAD_DIM), lambda b, i: (0, 0)),
        ],
        out_specs=(
            pl.BlockSpec((None, NSA_HEADS, TM, HEAD_DIM), lambda b, i: (b, 0, i, 0)),
            pl.BlockSpec((None, KV_GROUPS, TM, LANES), lambda b, i: (b, 0, i, 0)),
            pl.BlockSpec((None, TM, MEM_WIDTH), lambda b, i: (b, i, 0)),
        ),
        compiler_params=_params(("arbitrary", "arbitrary")),
        name="b_inproj",
    )(x, norm_g.reshape(1, d), w.astype(BF16), q_gain.reshape(1, HEAD_DIM), mk, mv,
      mem_q_gain.reshape(1, HEAD_DIM))


FF_CHUNK = 1024
TM_POST = 1024


def _post_kernel(x_ref, ytok_ref, ymem_ref, woa_ref, wob_ref, g_ref, wup_ref, wdn_ref, o_ref, hm_ref):
    @pl.when(pl.program_id(1) == 0)
    def _():
        x1 = x_ref[...] + _dot(ytok_ref[...], woa_ref[...]) + _dot(ymem_ref[...], wob_ref[...])
        o_ref[...] = x1
        hm_ref[...] = _rms(x1, g_ref[...]).astype(hm_ref.dtype)

    a = _dot(hm_ref[...], wup_ref[...])
    a = jnp.square(jnp.maximum(a, 0.0)).astype(BF16)
    o_ref[...] += _dot(a, wdn_ref[...])


def _post(x, y_tok, y_mem, w_out, norm_g, w_up, w_down):
    n, d = x.shape
    ff = w_up.shape[1]
    nt = y_tok.shape[1]
    tm = min(TM_POST, n)
    const = lambda i, c: (0, 0)
    rows = lambda i, c: (i, 0)
    return pl.pallas_call(
        _post_kernel,
        out_shape=jax.ShapeDtypeStruct((n, d), F32),
        grid=(n // tm, ff // FF_CHUNK),
        in_specs=[
            pl.BlockSpec((tm, d), rows),
            pl.BlockSpec((tm, nt), rows),
            pl.BlockSpec((tm, MEM_WIDTH), rows),
            pl.BlockSpec((nt, d), const),
            pl.BlockSpec((MEM_WIDTH, d), const),
            pl.BlockSpec((1, d), const),
            pl.BlockSpec((d, FF_CHUNK), lambda i, c: (0, c)),
            pl.BlockSpec((FF_CHUNK, d), lambda i, c: (c, 0)),
        ],
        out_specs=pl.BlockSpec((tm, d), rows),
        scratch_shapes=[pltpu.VMEM((tm, d), BF16)],
        compiler_params=_params(("arbitrary", "arbitrary")),
        name="post",
    )(x, y_tok, y_mem, w_out[:nt].astype(BF16), w_out[nt:].astype(BF16), norm_g.reshape(1, d),
      w_up.astype(BF16), w_down.astype(BF16))


K_AUG = 2 * HEAD_DIM
KS_AUG = K_AUG + SEL_LANES
V_AUG = HEAD_DIM + 16


def _build_kv_kernel(x_ref, g_ref, w_ref, kg_ref, ucmp_ref, ks_ref, vs_ref, kw_ref, vw_ref):
    h = _rms(x_ref[...], g_ref[...]).astype(BF16)
    kv = _dot(h, w_ref[...])
    gw = KV_GROUPS * HEAD_DIM
    tm = kv.shape[0]
    ucmp_ref[...] = kv[:, :2 * gw].astype(ucmp_ref.dtype)
    kg = kg_ref[...]
    ones = (lax.broadcasted_iota(jnp.int32, (tm, K_AUG - HEAD_DIM), 1) < 2).astype(ks_ref.dtype)
    pos = pl.program_id(1) * tm + lax.broadcasted_iota(jnp.int32, (tm, SEL_LANES), 0)
    blk = lax.broadcasted_iota(jnp.int32, (tm, SEL_LANES), 1)
    onehot = jnp.where(blk == jnp.right_shift(pos, 6), MASK_BIG, 0.0).astype(ks_ref.dtype)
    ones_row = (lax.broadcasted_iota(jnp.int32, (V_AUG - HEAD_DIM, tm), 0) == 0).astype(vs_ref.dtype)
    for g in range(KV_GROUPS):
        sl = lambda t: kv[:, t * gw + g * HEAD_DIM:t * gw + (g + 1) * HEAD_DIM]
        ks_ref[g, :, :HEAD_DIM] = _rms(sl(2), kg[1:2, :]).astype(ks_ref.dtype)
        ks_ref[g, :, HEAD_DIM:K_AUG] = ones
        ks_ref[g, :, K_AUG:] = onehot
        vs_ref[g, :HEAD_DIM, :] = jnp.transpose(sl(3)).astype(vs_ref.dtype)
        vs_ref[g, HEAD_DIM:, :] = ones_row
        kw_ref[g, :, :HEAD_DIM] = _rms(sl(4), kg[2:3, :]).astype(kw_ref.dtype)
        kw_ref[g, :, HEAD_DIM:] = ones
        vw_ref[g, :HEAD_DIM, :] = jnp.transpose(sl(5)).astype(vw_ref.dtype)
        vw_ref[g, HEAD_DIM:, :] = ones_row


def _build_kv(x, kv_norm, w_kv, k_gain):
    nb, s, d = x.shape
    gw = KV_GROUPS * HEAD_DIM
    ks = jax.ShapeDtypeStruct((nb, KV_GROUPS, s, KS_AUG), BF16)
    kw = jax.ShapeDtypeStruct((nb, KV_GROUPS, s, K_AUG), BF16)
    vt = jax.ShapeDtypeStruct((nb, KV_GROUPS, V_AUG, s), BF16)
    ks_spec = pl.BlockSpec((None, KV_GROUPS, TM, KS_AUG), lambda b, i: (b, 0, i, 0))
    kw_spec = pl.BlockSpec((None, KV_GROUPS, TM, K_AUG), lambda b, i: (b, 0, i, 0))
    v_spec = pl.BlockSpec((None, KV_GROUPS, V_AUG, TM), lambda b, i: (b, 0, 0, i))
    return pl.pallas_call(
        _build_kv_kernel,
        out_shape=(jax.ShapeDtypeStruct((nb, s, 2 * gw), BF16), ks, vt, kw, vt),
        grid=(nb, s // TM),
        in_specs=[
            pl.BlockSpec((None, TM, d), lambda b, i: (b, i, 0)),
            pl.BlockSpec((1, d), lambda b, i: (0, 0)),
            pl.BlockSpec((d, 6 * gw), lambda b, i: (0, 0)),
            pl.BlockSpec((N_BRANCH, HEAD_DIM), lambda b, i: (0, 0)),
        ],
        out_specs=(pl.BlockSpec((None, TM, 2 * gw), lambda b, i: (b, i, 0)),
                   ks_spec, v_spec, kw_spec, v_spec),
        compiler_params=_params(("arbitrary", "arbitrary")),
        name="build_kv",
    )(x, kv_norm.reshape(1, d), w_kv.astype(BF16), k_gain)


def _compress_kernel(u_ref, w1_ref, w2_ref, pos_ref, kg_ref, out_ref, *, nc):
    half = CMP_STRIDE * HEAD_DIM
    u = u_ref[...]
    lo = _dot(u, w1_ref[:half, :])
    hi = _dot(u, w1_ref[half:, :])
    posb = _dot(jnp.broadcast_to(pos_ref[...], (8, 2 * half)).astype(BF16), w1_ref[...])[0:1, :]
    h1 = lo + pltpu.roll(hi, nc - 1, 0) + posb
    hid = 0.5 * h1 * (1.0 + jnp.tanh(math.sqrt(2.0 / math.pi) * (h1 + 0.044715 * (h1 * h1 * h1))))
    c = _dot(hid.astype(BF16), w2_ref[...])
    is_key = pl.program_id(0) == 0
    out_ref[...] = jnp.where(is_key, _rms(c, kg_ref[...]), c).astype(out_ref.dtype)


def _compress(ucmp, cmp_pos, cmp_w1, cmp_w2, k_gain0):
    nb, s, _ = ucmp.shape
    nc = s // CMP_STRIDE
    width = CMP_STRIDE * HEAD_DIM
    u = ucmp.reshape(nb, nc, CMP_STRIDE, 2, KV_GROUPS, HEAD_DIM)
    u = u.transpose(3, 0, 4, 1, 2, 5).reshape(2, nb, KV_GROUPS, nc, width)
    hidden = cmp_w1.shape[2]
    return pl.pallas_call(
        functools.partial(_compress_kernel, nc=nc),
        out_shape=jax.ShapeDtypeStruct((2, nb, KV_GROUPS, nc, HEAD_DIM), BF16),
        grid=(2, nb, KV_GROUPS),
        in_specs=[
            pl.BlockSpec((None, None, None, nc, width), lambda t, b, g: (t, b, g, 0, 0)),
            pl.BlockSpec((None, 2 * width, hidden), lambda t, b, g: (t, 0, 0)),
            pl.BlockSpec((None, hidden, HEAD_DIM), lambda t, b, g: (t, 0, 0)),
            pl.BlockSpec((None, 1, 2 * width), lambda t, b, g: (t, 0, 0)),
            pl.BlockSpec((1, HEAD_DIM), lambda t, b, g: (0, 0)),
        ],
        out_specs=pl.BlockSpec((None, None, None, nc, HEAD_DIM), lambda t, b, g: (t, b, g, 0, 0)),
        compiler_params=_params(("arbitrary", "arbitrary", "arbitrary")),
        name="compress",
    )(u, cmp_w1.astype(BF16), cmp_w2.astype(BF16), cmp_pos.reshape(2, 1, 2 * width),
      k_gain0.reshape(1, HEAD_DIM))


def _overlap_np(nc, n_sel):
    c0 = np.arange(nc)[:, None] * CMP_STRIDE
    s0 = np.arange(SEL_LANES)[None, :] * SEL_BLOCK
    ov = np.minimum(c0 + CMP_LEN, s0 + SEL_BLOCK) - np.maximum(c0, s0)
    ov = np.clip(ov, 0, None).astype(np.float32) / CMP_LEN
    ov[nc - 1:, :] = 0.0
    ov[:, n_sel:] = 0.0
    return ov


def _select_topk(score, ntop):
    lane = lax.broadcasted_iota(jnp.int32, score.shape, 1).astype(F32)

    def body(_, carry):
        sc, sel = carry
        m = jnp.max(sc, axis=-1, keepdims=True)
        first = jnp.min(jnp.where(sc == m, lane, float(SEL_LANES)), axis=-1, keepdims=True)
        pick = lane == first
        return jnp.where(pick, -jnp.inf, sc), jnp.where(pick, 1.0, sel)

    _, sel = lax.fori_loop(0, ntop, body, (score, jnp.zeros(score.shape, F32)))
    return sel


def _cmp_attn_kernel(*refs, nj, n_sel):
    q_ref, kc_ref, vc_ref, ov_ref = refs[:4]
    tab_refs = refs[4:4 + nj]
    oc_ref, sel_ref = refs[4 + nj:]
    i = pl.program_id(2)
    nc = kc_ref.shape[0]
    qs = q_ref[...].reshape(HPG * TQ, HEAD_DIM)
    s = _dot_nt(qs, kc_ref[...]).reshape(HPG, TQ, nc)
    if nj == 1:
        bias = tab_refs[0][...]
    else:
        bias = jnp.concatenate([t[...] for t in tab_refs], axis=-1)
    s = s + bias
    m = jnp.maximum(jnp.max(s, axis=-1, keepdims=True), 0.5 * NEG)
    e = jnp.exp2(s - m)
    l = jnp.sum(e, axis=-1, keepdims=True)
    p = e / jnp.where(l == 0.0, 1.0, l)
    oc_ref[...] = _dot(p.reshape(HPG * TQ, nc).astype(BF16), vc_ref[...]).reshape(HPG, TQ, HEAD_DIM)

    psum = jnp.sum(p, axis=0)
    p_hi = psum.astype(BF16)
    p_lo = (psum - p_hi.astype(F32)).astype(BF16)
    imp = _dot(p_hi, ov_ref[...]) + _dot(p_lo, ov_ref[...])

    tq = i * TQ + lax.broadcasted_iota(jnp.int32, (TQ, SEL_LANES), 0)
    j = lax.broadcasted_iota(jnp.int32, (TQ, SEL_LANES), 1)
    back = jnp.right_shift(tq, 6) - j
    forced = (j == 0) | ((back >= 0) & (back < N_LOCAL_SEL))
    score = jnp.where(back >= 0, jnp.where(forced, FORCE_SCORE, imp), NEG)
    sel = _select_topk(score, min(SEL_TOPK, n_sel))
    sel_ref[...] = jnp.where(back >= 0, sel, 0.0).astype(sel_ref.dtype)


def _cmp_attn(q, kvc, tables):
    nb, _, s, _ = q.shape
    nc = s // CMP_STRIDE
    nj = nc // LANES
    n_sel = s // SEL_BLOCK
    ov = jnp.asarray(_overlap_np(nc, n_sel), dtype=BF16)

    def tab_map(jt):
        def index_map(b, g, i):
            didx = i - (LANES * CMP_STRIDE // TQ) * jt
            row = jnp.where(didx < 0, CMP_MASKED, jnp.minimum(didx, CMP_CONST))
            return (CMP_BASE + row, g, 0, 0)
        return index_map

    kv_spec = lambda t: pl.BlockSpec((None, None, None, nc, HEAD_DIM), lambda b, g, i: (t, b, g, 0, 0))
    return pl.pallas_call(
        functools.partial(_cmp_attn_kernel, nj=nj, n_sel=n_sel),
        out_shape=(jax.ShapeDtypeStruct((nb, NSA_HEADS, s, HEAD_DIM), F32),
                   jax.ShapeDtypeStruct((nb, KV_GROUPS, s, SEL_LANES), BF16)),
        grid=(nb, KV_GROUPS, s // TQ),
        in_specs=[
            pl.BlockSpec((None, HPG, TQ, HEAD_DIM), lambda b, g, i: (b, g, i, 0)),
            kv_spec(0), kv_spec(1),
            pl.BlockSpec((nc, SEL_LANES), lambda b, g, i: (0, 0)),
        ] + [pl.BlockSpec((None, HPG, TQ, LANES), tab_map(jt)) for jt in range(nj)],
        out_specs=(
            pl.BlockSpec((None, HPG, TQ, HEAD_DIM), lambda b, g, i: (b, g, i, 0)),
            pl.BlockSpec((None, None, TQ, SEL_LANES), lambda b, g, i: (b, g, i, 0)),
        ),
        compiler_params=_params(("arbitrary", "arbitrary", "arbitrary")),
        name="cmp_attn",
    )(q, kvc, kvc, ov, *([tables] * nj))


def _flash_t(q_aug, k_ref, vt_ref, u_lo, u_hi, bias_fn, s_scr, p_scr):
    lanes = q_aug.shape[1]
    last_tile = 2 * u_hi - 1

    def scores(t):
        k0 = pl.multiple_of(t * TK_SEL, TK_SEL)
        return bias_fn(t, _dot(k_ref[pl.ds(k0, TK_SEL), :], q_aug))

    def pv(t, slot):
        k0 = pl.multiple_of(t * TK_SEL, TK_SEL)
        return _dot(vt_ref[:, pl.ds(k0, TK_SEL)], p_scr[slot])

    def softmax(slot, m):
        s = s_scr[slot]
        m_new = jnp.maximum(m, jnp.max(s, axis=0, keepdims=True))
        p_scr[slot] = jnp.exp2(s - m_new).astype(p_scr.dtype)
        return m_new, jnp.exp2(m - m_new)

    s_scr[0] = scores(2 * u_lo)
    p_scr[1] = jnp.zeros(p_scr.shape[1:], p_scr.dtype)

    def body(u, carry):
        m, alpha, acc = carry
        a = 2 * u
        acc = alpha * acc + pv(jnp.maximum(a - 1, 0), 1)
        m, alpha = softmax(0, m)
        s_scr[1] = scores(a + 1)
        acc = alpha * acc + pv(a, 0)
        m, alpha = softmax(1, m)
        s_scr[0] = scores(jnp.minimum(a + 2, last_tile))
        return m, alpha, acc

    init = (jnp.full((1, lanes), NEG, F32), jnp.ones((1, lanes), F32), jnp.zeros((V_AUG, lanes), F32))
    _, alpha, acc = lax.fori_loop(u_lo, u_hi, body, init)
    acc = alpha * acc + pv(last_tile, 1)
    return acc[:HEAD_DIM, :] / acc[HEAD_DIM:HEAD_DIM + 1, :]


def _add_tiles(s, tile_fn):
    rows = []
    for part in range(TK_SEL // LANES):
        cols = [s[part * LANES:(part + 1) * LANES, h * TQ:(h + 1) * TQ] + tile_fn(part, h)
                for h in range(HPG)]
        rows.append(jnp.concatenate(cols, axis=1))
    return jnp.concatenate(rows, axis=0)


def _sel_win_kernel(q_ref, ks_ref, vs_ref, kw_ref, vw_ref, sel_ref, tab_ref, cvec_ref,
                    oc_ref, gates_ref, y_ref, s_scr, p_scr):
    i = pl.program_id(2)
    parts = TK_SEL // LANES

    q_t = jnp.concatenate([jnp.transpose(q_ref[h].astype(F32)) for h in range(HPG)], axis=1)
    c = cvec_ref[...]
    c_hi = c.astype(BF16).astype(F32)
    c_lo = c - c_hi
    arow = lax.broadcasted_iota(jnp.int32, (K_AUG - HEAD_DIM, HPG * TQ), 0)
    aug = jnp.where(arow == 0, c_hi, jnp.where(arow == 1, c_lo, 0.0))
    q_win = jnp.concatenate([q_t, aug], axis=0).astype(BF16)
    unsel_t = jnp.transpose(sel_ref[...].astype(F32)) - 1.0
    q_sel = jnp.concatenate([q_win, jnp.concatenate([unsel_t.astype(BF16)] * HPG, axis=1)], axis=0)

    def sel_bias(t, s):
        def tile(part, h):
            delta = i - parts * t - part
            row = jnp.where(delta < 0, SEL_MASKED, jnp.minimum(delta, SEL_CONST))
            return tab_ref[row, h]

        return _add_tiles(s, tile)

    def win_bias(t, s):
        def tile(part, h):
            delta = i - parts * t - part
            row = jnp.where((delta < 0) | (delta >= N_WIN_NEAR), SEL_MASKED, WIN_BASE + delta)
            return tab_ref[row, h]

        return _add_tiles(s, tile)

    n_tiles = i // parts + 1
    n_pairs = (n_tiles + 1) // 2
    os_t = _flash_t(q_sel, ks_ref, vs_ref, 0, n_pairs, sel_bias, s_scr, p_scr)
    win_lo = jnp.maximum(i // parts - WINDOW // TK_SEL, 0)
    ow_t = _flash_t(q_win, kw_ref, vw_ref, win_lo // 2, n_pairs, win_bias, s_scr, p_scr)

    oc = oc_ref[...]
    gt = gates_ref[...]
    for hg in range(HPG):
        c0 = N_BRANCH * hg
        o_s = jnp.transpose(os_t[:, hg * TQ:(hg + 1) * TQ])
        o_w = jnp.transpose(ow_t[:, hg * TQ:(hg + 1) * TQ])
        out = gt[:, c0:c0 + 1] * oc[hg] + gt[:, c0 + 1:c0 + 2] * o_s + gt[:, c0 + 2:c0 + 3] * o_w
        y_ref[:, hg * HEAD_DIM:(hg + 1) * HEAD_DIM] = out.astype(y_ref.dtype)


def _sel_win_attn(q, ks, vs_t, kw, vw_t, sel, tables, cvec, o_c, gates):
    nb, _, s, _ = q.shape
    ks_spec = pl.BlockSpec((None, None, s, KS_AUG), lambda b, g, i: (b, g, 0, 0))
    kw_spec = pl.BlockSpec((None, None, s, K_AUG), lambda b, g, i: (b, g, 0, 0))
    v_spec = pl.BlockSpec((None, None, V_AUG, s), lambda b, g, i: (b, g, 0, 0))
    gw = HPG * HEAD_DIM
    return pl.pallas_call(
        _sel_win_kernel,
        out_shape=jax.ShapeDtypeStruct((nb, s, NSA_HEADS * HEAD_DIM), BF16),
        grid=(nb, KV_GROUPS, s // TQ),
        in_specs=[
            pl.BlockSpec((None, HPG, TQ, HEAD_DIM), lambda b, g, i: (b, g, i, 0)),
            ks_spec, v_spec, kw_spec, v_spec,
            pl.BlockSpec((None, None, TQ, SEL_LANES), lambda b, g, i: (b, g, i, 0)),
            pl.BlockSpec((CMP_BASE, HPG, LANES, TQ), lambda b, g, i: (0, g, 0, 0)),
            pl.BlockSpec((None, 1, HPG * TQ), lambda b, g, i: (g, 0, 0)),
            pl.BlockSpec((None, HPG, TQ, HEAD_DIM), lambda b, g, i: (b, g, i, 0)),
            pl.BlockSpec((None, None, TQ, LANES), lambda b, g, i: (b, g, i, 0)),
        ],
        out_specs=pl.BlockSpec((None, TQ, gw), lambda b, g, i: (b, i, g)),
        scratch_shapes=[pltpu.VMEM((2, TK_SEL, HPG * TQ), F32), pltpu.VMEM((2, TK_SEL, HPG * TQ), BF16)],
        compiler_params=_params(("arbitrary", "arbitrary", "arbitrary")),
        name="sel_win_attn",
    )(q, ks, vs_t, kw, vw_t, sel, tables, cvec, o_c, gates)


def kernel(x, mem, mix_norm, a_w_in, a_conv_w, b_w_in, b_q_gain, kv_norm, w_kv_shared, k_gain,
           cmp_pos, cmp_w1, cmp_w2, rel_bias, mem_norm, mem_w_kv, mem_q_gain, mem_k_gain, w_out,
           mlp_norm, w_up, w_down):
    nb, s, d = x.shape
    depth = mix_norm.shape[0]
    n_a = a_w_in.shape[0]
    assert s % (LANES * CMP_STRIDE) == 0 and s // SEL_BLOCK <= SEL_LANES
    assert WIN_BASE + N_WIN_NEAR <= CMP_BASE

    mk, mv = _mem_kv(mem, mem_norm, mem_w_kv, mem_k_gain)
    tables = _bias_tables(rel_bias)
    cvec = jnp.repeat(rel_bias[REL_BUCKETS - 1].astype(F32) * LOG2E, TQ).reshape(KV_GROUPS, 1, HPG * TQ)
    shared = None
    for layer in range(depth):
        if layer < n_a:
            y_tok, y_mem = _mixer_a(x, mix_norm[layer], a_w_in[layer], a_conv_w[layer],
                                    mk[layer], mv[layer], mem_q_gain[layer])
        else:
            j = layer - n_a
            ks, vs, kw, vw, kvc = shared
            q, gates, y_mem = _b_inproj(x, mix_norm[layer], b_w_in[j], b_q_gain[j],
                                        mk[layer], mv[layer], mem_q_gain[layer])
            o_c, sel = _cmp_attn(q, kvc, tables)
            y_tok = _sel_win_attn(q, ks, vs, kw, vw, sel, tables, cvec, o_c, gates)
        x = _post(x.reshape(nb * s, d), y_tok.reshape(nb * s, -1), y_mem.reshape(nb * s, -1),
                  w_out[layer], mlp_norm[layer], w_up[layer], w_down[layer]).reshape(nb, s, d)
        if layer == n_a - 1:
            ucmp, ks, vs, kw, vw = _build_kv(x, kv_norm, w_kv_shared, k_gain)
            kvc = _compress(ucmp, cmp_pos, cmp_w1, cmp_w2, k_gain[0])
            shared = (ks, vs, kw, vw, kvc)
    return x
```

```python
import functools
import math

import numpy as np
import jax
import jax.numpy as jnp
from jax import lax
from jax.experimental import pallas as pl
from jax.experimental.pallas import tpu as pltpu

F32 = jnp.float32
BF16 = jnp.bfloat16

HEAD_DIM = 64
MEM_HEADS = 4
MEM_WIDTH = MEM_HEADS * HEAD_DIM
CONV_CH = 768
NSA_HEADS = 12
KV_GROUPS = 2
HPG = NSA_HEADS // KV_GROUPS
N_BRANCH = 3
CMP_LEN = 32
CMP_STRIDE = 16
SEL_BLOCK = 64
SEL_TOPK = 16
N_LOCAL_SEL = 2
WINDOW = 512
REL_BUCKETS = 32
REL_MAX_DIST = 1024
EPS = 1e-6
NEG = -1e30
FORCE_SCORE = 1e4
LOG2E = math.log2(math.e)
MASK_BIG = 2.0 ** 100

LANES = 128
VMEM_LIMIT_BYTES = 56 * 1024 * 1024

TQ = 128
TK_SEL = 256
SEL_LANES = 128
TM = 512

N_SEL_NEAR = 8
SEL_CONST = N_SEL_NEAR
SEL_MASKED = N_SEL_NEAR + 1
WIN_BASE = N_SEL_NEAR + 2
N_WIN_NEAR = WINDOW // LANES + 1
CMP_BASE = 16
N_CMP_NEAR = 23
CMP_CONST = N_CMP_NEAR
CMP_MASKED = N_CMP_NEAR + 1
N_CMP_TILES = N_CMP_NEAR + 2
N_TILES = CMP_BASE + N_CMP_TILES


def _rms(xf, g):
    ms = jnp.mean(xf * xf, axis=-1, keepdims=True)
    return xf * lax.rsqrt(ms + EPS) * g


def _dot(a, b):
    return jnp.dot(a, b, preferred_element_type=F32)


def _dot_nt(a, b):
    return lax.dot_general(a, b, (((1,), (1,)), ((), ())), preferred_element_type=F32)


def _params(sem):
    return pltpu.CompilerParams(dimension_semantics=sem, vmem_limit_bytes=VMEM_LIMIT_BYTES)


def _mem_kv_kernel(mem_ref, mnorm_ref, w_ref, kg_ref, mk_ref, mv_ref, *, nb, ml):
    mh = _rms(mem_ref[...], mnorm_ref[...]).astype(BF16)
    kv = _dot(mh, w_ref[...])
    for h in range(MEM_HEADS):
        kh = _rms(kv[:, h * HEAD_DIM:(h + 1) * HEAD_DIM], kg_ref[...]).astype(BF16)
        vh = kv[:, MEM_WIDTH + h * HEAD_DIM:MEM_WIDTH + (h + 1) * HEAD_DIM].astype(BF16)
        for b in range(nb):
            mk_ref[b, h] = kh[b * ml:(b + 1) * ml]
            mv_ref[b, h] = vh[b * ml:(b + 1) * ml]


def _mem_kv(mem, mem_norm, mem_w_kv, mem_k_gain):
    nb, ml, d = mem.shape
    depth = mem_w_kv.shape[0]
    out = jax.ShapeDtypeStruct((depth, nb, MEM_HEADS, ml, HEAD_DIM), BF16)
    return pl.pallas_call(
        functools.partial(_mem_kv_kernel, nb=nb, ml=ml),
        out_shape=(out, out),
        grid=(depth,),
        in_specs=[
            pl.BlockSpec((nb * ml, d), lambda l: (0, 0)),
            pl.BlockSpec((1, d), lambda l: (0, 0)),
            pl.BlockSpec((None, d, 2 * MEM_WIDTH), lambda l: (l, 0, 0)),
            pl.BlockSpec((None, 1, HEAD_DIM), lambda l: (l, 0, 0)),
        ],
        out_specs=(
            pl.BlockSpec((None, nb, MEM_HEADS, ml, HEAD_DIM), lambda l: (l, 0, 0, 0, 0)),
            pl.BlockSpec((None, nb, MEM_HEADS, ml, HEAD_DIM), lambda l: (l, 0, 0, 0, 0)),
        ),
        compiler_params=_params(("arbitrary",)),
        name="mem_kv",
    )(mem.reshape(nb * ml, d), mem_norm.reshape(1, d), mem_w_kv.astype(BF16),
      mem_k_gain.reshape(depth, 1, HEAD_DIM))


def _mem_attn(qm, mk_ref, mv_ref, qg, ymem_ref):
    for h in range(MEM_HEADS):
        qh = _rms(qm[:, h * HEAD_DIM:(h + 1) * HEAD_DIM], qg) * (HEAD_DIM ** -0.5)
        lg = _dot_nt(qh.astype(BF16), mk_ref[h])
        m = jnp.max(lg, axis=-1, keepdims=True)
        e = jnp.exp(lg - m)
        l = jnp.sum(e, axis=-1, keepdims=True)
        o = _dot(e.astype(BF16), mv_ref[h]) / l
        ymem_ref[:, h * HEAD_DIM:(h + 1) * HEAD_DIM] = o.astype(ymem_ref.dtype)


def _bucket_np(d):
    n = np.maximum(d, 0)
    max_exact = REL_BUCKETS // 2
    nf = np.maximum(n, 1).astype(np.float64)
    large = max_exact + (np.log(nf / max_exact) / math.log(REL_MAX_DIST / max_exact)
                         * (REL_BUCKETS - max_exact)).astype(np.int32)
    large = np.minimum(large, REL_BUCKETS - 1)
    return np.where(n < max_exact, n, large).astype(np.int32)


def _bucket_tiles():
    r = np.arange(TQ)[:, None]
    k = np.arange(LANES)[None, :]
    tiles = []
    for delta in range(N_SEL_NEAR):
        d = delta * LANES + r - k
        tiles.append(np.where(d >= 0, _bucket_np(d), -1).T)
    tiles.append(np.full((TQ, LANES), REL_BUCKETS - 1))
    tiles.append(np.full((TQ, LANES), -1))
    for delta in range(N_WIN_NEAR):
        d = delta * LANES + r - k
        tiles.append(np.where((d >= 0) & (d < WINDOW), _bucket_np(d), -1).T)
    while len(tiles) < CMP_BASE:
        tiles.append(np.full((TQ, LANES), -1))
    for delta in range(N_CMP_NEAR):
        d = delta * LANES + r - CMP_STRIDE * k - (CMP_LEN - 1)
        tiles.append(np.where(d >= 0, _bucket_np(d), -1).T)
    tiles.append(np.full((TQ, LANES), REL_BUCKETS - 1))
    tiles.append(np.full((TQ, LANES), -1))
    return np.stack(tiles).astype(np.int32)


def _bias_tables_kernel(tab_ref, ids_ref, out_ref):
    ids = ids_ref[...]
    relative = pl.program_id(0) < CMP_BASE
    for h in range(NSA_HEADS):
        acc = jnp.zeros(ids.shape, F32)
        for b in range(REL_BUCKETS):
            acc = jnp.where(ids == b, tab_ref[b, h], acc)
        acc = acc - jnp.where(relative, tab_ref[REL_BUCKETS - 1, h], 0.0)
        out_ref[h] = jnp.where(ids < 0, NEG, acc * LOG2E)


def _bias_tables(rel_bias):
    ids = jnp.asarray(_bucket_tiles())
    return pl.pallas_call(
        _bias_tables_kernel,
        out_shape=jax.ShapeDtypeStruct((N_TILES, NSA_HEADS, TQ, LANES), F32),
        grid=(N_TILES,),
        in_specs=[
            pl.BlockSpec(memory_space=pltpu.SMEM),
            pl.BlockSpec((None, TQ, LANES), lambda t: (t, 0, 0)),
        ],
        out_specs=pl.BlockSpec((None, NSA_HEADS, TQ, LANES), lambda t: (t, 0, 0, 0)),
        compiler_params=_params(("arbitrary",)),
        name="bias_tables",
    )(rel_bias.astype(F32), ids)


def _mixer_a_kernel(x_ref, g_ref, w_ref, cw_ref, mk_ref, mv_ref, qg_ref,
                    ytok_ref, ymem_ref, carry_ref, *, tm):
    @pl.when(pl.program_id(1) == 0)
    def _():
        carry_ref[...] = jnp.zeros_like(carry_ref)

    h = _rms(x_ref[...], g_ref[...]).astype(BF16)
    z = _dot(h, w_ref[...])
    gate_b = z[:, :CONV_CH]
    v = z[:, CONV_CH:2 * CONV_CH] * z[:, 2 * CONV_CH:3 * CONV_CH]
    prev = carry_ref[...]
    row = lax.broadcasted_iota(jnp.int32, (tm, CONV_CH), 0)
    v1 = jnp.where(row == 0, prev[7:8, :], pltpu.roll(v, 1, 0))
    v2 = jnp.where(row == 0, prev[6:7, :], jnp.where(row == 1, prev[7:8, :], pltpu.roll(v, 2, 0)))
    cw = cw_ref[...]
    y = gate_b * (cw[0:1, :] * v2 + cw[1:2, :] * v1 + cw[2:3, :] * v)
    carry_ref[...] = v[tm - 8:, :]
    ytok_ref[...] = y.astype(ytok_ref.dtype)
    _mem_attn(z[:, 3 * CONV_CH:], mk_ref, mv_ref, qg_ref[...], ymem_ref)


def _mixer_a(x, norm_g, w_in, conv_w, mk, mv, mem_q_gain):
    nb, s, d = x.shape
    ml = mk.shape[2]
    n_in = w_in.shape[1]
    return pl.pallas_call(
        functools.partial(_mixer_a_kernel, tm=TM),
        out_shape=(jax.ShapeDtypeStruct((nb, s, CONV_CH), BF16),
                   jax.ShapeDtypeStruct((nb, s, MEM_WIDTH), BF16)),
        grid=(nb, s // TM),
        in_specs=[
            pl.BlockSpec((None, TM, d), lambda b, i: (b, i, 0)),
            pl.BlockSpec((1, d), lambda b, i: (0, 0)),
            pl.BlockSpec((d, n_in), lambda b, i: (0, 0)),
            pl.BlockSpec((3, CONV_CH), lambda b, i: (0, 0)),
            pl.BlockSpec((None, MEM_HEADS, ml, HEAD_DIM), lambda b, i: (b, 0, 0, 0)),
            pl.BlockSpec((None, MEM_HEADS, ml, HEAD_DIM), lambda b, i: (b, 0, 0, 0)),
            pl.BlockSpec((1, HEAD_DIM), lambda b, i: (0, 0)),
        ],
        out_specs=(
            pl.BlockSpec((None, TM, CONV_CH), lambda b, i: (b, i, 0)),
            pl.BlockSpec((None, TM, MEM_WIDTH), lambda b, i: (b, i, 0)),
        ),
        scratch_shapes=[pltpu.VMEM((8, CONV_CH), F32)],
        compiler_params=_params(("arbitrary", "arbitrary")),
        name="mixer_a",
    )(x, norm_g.reshape(1, d), w_in.astype(BF16), conv_w, mk, mv, mem_q_gain.reshape(1, HEAD_DIM))


B_GATE0 = NSA_HEADS * HEAD_DIM
B_QMEM0 = B_GATE0 + KV_GROUPS * LANES
B_WIDTH = B_QMEM0 + MEM_WIDTH


GATE_ROWS = 32


def _b_inproj_kernel(x_ref, g_ref, w_ref, qg_ref, mk_ref, mv_ref, mqg_ref,
                     qt_ref, gates_ref, ymem_ref):
    h = _rms(x_ref[...], g_ref[...]).astype(BF16)
    z = _dot(h, w_ref[...])
    qg = qg_ref[...]
    tm = z.shape[0]
    for hh in range(NSA_HEADS):
        qh = _rms(z[:, hh * HEAD_DIM:(hh + 1) * HEAD_DIM], qg) * (HEAD_DIM ** -0.5 * LOG2E)
        qh_t = jnp.transpose(qh).astype(qt_ref.dtype)
        g, hg = divmod(hh, HPG)
        for qt in range(tm // TQ):
            lane0 = (qt * HPG + hg) * TQ
            qt_ref[g, :, lane0:lane0 + TQ] = qh_t[:, qt * TQ:(qt + 1) * TQ]
    for g in range(KV_GROUPS):
        gl = z[:, B_GATE0 + g * LANES:B_GATE0 + (g + 1) * LANES]
        gates_ref[g] = jnp.transpose(1.0 / (1.0 + jnp.exp(-gl)))[:GATE_ROWS, :]
    _mem_attn(z[:, B_QMEM0:], mk_ref, mv_ref, mqg_ref[...], ymem_ref)


def _b_inproj(x, norm_g, w_in, q_gain, mk, mv, mem_q_gain):
    nb, s, d = x.shape
    ml = mk.shape[2]
    nq = NSA_HEADS * HEAD_DIM
    ng = HPG * N_BRANCH
    pad = jnp.zeros((d, LANES - ng), w_in.dtype)
    w = jnp.concatenate([w_in[:, :nq], w_in[:, nq:nq + ng], pad,
                         w_in[:, nq + ng:nq + 2 * ng], pad, w_in[:, nq + 2 * ng:]], axis=1)
    return pl.pallas_call(
        _b_inproj_kernel,
        out_shape=(jax.ShapeDtypeStruct((nb, KV_GROUPS, HEAD_DIM, HPG * s), BF16),
                   jax.ShapeDtypeStruct((nb, KV_GROUPS, GATE_ROWS, s), F32),
                   jax.ShapeDtypeStruct((nb, s, MEM_WIDTH), BF16)),
        grid=(nb, s // TM),
        in_specs=[
            pl.BlockSpec((None, TM, d), lambda b, i: (b, i, 0)),
            pl.BlockSpec((1, d), lambda b, i: (0, 0)),
            pl.BlockSpec((d, B_WIDTH), lambda b, i: (0, 0)),
            pl.BlockSpec((1, HEAD_DIM), lambda b, i: (0, 0)),
            pl.BlockSpec((None, MEM_HEADS, ml, HEAD_DIM), lambda b, i: (b, 0, 0, 0)),
            pl.BlockSpec((None, MEM_HEADS, ml, HEAD_DIM), lambda b, i: (b, 0, 0, 0)),
            pl.BlockSpec((1, HEAD_DIM), lambda b, i: (0, 0)),
        ],
        out_specs=(
            pl.BlockSpec((None, KV_GROUPS, HEAD_DIM, HPG * TM), lambda b, i: (b, 0, 0, i)),
            pl.BlockSpec((None, KV_GROUPS, GATE_ROWS, TM), lambda b, i: (b, 0, 0, i)),
            pl.BlockSpec((None, TM, MEM_WIDTH), lambda b, i: (b, i, 0)),
        ),
        compiler_params=_params(("arbitrary", "arbitrary")),
        name="b_inproj",
    )(x, norm_g.reshape(1, d), w.astype(BF16), q_gain.reshape(1, HEAD_DIM), mk, mv,
      mem_q_gain.reshape(1, HEAD_DIM))


FF_CHUNK = 1024
TM_POST = 1024


def _post_kernel(x_ref, ytok_ref, ymem_ref, woa_ref, wob_ref, g_ref, wup_ref, wdn_ref, o_ref, hm_ref):
    @pl.when(pl.program_id(1) == 0)
    def _():
        x1 = x_ref[...] + _dot(ytok_ref[...], woa_ref[...]) + _dot(ymem_ref[...], wob_ref[...])
        o_ref[...] = x1
        hm_ref[...] = _rms(x1, g_ref[...]).astype(hm_ref.dtype)

    a = _dot(hm_ref[...], wup_ref[...])
    a = jnp.square(jnp.maximum(a, 0.0)).astype(BF16)
    o_ref[...] += _dot(a, wdn_ref[...])


def _post(x, y_tok, y_mem, w_out, norm_g, w_up, w_down):
    n, d = x.shape
    ff = w_up.shape[1]
    nt = y_tok.shape[1]
    tm = min(TM_POST, n)
    const = lambda i, c: (0, 0)
    rows = lambda i, c: (i, 0)
    return pl.pallas_call(
        _post_kernel,
        out_shape=jax.ShapeDtypeStruct((n, d), F32),
        grid=(n // tm, ff // FF_CHUNK),
        in_specs=[
            pl.BlockSpec((tm, d), rows),
            pl.BlockSpec((tm, nt), rows),
            pl.BlockSpec((tm, MEM_WIDTH), rows),
            pl.BlockSpec((nt, d), const),
            pl.BlockSpec((MEM_WIDTH, d), const),
            pl.BlockSpec((1, d), const),
            pl.BlockSpec((d, FF_CHUNK), lambda i, c: (0, c)),
            pl.BlockSpec((FF_CHUNK, d), lambda i, c: (c, 0)),
        ],
        out_specs=pl.BlockSpec((tm, d), rows),
        scratch_shapes=[pltpu.VMEM((tm, d), BF16)],
        compiler_params=_params(("arbitrary", "arbitrary")),
        name="post",
    )(x, y_tok, y_mem, w_out[:nt].astype(BF16), w_out[nt:].astype(BF16), norm_g.reshape(1, d),
      w_up.astype(BF16), w_down.astype(BF16))


K_AUG = 2 * HEAD_DIM
KS_AUG = K_AUG + SEL_LANES
V_AUG = HEAD_DIM + 16


def _build_kv_kernel(x_ref, g_ref, w_ref, kg_ref, ucmp_ref, ks_ref, vs_ref, kw_ref, vw_ref):
    h = _rms(x_ref[...], g_ref[...]).astype(BF16)
    kv = _dot(h, w_ref[...])
    gw = KV_GROUPS * HEAD_DIM
    tm = kv.shape[0]
    ucmp_ref[...] = kv[:, :2 * gw].astype(ucmp_ref.dtype)
    kg = kg_ref[...]
    ones = (lax.broadcasted_iota(jnp.int32, (tm, K_AUG - HEAD_DIM), 1) < 2).astype(ks_ref.dtype)
    pos = pl.program_id(1) * tm + lax.broadcasted_iota(jnp.int32, (tm, SEL_LANES), 0)
    blk = lax.broadcasted_iota(jnp.int32, (tm, SEL_LANES), 1)
    onehot = jnp.where(blk == jnp.right_shift(pos, 6), MASK_BIG, 0.0).astype(ks_ref.dtype)
    ones_row = (lax.broadcasted_iota(jnp.int32, (V_AUG - HEAD_DIM, tm), 0) == 0).astype(vs_ref.dtype)
    for g in range(KV_GROUPS):
        sl = lambda t: kv[:, t * gw + g * HEAD_DIM:t * gw + (g + 1) * HEAD_DIM]
        ks_ref[g, :, :HEAD_DIM] = _rms(sl(2), kg[1:2, :]).astype(ks_ref.dtype)
        ks_ref[g, :, HEAD_DIM:K_AUG] = ones
        ks_ref[g, :, K_AUG:] = onehot
        vs_ref[g, :HEAD_DIM, :] = jnp.transpose(sl(3)).astype(vs_ref.dtype)
        vs_ref[g, HEAD_DIM:, :] = ones_row
        kw_ref[g, :, :HEAD_DIM] = _rms(sl(4), kg[2:3, :]).astype(kw_ref.dtype)
        kw_ref[g, :, HEAD_DIM:] = ones
        vw_ref[g, :HEAD_DIM, :] = jnp.transpose(sl(5)).astype(vw_ref.dtype)
        vw_ref[g, HEAD_DIM:, :] = ones_row


def _build_kv(x, kv_norm, w_kv, k_gain):
    nb, s, d = x.shape
    gw = KV_GROUPS * HEAD_DIM
    ks = jax.ShapeDtypeStruct((nb, KV_GROUPS, s, KS_AUG), BF16)
    kw = jax.ShapeDtypeStruct((nb, KV_GROUPS, s, K_AUG), BF16)
    vt = jax.ShapeDtypeStruct((nb, KV_GROUPS, V_AUG, s), BF16)
    ks_spec = pl.BlockSpec((None, KV_GROUPS, TM, KS_AUG), lambda b, i: (b, 0, i, 0))
    kw_spec = pl.BlockSpec((None, KV_GROUPS, TM, K_AUG), lambda b, i: (b, 0, i, 0))
    v_spec = pl.BlockSpec((None, KV_GROUPS, V_AUG, TM), lambda b, i: (b, 0, 0, i))
    return pl.pallas_call(
        _build_kv_kernel,
        out_shape=(jax.ShapeDtypeStruct((nb, s, 2 * gw), BF16), ks, vt, kw, vt),
        grid=(nb, s // TM),
        in_specs=[
            pl.BlockSpec((None, TM, d), lambda b, i: (b, i, 0)),
            pl.BlockSpec((1, d), lambda b, i: (0, 0)),
            pl.BlockSpec((d, 6 * gw), lambda b, i: (0, 0)),
            pl.BlockSpec((N_BRANCH, HEAD_DIM), lambda b, i: (0, 0)),
        ],
        out_specs=(pl.BlockSpec((None, TM, 2 * gw), lambda b, i: (b, i, 0)),
                   ks_spec, v_spec, kw_spec, v_spec),
        compiler_params=_params(("arbitrary", "arbitrary")),
        name="build_kv",
    )(x, kv_norm.reshape(1, d), w_kv.astype(BF16), k_gain)


def _compress_kernel(u_ref, w1_ref, w2_ref, pos_ref, kg_ref, out_ref, out_t_ref, *, nc):
    half = CMP_STRIDE * HEAD_DIM
    u = u_ref[...]
    lo = _dot(u, w1_ref[:half, :])
    hi = _dot(u, w1_ref[half:, :])
    posb = _dot(jnp.broadcast_to(pos_ref[...], (8, 2 * half)).astype(BF16), w1_ref[...])[0:1, :]
    h1 = lo + pltpu.roll(hi, nc - 1, 0) + posb
    hid = 0.5 * h1 * (1.0 + jnp.tanh(math.sqrt(2.0 / math.pi) * (h1 + 0.044715 * (h1 * h1 * h1))))
    c = _dot(hid.astype(BF16), w2_ref[...])
    is_key = pl.program_id(0) == 0
    out = jnp.where(is_key, _rms(c, kg_ref[...]), c)
    out_ref[...] = out.astype(out_ref.dtype)
    out_t_ref[...] = jnp.transpose(out).astype(out_t_ref.dtype)


def _compress(ucmp, cmp_pos, cmp_w1, cmp_w2, k_gain0):
    nb, s, _ = ucmp.shape
    nc = s // CMP_STRIDE
    width = CMP_STRIDE * HEAD_DIM
    u = ucmp.reshape(nb, nc, CMP_STRIDE, 2, KV_GROUPS, HEAD_DIM)
    u = u.transpose(3, 0, 4, 1, 2, 5).reshape(2, nb, KV_GROUPS, nc, width)
    hidden = cmp_w1.shape[2]
    return pl.pallas_call(
        functools.partial(_compress_kernel, nc=nc),
        out_shape=(jax.ShapeDtypeStruct((2, nb, KV_GROUPS, nc, HEAD_DIM), BF16),
                   jax.ShapeDtypeStruct((2, nb, KV_GROUPS, HEAD_DIM, nc), BF16)),
        grid=(2, nb, KV_GROUPS),
        in_specs=[
            pl.BlockSpec((None, None, None, nc, width), lambda t, b, g: (t, b, g, 0, 0)),
            pl.BlockSpec((None, 2 * width, hidden), lambda t, b, g: (t, 0, 0)),
            pl.BlockSpec((None, hidden, HEAD_DIM), lambda t, b, g: (t, 0, 0)),
            pl.BlockSpec((None, 1, 2 * width), lambda t, b, g: (t, 0, 0)),
            pl.BlockSpec((1, HEAD_DIM), lambda t, b, g: (0, 0)),
        ],
        out_specs=(pl.BlockSpec((None, None, None, nc, HEAD_DIM), lambda t, b, g: (t, b, g, 0, 0)),
                   pl.BlockSpec((None, None, None, HEAD_DIM, nc), lambda t, b, g: (t, b, g, 0, 0))),
        compiler_params=_params(("arbitrary", "arbitrary", "arbitrary")),
        name="compress",
    )(u, cmp_w1.astype(BF16), cmp_w2.astype(BF16), cmp_pos.reshape(2, 1, 2 * width),
      k_gain0.reshape(1, HEAD_DIM))


def _overlap_np(nc, n_sel):
    c0 = np.arange(nc)[:, None] * CMP_STRIDE
    s0 = np.arange(SEL_LANES)[None, :] * SEL_BLOCK
    ov = np.minimum(c0 + CMP_LEN, s0 + SEL_BLOCK) - np.maximum(c0, s0)
    ov = np.clip(ov, 0, None).astype(np.float32) / CMP_LEN
    ov[nc - 1:, :] = 0.0
    ov[:, n_sel:] = 0.0
    return ov


def _select_topk(score, ntop):
    blk = lax.broadcasted_iota(jnp.int32, score.shape, 0).astype(F32)
    sel = jnp.zeros(score.shape, F32)
    for _ in range(ntop):
        m = jnp.max(score, axis=0, keepdims=True)
        first = jnp.min(jnp.where(score == m, blk, float(SEL_LANES)), axis=0, keepdims=True)
        pick = blk == first
        score = jnp.where(pick, -jnp.inf, score)
        sel = jnp.where(pick, 1.0, sel)
    return sel


def _cmp_attn_kernel(*refs, nj, n_sel):
    qt_ref, kc_ref, vct_ref, ovt_ref = refs[:4]
    tab_refs = refs[4:4 + nj]
    oct_ref, unsel_ref = refs[4 + nj:]
    i = pl.program_id(2)
    s = _dot(kc_ref[...], qt_ref[...])
    s = jnp.concatenate(
        [jnp.concatenate([s[jt * LANES:(jt + 1) * LANES, h * TQ:(h + 1) * TQ] + tab_refs[jt][h]
                          for h in range(HPG)], axis=1) for jt in range(nj)], axis=0)
    m = jnp.maximum(jnp.max(s, axis=0, keepdims=True), 0.5 * NEG)
    e = jnp.exp2(s - m)
    l = jnp.sum(e, axis=0, keepdims=True)
    p = e * (1.0 / jnp.where(l == 0.0, 1.0, l))
    oct_ref[...] = _dot(vct_ref[...], p.astype(BF16))

    psum = p[:, :TQ]
    for h in range(1, HPG):
        psum = psum + p[:, h * TQ:(h + 1) * TQ]
    p_hi = psum.astype(BF16)
    p_lo = (psum - p_hi.astype(F32)).astype(BF16)
    imp_t = _dot(ovt_ref[...], p_hi) + _dot(ovt_ref[...], p_lo)

    tq = i * TQ + lax.broadcasted_iota(jnp.int32, (SEL_LANES, TQ), 1)
    j = lax.broadcasted_iota(jnp.int32, (SEL_LANES, TQ), 0)
    back = jnp.right_shift(tq, 6) - j
    forced = (j == 0) | ((back >= 0) & (back < N_LOCAL_SEL))
    score = jnp.where(back >= 0, jnp.where(forced, FORCE_SCORE, imp_t), NEG)
    sel = _select_topk(score, min(SEL_TOPK, n_sel))
    unsel_ref[...] = (jnp.where(back >= 0, sel, 0.0) - 1.0).astype(unsel_ref.dtype)


def _cmp_attn(q_t, kvc, kvc_t, tables):
    nb = q_t.shape[0]
    s = q_t.shape[3] // HPG
    nc = s // CMP_STRIDE
    nj = nc // LANES
    n_sel = s // SEL_BLOCK
    ov_t = jnp.asarray(_overlap_np(nc, n_sel).T, dtype=BF16)

    def tab_map(jt):
        def index_map(b, g, i):
            didx = i - (LANES * CMP_STRIDE // TQ) * jt
            row = jnp.where(didx < 0, CMP_MASKED, jnp.minimum(didx, CMP_CONST))
            return (CMP_BASE + row, g, 0, 0)
        return index_map

    qt_spec = pl.BlockSpec((None, None, HEAD_DIM, HPG * TQ), lambda b, g, i: (b, g, 0, i))
    return pl.pallas_call(
        functools.partial(_cmp_attn_kernel, nj=nj, n_sel=n_sel),
        out_shape=(jax.ShapeDtypeStruct((nb, KV_GROUPS, HEAD_DIM, HPG * s), F32),
                   jax.ShapeDtypeStruct((nb, KV_GROUPS, SEL_LANES, s), BF16)),
        grid=(nb, KV_GROUPS, s // TQ),
        in_specs=[
            qt_spec,
            pl.BlockSpec((None, None, None, nc, HEAD_DIM), lambda b, g, i: (0, b, g, 0, 0)),
            pl.BlockSpec((None, None, None, HEAD_DIM, nc), lambda b, g, i: (1, b, g, 0, 0)),
            pl.BlockSpec((SEL_LANES, nc), lambda b, g, i: (0, 0)),
        ] + [pl.BlockSpec((None, HPG, LANES, TQ), tab_map(jt)) for jt in range(nj)],
        out_specs=(
            qt_spec,
            pl.BlockSpec((None, None, SEL_LANES, TQ), lambda b, g, i: (b, g, 0, i)),
        ),
        compiler_params=_params(("arbitrary", "arbitrary", "arbitrary")),
        name="cmp_attn",
    )(q_t, kvc, kvc_t, ov_t, *([tables] * nj))


def _flash_t(q_aug, k_ref, vt_ref, u_lo, u_hi, bias_fn, s_scr, p_scr):
    lanes = q_aug.shape[1]
    last_tile = 2 * u_hi - 1

    def scores(t):
        k0 = pl.multiple_of(t * TK_SEL, TK_SEL)
        return bias_fn(t, _dot(k_ref[pl.ds(k0, TK_SEL), :], q_aug))

    def pv(t, slot):
        k0 = pl.multiple_of(t * TK_SEL, TK_SEL)
        return _dot(vt_ref[:, pl.ds(k0, TK_SEL)], p_scr[slot])

    def softmax(slot, m):
        s = s_scr[slot]
        m_new = jnp.maximum(m, jnp.max(s, axis=0, keepdims=True))
        p_scr[slot] = jnp.exp2(s - m_new).astype(p_scr.dtype)
        return m_new, jnp.exp2(m - m_new)

    s_scr[0] = scores(2 * u_lo)
    p_scr[1] = jnp.zeros(p_scr.shape[1:], p_scr.dtype)

    def body(u, carry):
        m, alpha, acc = carry
        a = 2 * u
        acc = alpha * acc + pv(jnp.maximum(a - 1, 0), 1)
        m, alpha = softmax(0, m)
        s_scr[1] = scores(a + 1)
        acc = alpha * acc + pv(a, 0)
        m, alpha = softmax(1, m)
        s_scr[0] = scores(jnp.minimum(a + 2, last_tile))
        return m, alpha, acc

    init = (jnp.full((1, lanes), NEG, F32), jnp.ones((1, lanes), F32), jnp.zeros((V_AUG, lanes), F32))
    _, alpha, acc = lax.fori_loop(u_lo, u_hi, body, init)
    acc = alpha * acc + pv(last_tile, 1)
    return acc[:HEAD_DIM, :] / acc[HEAD_DIM:HEAD_DIM + 1, :]


def _add_tiles(s, tile_fn):
    rows = []
    for part in range(TK_SEL // LANES):
        cols = [s[part * LANES:(part + 1) * LANES, h * TQ:(h + 1) * TQ] + tile_fn(part, h)
                for h in range(HPG)]
        rows.append(jnp.concatenate(cols, axis=1))
    return jnp.concatenate(rows, axis=0)


def _sel_win_kernel(qt_ref, ks_ref, vs_ref, kw_ref, vw_ref, unsel_ref, tab_ref, cvec_ref,
                    oct_ref, gates_ref, y_ref, s_scr, p_scr):
    i = pl.program_id(2)
    parts = TK_SEL // LANES

    c = cvec_ref[...]
    c_hi = c.astype(BF16).astype(F32)
    c_lo = c - c_hi
    arow = lax.broadcasted_iota(jnp.int32, (K_AUG - HEAD_DIM, HPG * TQ), 0)
    aug = jnp.where(arow == 0, c_hi, jnp.where(arow == 1, c_lo, 0.0)).astype(BF16)
    q_win = jnp.concatenate([qt_ref[...], aug], axis=0)
    unsel_t = unsel_ref[...]
    q_sel = jnp.concatenate([q_win, jnp.concatenate([unsel_t] * HPG, axis=1)], axis=0)

    def sel_bias(t, s):
        def tile(part, h):
            delta = i - parts * t - part
            row = jnp.where(delta < 0, SEL_MASKED, jnp.minimum(delta, SEL_CONST))
            return tab_ref[row, h]

        return _add_tiles(s, tile)

    def win_bias(t, s):
        def tile(part, h):
            delta = i - parts * t - part
            row = jnp.where((delta < 0) | (delta >= N_WIN_NEAR), SEL_MASKED, WIN_BASE + delta)
            return tab_ref[row, h]

        return _add_tiles(s, tile)

    n_tiles = i // parts + 1
    n_pairs = (n_tiles + 1) // 2
    os_t = _flash_t(q_sel, ks_ref, vs_ref, 0, n_pairs, sel_bias, s_scr, p_scr)
    win_lo = jnp.maximum(i // parts - WINDOW // TK_SEL, 0)
    ow_t = _flash_t(q_win, kw_ref, vw_ref, win_lo // 2, n_pairs, win_bias, s_scr, p_scr)

    oc_t = oct_ref[...]
    gt = gates_ref[...]
    for hg in range(HPG):
        c0 = N_BRANCH * hg
        lanes = slice(hg * TQ, (hg + 1) * TQ)
        out_t = (gt[c0:c0 + 1, :] * oc_t[:, lanes] + gt[c0 + 1:c0 + 2, :] * os_t[:, lanes]
                 + gt[c0 + 2:c0 + 3, :] * ow_t[:, lanes])
        y_ref[:, hg * HEAD_DIM:(hg + 1) * HEAD_DIM] = jnp.transpose(out_t).astype(y_ref.dtype)


def _sel_win_attn(q_t, ks, vs_t, kw, vw_t, unsel, tables, cvec, oc_t, gates_t):
    nb = q_t.shape[0]
    s = q_t.shape[3] // HPG
    ks_spec = pl.BlockSpec((None, None, s, KS_AUG), lambda b, g, i: (b, g, 0, 0))
    kw_spec = pl.BlockSpec((None, None, s, K_AUG), lambda b, g, i: (b, g, 0, 0))
    v_spec = pl.BlockSpec((None, None, V_AUG, s), lambda b, g, i: (b, g, 0, 0))
    qt_spec = pl.BlockSpec((None, None, HEAD_DIM, HPG * TQ), lambda b, g, i: (b, g, 0, i))
    gw = HPG * HEAD_DIM
    return pl.pallas_call(
        _sel_win_kernel,
        out_shape=jax.ShapeDtypeStruct((nb, s, NSA_HEADS * HEAD_DIM), BF16),
        grid=(nb, KV_GROUPS, s // TQ),
        in_specs=[
            qt_spec,
            ks_spec, v_spec, kw_spec, v_spec,
            pl.BlockSpec((None, None, SEL_LANES, TQ), lambda b, g, i: (b, g, 0, i)),
            pl.BlockSpec((CMP_BASE, HPG, LANES, TQ), lambda b, g, i: (0, g, 0, 0)),
            pl.BlockSpec((None, 1, HPG * TQ), lambda b, g, i: (g, 0, 0)),
            qt_spec,
            pl.BlockSpec((None, None, GATE_ROWS, TQ), lambda b, g, i: (b, g, 0, i)),
        ],
        out_specs=pl.BlockSpec((None, TQ, gw), lambda b, g, i: (b, i, g)),
        scratch_shapes=[pltpu.VMEM((2, TK_SEL, HPG * TQ), F32), pltpu.VMEM((2, TK_SEL, HPG * TQ), BF16)],
        compiler_params=_params(("arbitrary", "arbitrary", "arbitrary")),
        name="sel_win_attn",
    )(q_t, ks, vs_t, kw, vw_t, unsel, tables, cvec, oc_t, gates_t)


def kernel(x, mem, mix_norm, a_w_in, a_conv_w, b_w_in, b_q_gain, kv_norm, w_kv_shared, k_gain,
           cmp_pos, cmp_w1, cmp_w2, rel_bias, mem_norm, mem_w_kv, mem_q_gain, mem_k_gain, w_out,
           mlp_norm, w_up, w_down):
    nb, s, d = x.shape
    depth = mix_norm.shape[0]
    n_a = a_w_in.shape[0]
    assert s % (LANES * CMP_STRIDE) == 0 and s // SEL_BLOCK <= SEL_LANES
    assert WIN_BASE + N_WIN_NEAR <= CMP_BASE

    mk, mv = _mem_kv(mem, mem_norm, mem_w_kv, mem_k_gain)
    tables = _bias_tables(rel_bias)
    cvec = jnp.repeat(rel_bias[REL_BUCKETS - 1].astype(F32) * LOG2E, TQ).reshape(KV_GROUPS, 1, HPG * TQ)
    shared = None
    for layer in range(depth):
        if layer < n_a:
            y_tok, y_mem = _mixer_a(x, mix_norm[layer], a_w_in[layer], a_conv_w[layer],
                                    mk[layer], mv[layer], mem_q_gain[layer])
        else:
            j = layer - n_a
            ks, vs_t, kw, vw_t, kvc, kvc_t = shared
            q_t, gates_t, y_mem = _b_inproj(x, mix_norm[layer], b_w_in[j], b_q_gain[j],
                                            mk[layer], mv[layer], mem_q_gain[layer])
            oc_t, unsel = _cmp_attn(q_t, kvc, kvc_t, tables)
            y_tok = _sel_win_attn(q_t, ks, vs_t, kw, vw_t, unsel, tables, cvec, oc_t, gates_t)
        x = _post(x.reshape(nb * s, d), y_tok.reshape(nb * s, -1), y_mem.reshape(nb * s, -1),
                  w_out[layer], mlp_norm[layer], w_up[layer], w_down[layer]).reshape(nb, s, d)
        if layer == n_a - 1:
            ucmp, ks, vs_t, kw, vw_t = _build_kv(x, kv_norm, w_kv_shared, k_gain)
            kvc, kvc_t = _compress(ucmp, cmp_pos, cmp_w1, cmp_w2, k_gain[0])
            shared = (ks, vs_t, kw, vw_t, kvc, kvc_t)
    return x
```

```python
import functools
import math

import numpy as np
import jax
import jax.numpy as jnp
from jax import lax
from jax.experimental import pallas as pl
from jax.experimental.pallas import tpu as pltpu

F32 = jnp.float32
BF16 = jnp.bfloat16

HEAD_DIM = 64
MEM_HEADS = 4
MEM_WIDTH = MEM_HEADS * HEAD_DIM
CONV_CH = 768
NSA_HEADS = 12
KV_GROUPS = 2
HPG = NSA_HEADS // KV_GROUPS
N_BRANCH = 3
CMP_LEN = 32
CMP_STRIDE = 16
SEL_BLOCK = 64
SEL_TOPK = 16
N_LOCAL_SEL = 2
WINDOW = 512
REL_BUCKETS = 32
REL_MAX_DIST = 1024
EPS = 1e-6
NEG = -1e30
FORCE_SCORE = 1e4
LOG2E = math.log2(math.e)
MASK_BIG = 2.0 ** 100

LANES = 128
VMEM_LIMIT_BYTES = 56 * 1024 * 1024

TQ = 128
TK_SEL = 256
SEL_LANES = 128
TM = 512

N_SEL_NEAR = 8
SEL_CONST = N_SEL_NEAR
SEL_MASKED = N_SEL_NEAR + 1
WIN_BASE = N_SEL_NEAR + 2
N_WIN_NEAR = WINDOW // LANES + 1
CMP_BASE = 16
N_CMP_NEAR = 23
CMP_CONST = N_CMP_NEAR
CMP_MASKED = N_CMP_NEAR + 1
N_CMP_TILES = N_CMP_NEAR + 2
N_TILES = CMP_BASE + N_CMP_TILES


def _rms(xf, g):
    ms = jnp.mean(xf * xf, axis=-1, keepdims=True)
    return xf * lax.rsqrt(ms + EPS) * g


def _dot(a, b):
    return jnp.dot(a, b, preferred_element_type=F32)


def _dot_nt(a, b):
    return lax.dot_general(a, b, (((1,), (1,)), ((), ())), preferred_element_type=F32)


def _params(sem):
    return pltpu.CompilerParams(dimension_semantics=sem, vmem_limit_bytes=VMEM_LIMIT_BYTES)


def _mem_kv_kernel(mem_ref, mnorm_ref, w_ref, kg_ref, mk_ref, mvt_ref, *, nb, ml):
    mh = _rms(mem_ref[...], mnorm_ref[...]).astype(BF16)
    kv = _dot(mh, w_ref[...])
    for h in range(MEM_HEADS):
        kh = _rms(kv[:, h * HEAD_DIM:(h + 1) * HEAD_DIM], kg_ref[...]).astype(BF16)
        vh = kv[:, MEM_WIDTH + h * HEAD_DIM:MEM_WIDTH + (h + 1) * HEAD_DIM]
        for b in range(nb):
            mk_ref[b, h] = kh[b * ml:(b + 1) * ml]
            mvt_ref[b, h] = jnp.transpose(vh[b * ml:(b + 1) * ml]).astype(BF16)


def _mem_kv(mem, mem_norm, mem_w_kv, mem_k_gain):
    nb, ml, d = mem.shape
    depth = mem_w_kv.shape[0]
    out = jax.ShapeDtypeStruct((depth, nb, MEM_HEADS, ml, HEAD_DIM), BF16)
    out_t = jax.ShapeDtypeStruct((depth, nb, MEM_HEADS, HEAD_DIM, ml), BF16)
    return pl.pallas_call(
        functools.partial(_mem_kv_kernel, nb=nb, ml=ml),
        out_shape=(out, out_t),
        grid=(depth,),
        in_specs=[
            pl.BlockSpec((nb * ml, d), lambda l: (0, 0)),
            pl.BlockSpec((1, d), lambda l: (0, 0)),
            pl.BlockSpec((None, d, 2 * MEM_WIDTH), lambda l: (l, 0, 0)),
            pl.BlockSpec((None, 1, HEAD_DIM), lambda l: (l, 0, 0)),
        ],
        out_specs=(
            pl.BlockSpec((None, nb, MEM_HEADS, ml, HEAD_DIM), lambda l: (l, 0, 0, 0, 0)),
            pl.BlockSpec((None, nb, MEM_HEADS, HEAD_DIM, ml), lambda l: (l, 0, 0, 0, 0)),
        ),
        compiler_params=_params(("arbitrary",)),
        name="mem_kv",
    )(mem.reshape(nb * ml, d), mem_norm.reshape(1, d), mem_w_kv.astype(BF16),
      mem_k_gain.reshape(depth, 1, HEAD_DIM))


def _rms_t(xt, g_col):
    ms = jnp.mean(xt * xt, axis=0, keepdims=True)
    return xt * lax.rsqrt(ms + EPS) * g_col


def _mem_attn(qm, mk_ref, mvt_ref, qg_col, ymem_ref):
    qm_t = jnp.transpose(qm)
    outs = []
    for h in range(MEM_HEADS):
        qh = _rms_t(qm_t[h * HEAD_DIM:(h + 1) * HEAD_DIM, :], qg_col) * (HEAD_DIM ** -0.5)
        lg = _dot(mk_ref[h], qh.astype(BF16))
        m = jnp.max(lg, axis=0, keepdims=True)
        e = jnp.exp(lg - m)
        l = jnp.sum(e, axis=0, keepdims=True)
        outs.append(_dot(mvt_ref[h], e.astype(BF16)) / l)
    ymem_ref[...] = jnp.transpose(jnp.concatenate(outs, axis=0)).astype(ymem_ref.dtype)


def _bucket_np(d):
    n = np.maximum(d, 0)
    max_exact = REL_BUCKETS // 2
    nf = np.maximum(n, 1).astype(np.float64)
    large = max_exact + (np.log(nf / max_exact) / math.log(REL_MAX_DIST / max_exact)
                         * (REL_BUCKETS - max_exact)).astype(np.int32)
    large = np.minimum(large, REL_BUCKETS - 1)
    return np.where(n < max_exact, n, large).astype(np.int32)


def _bucket_tiles():
    r = np.arange(TQ)[:, None]
    k = np.arange(LANES)[None, :]
    tiles = []
    for delta in range(N_SEL_NEAR):
        d = delta * LANES + r - k
        tiles.append(np.where(d >= 0, _bucket_np(d), -1).T)
    tiles.append(np.full((TQ, LANES), REL_BUCKETS - 1))
    tiles.append(np.full((TQ, LANES), -1))
    for delta in range(N_WIN_NEAR):
        d = delta * LANES + r - k
        tiles.append(np.where((d >= 0) & (d < WINDOW), _bucket_np(d), -1).T)
    while len(tiles) < CMP_BASE:
        tiles.append(np.full((TQ, LANES), -1))
    for delta in range(N_CMP_NEAR):
        d = delta * LANES + r - CMP_STRIDE * k - (CMP_LEN - 1)
        tiles.append(np.where(d >= 0, _bucket_np(d), -1).T)
    tiles.append(np.full((TQ, LANES), REL_BUCKETS - 1))
    tiles.append(np.full((TQ, LANES), -1))
    return np.stack(tiles).astype(np.int32)


def _bias_tables_kernel(tab_ref, ids_ref, out_ref):
    ids = ids_ref[...]
    relative = pl.program_id(0) < CMP_BASE
    for h in range(NSA_HEADS):
        acc = jnp.zeros(ids.shape, F32)
        for b in range(REL_BUCKETS):
            acc = jnp.where(ids == b, tab_ref[b, h], acc)
        acc = acc - jnp.where(relative, tab_ref[REL_BUCKETS - 1, h], 0.0)
        out_ref[h] = jnp.where(ids < 0, NEG, acc * LOG2E)


def _bias_tables(rel_bias):
    ids = jnp.asarray(_bucket_tiles())
    return pl.pallas_call(
        _bias_tables_kernel,
        out_shape=jax.ShapeDtypeStruct((N_TILES, NSA_HEADS, TQ, LANES), F32),
        grid=(N_TILES,),
        in_specs=[
            pl.BlockSpec(memory_space=pltpu.SMEM),
            pl.BlockSpec((None, TQ, LANES), lambda t: (t, 0, 0)),
        ],
        out_specs=pl.BlockSpec((None, NSA_HEADS, TQ, LANES), lambda t: (t, 0, 0, 0)),
        compiler_params=_params(("arbitrary",)),
        name="bias_tables",
    )(rel_bias.astype(F32), ids)


def _mixer_a_kernel(x_ref, g_ref, w_ref, cw_ref, mk_ref, mvt_ref, qg_ref,
                    ytok_ref, ymem_ref, carry_ref, *, tm):
    @pl.when(pl.program_id(1) == 0)
    def _():
        carry_ref[...] = jnp.zeros_like(carry_ref)

    h = _rms(x_ref[...], g_ref[...]).astype(BF16)
    z = _dot(h, w_ref[...])
    gate_b = z[:, :CONV_CH]
    v = z[:, CONV_CH:2 * CONV_CH] * z[:, 2 * CONV_CH:3 * CONV_CH]
    prev = carry_ref[...]
    row = lax.broadcasted_iota(jnp.int32, (tm, CONV_CH), 0)
    v1 = jnp.where(row == 0, prev[7:8, :], pltpu.roll(v, 1, 0))
    v2 = jnp.where(row == 0, prev[6:7, :], jnp.where(row == 1, prev[7:8, :], pltpu.roll(v, 2, 0)))
    cw = cw_ref[...]
    y = gate_b * (cw[0:1, :] * v2 + cw[1:2, :] * v1 + cw[2:3, :] * v)
    carry_ref[...] = v[tm - 8:, :]
    ytok_ref[...] = y.astype(ytok_ref.dtype)
    _mem_attn(z[:, 3 * CONV_CH:], mk_ref, mvt_ref, qg_ref[...], ymem_ref)


def _mixer_a(x, norm_g, w_in, conv_w, mk, mv_t, mem_q_gain):
    nb, s, d = x.shape
    ml = mk.shape[2]
    n_in = w_in.shape[1]
    return pl.pallas_call(
        functools.partial(_mixer_a_kernel, tm=TM),
        out_shape=(jax.ShapeDtypeStruct((nb, s, CONV_CH), BF16),
                   jax.ShapeDtypeStruct((nb, s, MEM_WIDTH), BF16)),
        grid=(nb, s // TM),
        in_specs=[
            pl.BlockSpec((None, TM, d), lambda b, i: (b, i, 0)),
            pl.BlockSpec((1, d), lambda b, i: (0, 0)),
            pl.BlockSpec((d, n_in), lambda b, i: (0, 0)),
            pl.BlockSpec((3, CONV_CH), lambda b, i: (0, 0)),
            pl.BlockSpec((None, MEM_HEADS, ml, HEAD_DIM), lambda b, i: (b, 0, 0, 0)),
            pl.BlockSpec((None, MEM_HEADS, HEAD_DIM, ml), lambda b, i: (b, 0, 0, 0)),
            pl.BlockSpec((HEAD_DIM, 1), lambda b, i: (0, 0)),
        ],
        out_specs=(
            pl.BlockSpec((None, TM, CONV_CH), lambda b, i: (b, i, 0)),
            pl.BlockSpec((None, TM, MEM_WIDTH), lambda b, i: (b, i, 0)),
        ),
        scratch_shapes=[pltpu.VMEM((8, CONV_CH), F32)],
        compiler_params=_params(("arbitrary", "arbitrary")),
        name="mixer_a",
    )(x, norm_g.reshape(1, d), w_in.astype(BF16), conv_w, mk, mv_t, mem_q_gain.reshape(HEAD_DIM, 1))


B_GATE0 = NSA_HEADS * HEAD_DIM
B_QMEM0 = B_GATE0 + KV_GROUPS * LANES
B_WIDTH = B_QMEM0 + MEM_WIDTH


GATE_ROWS = 32


def _b_inproj_kernel(x_ref, g_ref, w_ref, qg_ref, mk_ref, mvt_ref, mqg_ref,
                     qt_ref, gates_ref, ymem_ref):
    h = _rms(x_ref[...], g_ref[...]).astype(BF16)
    z = _dot(h, w_ref[...])
    qg = qg_ref[...]
    tm = z.shape[0]
    for pair in range(NSA_HEADS // 2):
        z_t = jnp.transpose(z[:, pair * LANES:(pair + 1) * LANES])
        for half in range(2):
            qh_t = _rms_t(z_t[half * HEAD_DIM:(half + 1) * HEAD_DIM, :], qg) * (HEAD_DIM ** -0.5 * LOG2E)
            qh_t = qh_t.astype(qt_ref.dtype)
            g, hg = divmod(2 * pair + half, HPG)
            for qt in range(tm // TQ):
                lane0 = (qt * HPG + hg) * TQ
                qt_ref[g, :, lane0:lane0 + TQ] = qh_t[:, qt * TQ:(qt + 1) * TQ]
    for g in range(KV_GROUPS):
        gl = z[:, B_GATE0 + g * LANES:B_GATE0 + (g + 1) * LANES]
        gates_ref[g] = jnp.transpose(1.0 / (1.0 + jnp.exp(-gl)))[:GATE_ROWS, :]
    _mem_attn(z[:, B_QMEM0:], mk_ref, mvt_ref, mqg_ref[...], ymem_ref)


def _b_inproj(x, norm_g, w_in, q_gain, mk, mv_t, mem_q_gain):
    nb, s, d = x.shape
    ml = mk.shape[2]
    nq = NSA_HEADS * HEAD_DIM
    ng = HPG * N_BRANCH
    pad = jnp.zeros((d, LANES - ng), w_in.dtype)
    w = jnp.concatenate([w_in[:, :nq], w_in[:, nq:nq + ng], pad,
                         w_in[:, nq + ng:nq + 2 * ng], pad, w_in[:, nq + 2 * ng:]], axis=1)
    return pl.pallas_call(
        _b_inproj_kernel,
        out_shape=(jax.ShapeDtypeStruct((nb, KV_GROUPS, HEAD_DIM, HPG * s), BF16),
                   jax.ShapeDtypeStruct((nb, KV_GROUPS, GATE_ROWS, s), F32),
                   jax.ShapeDtypeStruct((nb, s, MEM_WIDTH), BF16)),
        grid=(nb, s // TM),
        in_specs=[
            pl.BlockSpec((None, TM, d), lambda b, i: (b, i, 0)),
            pl.BlockSpec((1, d), lambda b, i: (0, 0)),
            pl.BlockSpec((d, B_WIDTH), lambda b, i: (0, 0)),
            pl.BlockSpec((HEAD_DIM, 1), lambda b, i: (0, 0)),
            pl.BlockSpec((None, MEM_HEADS, ml, HEAD_DIM), lambda b, i: (b, 0, 0, 0)),
            pl.BlockSpec((None, MEM_HEADS, HEAD_DIM, ml), lambda b, i: (b, 0, 0, 0)),
            pl.BlockSpec((HEAD_DIM, 1), lambda b, i: (0, 0)),
        ],
        out_specs=(
            pl.BlockSpec((None, KV_GROUPS, HEAD_DIM, HPG * TM), lambda b, i: (b, 0, 0, i)),
            pl.BlockSpec((None, KV_GROUPS, GATE_ROWS, TM), lambda b, i: (b, 0, 0, i)),
            pl.BlockSpec((None, TM, MEM_WIDTH), lambda b, i: (b, i, 0)),
        ),
        compiler_params=_params(("arbitrary", "arbitrary")),
        name="b_inproj",
    )(x, norm_g.reshape(1, d), w.astype(BF16), q_gain.reshape(HEAD_DIM, 1), mk, mv_t,
      mem_q_gain.reshape(HEAD_DIM, 1))


FF_CHUNK = 1024
TM_POST = 1024


def _post_kernel(x_ref, ytok_ref, ymem_ref, woa_ref, wob_ref, g_ref, wup_ref, wdn_ref, o_ref, hm_ref):
    @pl.when(pl.program_id(1) == 0)
    def _():
        x1 = x_ref[...] + _dot(ytok_ref[...], woa_ref[...]) + _dot(ymem_ref[...], wob_ref[...])
        o_ref[...] = x1
        hm_ref[...] = _rms(x1, g_ref[...]).astype(hm_ref.dtype)

    a = _dot(hm_ref[...], wup_ref[...])
    a = jnp.square(jnp.maximum(a, 0.0)).astype(BF16)
    o_ref[...] += _dot(a, wdn_ref[...])


def _post(x, y_tok, y_mem, w_out, norm_g, w_up, w_down):
    n, d = x.shape
    ff = w_up.shape[1]
    nt = y_tok.shape[1]
    tm = min(TM_POST, n)
    const = lambda i, c: (0, 0)
    rows = lambda i, c: (i, 0)
    return pl.pallas_call(
        _post_kernel,
        out_shape=jax.ShapeDtypeStruct((n, d), F32),
        grid=(n // tm, ff // FF_CHUNK),
        in_specs=[
            pl.BlockSpec((tm, d), rows),
            pl.BlockSpec((tm, nt), rows),
            pl.BlockSpec((tm, MEM_WIDTH), rows),
            pl.BlockSpec((nt, d), const),
            pl.BlockSpec((MEM_WIDTH, d), const),
            pl.BlockSpec((1, d), const),
            pl.BlockSpec((d, FF_CHUNK), lambda i, c: (0, c)),
            pl.BlockSpec((FF_CHUNK, d), lambda i, c: (c, 0)),
        ],
        out_specs=pl.BlockSpec((tm, d), rows),
        scratch_shapes=[pltpu.VMEM((tm, d), BF16)],
        compiler_params=_params(("arbitrary", "arbitrary")),
        name="post",
    )(x, y_tok, y_mem, w_out[:nt].astype(BF16), w_out[nt:].astype(BF16), norm_g.reshape(1, d),
      w_up.astype(BF16), w_down.astype(BF16))


K_AUG = 2 * HEAD_DIM
KS_AUG = K_AUG + SEL_LANES
V_AUG = HEAD_DIM + 16


def _build_kv_kernel(x_ref, g_ref, w_ref, kg_ref, ucmp_ref, ks_ref, vs_ref, kw_ref, vw_ref):
    h = _rms(x_ref[...], g_ref[...]).astype(BF16)
    kv = _dot(h, w_ref[...])
    gw = KV_GROUPS * HEAD_DIM
    tm = kv.shape[0]
    ucmp_ref[...] = kv[:, :2 * gw].astype(ucmp_ref.dtype)
    kg = kg_ref[...]
    ones = (lax.broadcasted_iota(jnp.int32, (tm, K_AUG - HEAD_DIM), 1) < 2).astype(ks_ref.dtype)
    pos = pl.program_id(1) * tm + lax.broadcasted_iota(jnp.int32, (tm, SEL_LANES), 0)
    blk = lax.broadcasted_iota(jnp.int32, (tm, SEL_LANES), 1)
    onehot = jnp.where(blk == jnp.right_shift(pos, 6), MASK_BIG, 0.0).astype(ks_ref.dtype)
    ones_row = (lax.broadcasted_iota(jnp.int32, (V_AUG - HEAD_DIM, tm), 0) == 0).astype(vs_ref.dtype)
    for g in range(KV_GROUPS):
        sl = lambda t: kv[:, t * gw + g * HEAD_DIM:t * gw + (g + 1) * HEAD_DIM]
        ks_ref[g, :, :HEAD_DIM] = _rms(sl(2), kg[1:2, :]).astype(ks_ref.dtype)
        ks_ref[g, :, HEAD_DIM:K_AUG] = ones
        ks_ref[g, :, K_AUG:] = onehot
        vs_ref[g, :HEAD_DIM, :] = jnp.transpose(sl(3)).astype(vs_ref.dtype)
        vs_ref[g, HEAD_DIM:, :] = ones_row
        kw_ref[g, :, :HEAD_DIM] = _rms(sl(4), kg[2:3, :]).astype(kw_ref.dtype)
        kw_ref[g, :, HEAD_DIM:] = ones
        vw_ref[g, :HEAD_DIM, :] = jnp.transpose(sl(5)).astype(vw_ref.dtype)
        vw_ref[g, HEAD_DIM:, :] = ones_row


def _build_kv(x, kv_norm, w_kv, k_gain):
    nb, s, d = x.shape
    gw = KV_GROUPS * HEAD_DIM
    ks = jax.ShapeDtypeStruct((nb, KV_GROUPS, s, KS_AUG), BF16)
    kw = jax.ShapeDtypeStruct((nb, KV_GROUPS, s, K_AUG), BF16)
    vt = jax.ShapeDtypeStruct((nb, KV_GROUPS, V_AUG, s), BF16)
    ks_spec = pl.BlockSpec((None, KV_GROUPS, TM, KS_AUG), lambda b, i: (b, 0, i, 0))
    kw_spec = pl.BlockSpec((None, KV_GROUPS, TM, K_AUG), lambda b, i: (b, 0, i, 0))
    v_spec = pl.BlockSpec((None, KV_GROUPS, V_AUG, TM), lambda b, i: (b, 0, 0, i))
    return pl.pallas_call(
        _build_kv_kernel,
        out_shape=(jax.ShapeDtypeStruct((nb, s, 2 * gw), BF16), ks, vt, kw, vt),
        grid=(nb, s // TM),
        in_specs=[
            pl.BlockSpec((None, TM, d), lambda b, i: (b, i, 0)),
            pl.BlockSpec((1, d), lambda b, i: (0, 0)),
            pl.BlockSpec((d, 6 * gw), lambda b, i: (0, 0)),
            pl.BlockSpec((N_BRANCH, HEAD_DIM), lambda b, i: (0, 0)),
        ],
        out_specs=(pl.BlockSpec((None, TM, 2 * gw), lambda b, i: (b, i, 0)),
                   ks_spec, v_spec, kw_spec, v_spec),
        compiler_params=_params(("arbitrary", "arbitrary")),
        name="build_kv",
    )(x, kv_norm.reshape(1, d), w_kv.astype(BF16), k_gain)


def _compress_kernel(u_ref, w1_ref, w2_ref, pos_ref, kg_ref, out_ref, out_t_ref, *, nc):
    half = CMP_STRIDE * HEAD_DIM
    u = u_ref[...]
    lo = _dot(u, w1_ref[:half, :])
    hi = _dot(u, w1_ref[half:, :])
    posb = _dot(jnp.broadcast_to(pos_ref[...], (8, 2 * half)).astype(BF16), w1_ref[...])[0:1, :]
    h1 = lo + pltpu.roll(hi, nc - 1, 0) + posb
    hid = 0.5 * h1 * (1.0 + jnp.tanh(math.sqrt(2.0 / math.pi) * (h1 + 0.044715 * (h1 * h1 * h1))))
    c = _dot(hid.astype(BF16), w2_ref[...])
    is_key = pl.program_id(0) == 0
    out = jnp.where(is_key, _rms(c, kg_ref[...]), c)
    out_ref[...] = out.astype(out_ref.dtype)
    out_t_ref[...] = jnp.transpose(out).astype(out_t_ref.dtype)


def _compress(ucmp, cmp_pos, cmp_w1, cmp_w2, k_gain0):
    nb, s, _ = ucmp.shape
    nc = s // CMP_STRIDE
    width = CMP_STRIDE * HEAD_DIM
    u = ucmp.reshape(nb, nc, CMP_STRIDE, 2, KV_GROUPS, HEAD_DIM)
    u = u.transpose(3, 0, 4, 1, 2, 5).reshape(2, nb, KV_GROUPS, nc, width)
    hidden = cmp_w1.shape[2]
    return pl.pallas_call(
        functools.partial(_compress_kernel, nc=nc),
        out_shape=(jax.ShapeDtypeStruct((2, nb, KV_GROUPS, nc, HEAD_DIM), BF16),
                   jax.ShapeDtypeStruct((2, nb, KV_GROUPS, HEAD_DIM, nc), BF16)),
        grid=(2, nb, KV_GROUPS),
        in_specs=[
            pl.BlockSpec((None, None, None, nc, width), lambda t, b, g: (t, b, g, 0, 0)),
            pl.BlockSpec((None, 2 * width, hidden), lambda t, b, g: (t, 0, 0)),
            pl.BlockSpec((None, hidden, HEAD_DIM), lambda t, b, g: (t, 0, 0)),
            pl.BlockSpec((None, 1, 2 * width), lambda t, b, g: (t, 0, 0)),
            pl.BlockSpec((1, HEAD_DIM), lambda t, b, g: (0, 0)),
        ],
        out_specs=(pl.BlockSpec((None, None, None, nc, HEAD_DIM), lambda t, b, g: (t, b, g, 0, 0)),
                   pl.BlockSpec((None, None, None, HEAD_DIM, nc), lambda t, b, g: (t, b, g, 0, 0))),
        compiler_params=_params(("arbitrary", "arbitrary", "arbitrary")),
        name="compress",
    )(u, cmp_w1.astype(BF16), cmp_w2.astype(BF16), cmp_pos.reshape(2, 1, 2 * width),
      k_gain0.reshape(1, HEAD_DIM))


def _overlap_np(nc, n_sel):
    c0 = np.arange(nc)[:, None] * CMP_STRIDE
    s0 = np.arange(SEL_LANES)[None, :] * SEL_BLOCK
    ov = np.minimum(c0 + CMP_LEN, s0 + SEL_BLOCK) - np.maximum(c0, s0)
    ov = np.clip(ov, 0, None).astype(np.float32) / CMP_LEN
    ov[nc - 1:, :] = 0.0
    ov[:, n_sel:] = 0.0
    return ov


def _select_topk(score, ntop):
    blk = lax.broadcasted_iota(jnp.int32, score.shape, 0).astype(F32)
    for _ in range(ntop):
        m = jnp.max(score, axis=0, keepdims=True)
        first = jnp.min(jnp.where(score == m, blk, float(SEL_LANES)), axis=0, keepdims=True)
        score = jnp.where(blk == first, -jnp.inf, score)
    return score == -jnp.inf


CMP_QSUB = 2


def _cmp_attn_kernel(*refs, nj, n_sel):
    qt_ref, kc_ref, vct_ref, ovt_ref = refs[:4]
    tab_refs = refs[4:4 + nj * CMP_QSUB]
    oct_ref, unsel_ref = refs[4 + nj * CMP_QSUB:]
    i = pl.program_id(2)
    nq = CMP_QSUB * TQ
    s = _dot(kc_ref[...], qt_ref[...])
    s = jnp.concatenate(
        [jnp.concatenate([s[jt * LANES:(jt + 1) * LANES, (qs * HPG + h) * TQ:(qs * HPG + h + 1) * TQ]
                          + tab_refs[qs * nj + jt][h]
                          for qs in range(CMP_QSUB) for h in range(HPG)], axis=1) for jt in range(nj)], axis=0)
    m = jnp.maximum(jnp.max(s, axis=0, keepdims=True), 0.5 * NEG)
    e = jnp.exp2(s - m)
    l = jnp.sum(e, axis=0, keepdims=True)
    p = e * (1.0 / jnp.where(l == 0.0, 1.0, l))
    oct_ref[...] = _dot(vct_ref[...], p.astype(BF16))

    psums = []
    for qs in range(CMP_QSUB):
        acc = p[:, qs * HPG * TQ:(qs * HPG + 1) * TQ]
        for h in range(1, HPG):
            acc = acc + p[:, (qs * HPG + h) * TQ:(qs * HPG + h + 1) * TQ]
        psums.append(acc)
    psum = jnp.concatenate(psums, axis=1)
    p_hi = psum.astype(BF16)
    p_lo = (psum - p_hi.astype(F32)).astype(BF16)
    imp_t = _dot(ovt_ref[...], p_hi) + _dot(ovt_ref[...], p_lo)

    tq = i * nq + lax.broadcasted_iota(jnp.int32, (SEL_LANES, nq), 1)
    j = lax.broadcasted_iota(jnp.int32, (SEL_LANES, nq), 0)
    back = jnp.right_shift(tq, 6) - j
    forced = (j == 0) | ((back >= 0) & (back < N_LOCAL_SEL))
    causal = back >= 0
    n_free = min(SEL_TOPK, n_sel) - (N_LOCAL_SEL + 1)
    picked = _select_topk(jnp.where(causal & ~forced, imp_t, NEG), n_free)
    sel = causal & (forced | picked)
    unsel_ref[...] = jnp.where(sel, 0.0, -1.0).astype(unsel_ref.dtype)


def _cmp_attn(q_t, kvc, kvc_t, tables):
    nb = q_t.shape[0]
    s = q_t.shape[3] // HPG
    nc = s // CMP_STRIDE
    nj = nc // LANES
    n_sel = s // SEL_BLOCK
    ov_t = jnp.asarray(_overlap_np(nc, n_sel).T, dtype=BF16)

    def tab_map(qs, jt):
        def index_map(b, g, i):
            didx = CMP_QSUB * i + qs - (LANES * CMP_STRIDE // TQ) * jt
            row = jnp.where(didx < 0, CMP_MASKED, jnp.minimum(didx, CMP_CONST))
            return (CMP_BASE + row, g, 0, 0)
        return index_map

    nq = CMP_QSUB * TQ
    qt_spec = pl.BlockSpec((None, None, HEAD_DIM, HPG * nq), lambda b, g, i: (b, g, 0, i))
    n_tabs = CMP_QSUB * nj
    return pl.pallas_call(
        functools.partial(_cmp_attn_kernel, nj=nj, n_sel=n_sel),
        out_shape=(jax.ShapeDtypeStruct((nb, KV_GROUPS, HEAD_DIM, HPG * s), F32),
                   jax.ShapeDtypeStruct((nb, KV_GROUPS, SEL_LANES, s), BF16)),
        grid=(nb, KV_GROUPS, s // nq),
        in_specs=[
            qt_spec,
            pl.BlockSpec((None, None, None, nc, HEAD_DIM), lambda b, g, i: (0, b, g, 0, 0)),
            pl.BlockSpec((None, None, None, HEAD_DIM, nc), lambda b, g, i: (1, b, g, 0, 0)),
            pl.BlockSpec((SEL_LANES, nc), lambda b, g, i: (0, 0)),
        ] + [pl.BlockSpec((None, HPG, LANES, TQ), tab_map(qs, jt))
             for qs in range(CMP_QSUB) for jt in range(nj)],
        out_specs=(
            qt_spec,
            pl.BlockSpec((None, None, SEL_LANES, nq), lambda b, g, i: (b, g, 0, i)),
        ),
        compiler_params=_params(("arbitrary", "arbitrary", "arbitrary")),
        name="cmp_attn",
    )(q_t, kvc, kvc_t, ov_t, *([tables] * n_tabs))


def _flash_t(q_aug, k_ref, vt_ref, u_lo, u_hi, bias_fn, s_scr, p_scr):
    lanes = q_aug.shape[1]
    last_tile = 2 * u_hi - 1

    def scores(t):
        k0 = pl.multiple_of(t * TK_SEL, TK_SEL)
        return bias_fn(t, _dot(k_ref[pl.ds(k0, TK_SEL), :], q_aug))

    def pv(t, slot):
        k0 = pl.multiple_of(t * TK_SEL, TK_SEL)
        return _dot(vt_ref[:, pl.ds(k0, TK_SEL)], p_scr[slot])

    def softmax(slot, m):
        s = s_scr[slot]
        m_new = jnp.maximum(m, jnp.max(s, axis=0, keepdims=True))
        p_scr[slot] = jnp.exp2(s - m_new).astype(p_scr.dtype)
        return m_new, jnp.exp2(m - m_new)

    s_scr[0] = scores(2 * u_lo)
    p_scr[1] = jnp.zeros(p_scr.shape[1:], p_scr.dtype)

    def trip(u, carry, has_next):
        m, alpha, acc = carry
        a = 2 * u
        acc = alpha * acc + pv(jnp.maximum(a - 1, 0), 1)
        m, alpha = softmax(0, m)
        s_scr[1] = scores(a + 1)
        acc = alpha * acc + pv(a, 0)
        m, alpha = softmax(1, m)
        if has_next:
            s_scr[0] = scores(a + 2)
        return m, alpha, acc

    init = (jnp.full((1, lanes), NEG, F32), jnp.ones((1, lanes), F32), jnp.zeros((V_AUG, lanes), F32))
    carry = lax.fori_loop(u_lo, u_hi - 1, functools.partial(trip, has_next=True), init)
    _, alpha, acc = trip(u_hi - 1, carry, has_next=False)
    acc = alpha * acc + pv(last_tile, 1)
    return acc[:HEAD_DIM, :] / acc[HEAD_DIM:HEAD_DIM + 1, :]


def _add_tiles(s, tile_fn):
    rows = []
    for part in range(TK_SEL // LANES):
        cols = [s[part * LANES:(part + 1) * LANES, h * TQ:(h + 1) * TQ] + tile_fn(part, h)
                for h in range(HPG)]
        rows.append(jnp.concatenate(cols, axis=1))
    return jnp.concatenate(rows, axis=0)


def _sel_win_kernel(qt_ref, ks_ref, vs_ref, kw_ref, vw_ref, unsel_ref, tab_ref, cvec_ref,
                    oct_ref, gates_ref, y_ref, s_scr, p_scr):
    i = pl.program_id(2)
    parts = TK_SEL // LANES

    c = cvec_ref[...]
    c_hi = c.astype(BF16).astype(F32)
    c_lo = c - c_hi
    arow = lax.broadcasted_iota(jnp.int32, (K_AUG - HEAD_DIM, HPG * TQ), 0)
    aug = jnp.where(arow == 0, c_hi, jnp.where(arow == 1, c_lo, 0.0)).astype(BF16)
    q_win = jnp.concatenate([qt_ref[...], aug], axis=0)
    unsel_t = unsel_ref[...]
    q_sel = jnp.concatenate([q_win, jnp.concatenate([unsel_t] * HPG, axis=1)], axis=0)

    def sel_bias(t, s):
        def tile(part, h):
            delta = i - parts * t - part
            row = jnp.where(delta < 0, SEL_MASKED, jnp.minimum(delta, SEL_CONST))
            return tab_ref[row, h]

        return _add_tiles(s, tile)

    def win_bias(t, s):
        def tile(part, h):
            delta = i - parts * t - part
            row = jnp.where((delta < 0) | (delta >= N_WIN_NEAR), SEL_MASKED, WIN_BASE + delta)
            return tab_ref[row, h]

        return _add_tiles(s, tile)

    n_tiles = i // parts + 1
    n_pairs = (n_tiles + 1) // 2
    os_t = _flash_t(q_sel, ks_ref, vs_ref, 0, n_pairs, sel_bias, s_scr, p_scr)
    win_lo = jnp.maximum(i // parts - WINDOW // TK_SEL, 0)
    ow_t = _flash_t(q_win, kw_ref, vw_ref, win_lo // 2, n_pairs, win_bias, s_scr, p_scr)

    oc_t = oct_ref[...]
    gt = gates_ref[...]
    for hg in range(HPG):
        c0 = N_BRANCH * hg
        lanes = slice(hg * TQ, (hg + 1) * TQ)
        out_t = (gt[c0:c0 + 1, :] * oc_t[:, lanes] + gt[c0 + 1:c0 + 2, :] * os_t[:, lanes]
                 + gt[c0 + 2:c0 + 3, :] * ow_t[:, lanes])
        y_ref[:, hg * HEAD_DIM:(hg + 1) * HEAD_DIM] = jnp.transpose(out_t).astype(y_ref.dtype)


def _sel_win_attn(q_t, ks, vs_t, kw, vw_t, unsel, tables, cvec, oc_t, gates_t):
    nb = q_t.shape[0]
    s = q_t.shape[3] // HPG
    ks_spec = pl.BlockSpec((None, None, s, KS_AUG), lambda b, g, i: (b, g, 0, 0))
    kw_spec = pl.BlockSpec((None, None, s, K_AUG), lambda b, g, i: (b, g, 0, 0))
    v_spec = pl.BlockSpec((None, None, V_AUG, s), lambda b, g, i: (b, g, 0, 0))
    qt_spec = pl.BlockSpec((None, None, HEAD_DIM, HPG * TQ), lambda b, g, i: (b, g, 0, i))
    gw = HPG * HEAD_DIM
    return pl.pallas_call(
        _sel_win_kernel,
        out_shape=jax.ShapeDtypeStruct((nb, s, NSA_HEADS * HEAD_DIM), BF16),
        grid=(nb, KV_GROUPS, s // TQ),
        in_specs=[
            qt_spec,
            ks_spec, v_spec, kw_spec, v_spec,
            pl.BlockSpec((None, None, SEL_LANES, TQ), lambda b, g, i: (b, g, 0, i)),
            pl.BlockSpec((CMP_BASE, HPG, LANES, TQ), lambda b, g, i: (0, g, 0, 0)),
            pl.BlockSpec((None, 1, HPG * TQ), lambda b, g, i: (g, 0, 0)),
            qt_spec,
            pl.BlockSpec((None, None, GATE_ROWS, TQ), lambda b, g, i: (b, g, 0, i)),
        ],
        out_specs=pl.BlockSpec((None, TQ, gw), lambda b, g, i: (b, i, g)),
        scratch_shapes=[pltpu.VMEM((2, TK_SEL, HPG * TQ), F32), pltpu.VMEM((2, TK_SEL, HPG * TQ), BF16)],
        compiler_params=_params(("arbitrary", "arbitrary", "arbitrary")),
        name="sel_win_attn",
    )(q_t, ks, vs_t, kw, vw_t, unsel, tables, cvec, oc_t, gates_t)


def kernel(x, mem, mix_norm, a_w_in, a_conv_w, b_w_in, b_q_gain, kv_norm, w_kv_shared, k_gain,
           cmp_pos, cmp_w1, cmp_w2, rel_bias, mem_norm, mem_w_kv, mem_q_gain, mem_k_gain, w_out,
           mlp_norm, w_up, w_down):
    nb, s, d = x.shape
    depth = mix_norm.shape[0]
    n_a = a_w_in.shape[0]
    assert s % (LANES * CMP_STRIDE) == 0 and s // SEL_BLOCK <= SEL_LANES
    assert WIN_BASE + N_WIN_NEAR <= CMP_BASE

    mk, mv_t = _mem_kv(mem, mem_norm, mem_w_kv, mem_k_gain)
    tables = _bias_tables(rel_bias)
    cvec = jnp.repeat(rel_bias[REL_BUCKETS - 1].astype(F32) * LOG2E, TQ).reshape(KV_GROUPS, 1, HPG * TQ)
    shared = None
    for layer in range(depth):
        if layer < n_a:
            y_tok, y_mem = _mixer_a(x, mix_norm[layer], a_w_in[layer], a_conv_w[layer],
                                    mk[layer], mv_t[layer], mem_q_gain[layer])
        else:
            j = layer - n_a
            ks, vs_t, kw, vw_t, kvc, kvc_t = shared
            q_t, gates_t, y_mem = _b_inproj(x, mix_norm[layer], b_w_in[j], b_q_gain[j],
                                            mk[layer], mv_t[layer], mem_q_gain[layer])
            oc_t, unsel = _cmp_attn(q_t, kvc, kvc_t, tables)
            y_tok = _sel_win_attn(q_t, ks, vs_t, kw, vw_t, unsel, tables, cvec, oc_t, gates_t)
        x = _post(x.reshape(nb * s, d), y_tok.reshape(nb * s, -1), y_mem.reshape(nb * s, -1),
                  w_out[layer], mlp_norm[layer], w_up[layer], w_down[layer]).reshape(nb, s, d)
        if layer == n_a - 1:
            ucmp, ks, vs_t, kw, vw_t = _build_kv(x, kv_norm, w_kv_shared, k_gain)
            kvc, kvc_t = _compress(ucmp, cmp_pos, cmp_w1, cmp_w2, k_gain[0])
            shared = (ks, vs_t, kw, vw_t, kvc, kvc_t)
    return x
```

```python
import functools
import math

import numpy as np
import jax
import jax.numpy as jnp
from jax import lax
from jax.experimental import pallas as pl
from jax.experimental.pallas import tpu as pltpu

F32 = jnp.float32
BF16 = jnp.bfloat16

HEAD_DIM = 64
MEM_HEADS = 4
MEM_WIDTH = MEM_HEADS * HEAD_DIM
CONV_CH = 768
NSA_HEADS = 12
KV_GROUPS = 2
HPG = NSA_HEADS // KV_GROUPS
N_BRANCH = 3
CMP_LEN = 32
CMP_STRIDE = 16
SEL_BLOCK = 64
SEL_TOPK = 16
N_LOCAL_SEL = 2
WINDOW = 512
REL_BUCKETS = 32
REL_MAX_DIST = 1024
EPS = 1e-6
NEG = -1e30
FORCE_SCORE = 1e4
LOG2E = math.log2(math.e)
MASK_BIG = 2.0 ** 100

LANES = 128
VMEM_LIMIT_BYTES = 56 * 1024 * 1024

TQ = 128
TK_SEL = 256
SEL_LANES = 128
TM = 512

N_SEL_NEAR = 8
SEL_CONST = N_SEL_NEAR
SEL_MASKED = N_SEL_NEAR + 1
WIN_BASE = N_SEL_NEAR + 2
N_WIN_NEAR = WINDOW // LANES + 1
CMP_BASE = 16
N_CMP_NEAR = 23
CMP_CONST = N_CMP_NEAR
CMP_MASKED = N_CMP_NEAR + 1
N_CMP_TILES = N_CMP_NEAR + 2
N_TILES = CMP_BASE + N_CMP_TILES


def _rms(xf, g):
    ms = jnp.mean(xf * xf, axis=-1, keepdims=True)
    return xf * lax.rsqrt(ms + EPS) * g


def _dot(a, b):
    return jnp.dot(a, b, preferred_element_type=F32)


def _dot_nt(a, b):
    return lax.dot_general(a, b, (((1,), (1,)), ((), ())), preferred_element_type=F32)


def _params(sem):
    return pltpu.CompilerParams(dimension_semantics=sem, vmem_limit_bytes=VMEM_LIMIT_BYTES)


def _mem_kv_kernel(mem_ref, mnorm_ref, w_ref, kg_ref, mk_ref, mvt_ref, *, nb, ml):
    mh = _rms(mem_ref[...], mnorm_ref[...]).astype(BF16)
    kv = _dot(mh, w_ref[...])
    for h in range(MEM_HEADS):
        kh = _rms(kv[:, h * HEAD_DIM:(h + 1) * HEAD_DIM], kg_ref[...]).astype(BF16)
        vh = kv[:, MEM_WIDTH + h * HEAD_DIM:MEM_WIDTH + (h + 1) * HEAD_DIM]
        for b in range(nb):
            mk_ref[b, h] = kh[b * ml:(b + 1) * ml]
            mvt_ref[b, h] = jnp.transpose(vh[b * ml:(b + 1) * ml]).astype(BF16)


def _mem_kv(mem, mem_norm, mem_w_kv, mem_k_gain):
    nb, ml, d = mem.shape
    depth = mem_w_kv.shape[0]
    out = jax.ShapeDtypeStruct((depth, nb, MEM_HEADS, ml, HEAD_DIM), BF16)
    out_t = jax.ShapeDtypeStruct((depth, nb, MEM_HEADS, HEAD_DIM, ml), BF16)
    return pl.pallas_call(
        functools.partial(_mem_kv_kernel, nb=nb, ml=ml),
        out_shape=(out, out_t),
        grid=(depth,),
        in_specs=[
            pl.BlockSpec((nb * ml, d), lambda l: (0, 0)),
            pl.BlockSpec((1, d), lambda l: (0, 0)),
            pl.BlockSpec((None, d, 2 * MEM_WIDTH), lambda l: (l, 0, 0)),
            pl.BlockSpec((None, 1, HEAD_DIM), lambda l: (l, 0, 0)),
        ],
        out_specs=(
            pl.BlockSpec((None, nb, MEM_HEADS, ml, HEAD_DIM), lambda l: (l, 0, 0, 0, 0)),
            pl.BlockSpec((None, nb, MEM_HEADS, HEAD_DIM, ml), lambda l: (l, 0, 0, 0, 0)),
        ),
        compiler_params=_params(("arbitrary",)),
        name="mem_kv",
    )(mem.reshape(nb * ml, d), mem_norm.reshape(1, d), mem_w_kv.astype(BF16),
      mem_k_gain.reshape(depth, 1, HEAD_DIM))


def _rms_t(xt, g_col):
    ms = jnp.mean(xt * xt, axis=0, keepdims=True)
    return xt * lax.rsqrt(ms + EPS) * g_col


def _mem_attn(qm, mk_ref, mvt_ref, qg_col, ymem_ref):
    qm_t = jnp.transpose(qm)
    outs = []
    for h in range(MEM_HEADS):
        qh = _rms_t(qm_t[h * HEAD_DIM:(h + 1) * HEAD_DIM, :], qg_col) * (HEAD_DIM ** -0.5)
        lg = _dot(mk_ref[h], qh.astype(BF16))
        m = jnp.max(lg, axis=0, keepdims=True)
        e = jnp.exp(lg - m)
        l = jnp.sum(e, axis=0, keepdims=True)
        outs.append(_dot(mvt_ref[h], e.astype(BF16)) / l)
    ymem_ref[...] = jnp.transpose(jnp.concatenate(outs, axis=0)).astype(ymem_ref.dtype)


def _bucket_np(d):
    n = np.maximum(d, 0)
    max_exact = REL_BUCKETS // 2
    nf = np.maximum(n, 1).astype(np.float64)
    large = max_exact + (np.log(nf / max_exact) / math.log(REL_MAX_DIST / max_exact)
                         * (REL_BUCKETS - max_exact)).astype(np.int32)
    large = np.minimum(large, REL_BUCKETS - 1)
    return np.where(n < max_exact, n, large).astype(np.int32)


def _bucket_tiles():
    r = np.arange(TQ)[:, None]
    k = np.arange(LANES)[None, :]
    tiles = []
    for delta in range(N_SEL_NEAR):
        d = delta * LANES + r - k
        tiles.append(np.where(d >= 0, _bucket_np(d), -1).T)
    tiles.append(np.full((TQ, LANES), REL_BUCKETS - 1))
    tiles.append(np.full((TQ, LANES), -1))
    for delta in range(N_WIN_NEAR):
        d = delta * LANES + r - k
        tiles.append(np.where((d >= 0) & (d < WINDOW), _bucket_np(d), -1).T)
    while len(tiles) < CMP_BASE:
        tiles.append(np.full((TQ, LANES), -1))
    for delta in range(N_CMP_NEAR):
        d = delta * LANES + r - CMP_STRIDE * k - (CMP_LEN - 1)
        tiles.append(np.where(d >= 0, _bucket_np(d), -1).T)
    tiles.append(np.full((TQ, LANES), REL_BUCKETS - 1))
    tiles.append(np.full((TQ, LANES), -1))
    return np.stack(tiles).astype(np.int32)


def _bias_tables_kernel(tab_ref, ids_ref, out_ref):
    ids = ids_ref[...]
    relative = pl.program_id(0) < CMP_BASE
    for h in range(NSA_HEADS):
        acc = jnp.zeros(ids.shape, F32)
        for b in range(REL_BUCKETS):
            acc = jnp.where(ids == b, tab_ref[b, h], acc)
        acc = acc - jnp.where(relative, tab_ref[REL_BUCKETS - 1, h], 0.0)
        out_ref[h] = jnp.where(ids < 0, NEG, acc * LOG2E)


def _bias_tables(rel_bias):
    ids = jnp.asarray(_bucket_tiles())
    return pl.pallas_call(
        _bias_tables_kernel,
        out_shape=jax.ShapeDtypeStruct((N_TILES, NSA_HEADS, TQ, LANES), F32),
        grid=(N_TILES,),
        in_specs=[
            pl.BlockSpec(memory_space=pltpu.SMEM),
            pl.BlockSpec((None, TQ, LANES), lambda t: (t, 0, 0)),
        ],
        out_specs=pl.BlockSpec((None, NSA_HEADS, TQ, LANES), lambda t: (t, 0, 0, 0)),
        compiler_params=_params(("arbitrary",)),
        name="bias_tables",
    )(rel_bias.astype(F32), ids)


def _mixer_a_kernel(x_ref, g_ref, w_ref, cw_ref, mk_ref, mvt_ref, qg_ref,
                    ytok_ref, ymem_ref, carry_ref, *, tm):
    @pl.when(pl.program_id(1) == 0)
    def _():
        carry_ref[...] = jnp.zeros_like(carry_ref)

    h = _rms(x_ref[...], g_ref[...]).astype(BF16)
    z = _dot(h, w_ref[...])
    gate_b = z[:, :CONV_CH]
    v = z[:, CONV_CH:2 * CONV_CH] * z[:, 2 * CONV_CH:3 * CONV_CH]
    prev = carry_ref[...]
    row = lax.broadcasted_iota(jnp.int32, (tm, CONV_CH), 0)
    v1 = jnp.where(row == 0, prev[7:8, :], pltpu.roll(v, 1, 0))
    v2 = jnp.where(row == 0, prev[6:7, :], jnp.where(row == 1, prev[7:8, :], pltpu.roll(v, 2, 0)))
    cw = cw_ref[...]
    y = gate_b * (cw[0:1, :] * v2 + cw[1:2, :] * v1 + cw[2:3, :] * v)
    carry_ref[...] = v[tm - 8:, :]
    ytok_ref[...] = y.astype(ytok_ref.dtype)
    _mem_attn(z[:, 3 * CONV_CH:], mk_ref, mvt_ref, qg_ref[...], ymem_ref)


def _mixer_a(x, norm_g, w_in, conv_w, mk, mv_t, mem_q_gain):
    nb, s, d = x.shape
    ml = mk.shape[2]
    n_in = w_in.shape[1]
    return pl.pallas_call(
        functools.partial(_mixer_a_kernel, tm=TM),
        out_shape=(jax.ShapeDtypeStruct((nb, s, CONV_CH), BF16),
                   jax.ShapeDtypeStruct((nb, s, MEM_WIDTH), BF16)),
        grid=(nb, s // TM),
        in_specs=[
            pl.BlockSpec((None, TM, d), lambda b, i: (b, i, 0)),
            pl.BlockSpec((1, d), lambda b, i: (0, 0)),
            pl.BlockSpec((d, n_in), lambda b, i: (0, 0)),
            pl.BlockSpec((3, CONV_CH), lambda b, i: (0, 0)),
            pl.BlockSpec((None, MEM_HEADS, ml, HEAD_DIM), lambda b, i: (b, 0, 0, 0)),
            pl.BlockSpec((None, MEM_HEADS, HEAD_DIM, ml), lambda b, i: (b, 0, 0, 0)),
            pl.BlockSpec((HEAD_DIM, 1), lambda b, i: (0, 0)),
        ],
        out_specs=(
            pl.BlockSpec((None, TM, CONV_CH), lambda b, i: (b, i, 0)),
            pl.BlockSpec((None, TM, MEM_WIDTH), lambda b, i: (b, i, 0)),
        ),
        scratch_shapes=[pltpu.VMEM((8, CONV_CH), F32)],
        compiler_params=_params(("arbitrary", "arbitrary")),
        name="mixer_a",
    )(x, norm_g.reshape(1, d), w_in.astype(BF16), conv_w, mk, mv_t, mem_q_gain.reshape(HEAD_DIM, 1))


B_GATE0 = NSA_HEADS * HEAD_DIM
B_QMEM0 = B_GATE0 + KV_GROUPS * LANES
B_WIDTH = B_QMEM0 + MEM_WIDTH


GATE_ROWS = 32


def _b_inproj_kernel(x_ref, g_ref, w_ref, qg_ref, mk_ref, mvt_ref, mqg_ref,
                     qt_ref, gates_ref, ymem_ref):
    h = _rms(x_ref[...], g_ref[...]).astype(BF16)
    z = _dot(h, w_ref[...])
    qg = qg_ref[...]
    tm = z.shape[0]
    for pair in range(NSA_HEADS // 2):
        z_t = jnp.transpose(z[:, pair * LANES:(pair + 1) * LANES])
        for half in range(2):
            qh_t = _rms_t(z_t[half * HEAD_DIM:(half + 1) * HEAD_DIM, :], qg) * (HEAD_DIM ** -0.5 * LOG2E)
            qh_t = qh_t.astype(qt_ref.dtype)
            g, hg = divmod(2 * pair + half, HPG)
            for qt in range(tm // TQ):
                lane0 = (qt * HPG + hg) * TQ
                qt_ref[g, :, lane0:lane0 + TQ] = qh_t[:, qt * TQ:(qt + 1) * TQ]
    for g in range(KV_GROUPS):
        gl = z[:, B_GATE0 + g * LANES:B_GATE0 + (g + 1) * LANES]
        gates_ref[g] = jnp.transpose(1.0 / (1.0 + jnp.exp(-gl)))[:GATE_ROWS, :]
    _mem_attn(z[:, B_QMEM0:], mk_ref, mvt_ref, mqg_ref[...], ymem_ref)


def _b_inproj(x, norm_g, w_in, q_gain, mk, mv_t, mem_q_gain):
    nb, s, d = x.shape
    ml = mk.shape[2]
    nq = NSA_HEADS * HEAD_DIM
    ng = HPG * N_BRANCH
    pad = jnp.zeros((d, LANES - ng), w_in.dtype)
    w = jnp.concatenate([w_in[:, :nq], w_in[:, nq:nq + ng], pad,
                         w_in[:, nq + ng:nq + 2 * ng], pad, w_in[:, nq + 2 * ng:]], axis=1)
    return pl.pallas_call(
        _b_inproj_kernel,
        out_shape=(jax.ShapeDtypeStruct((nb, KV_GROUPS, HEAD_DIM, HPG * s), BF16),
                   jax.ShapeDtypeStruct((nb, KV_GROUPS, GATE_ROWS, s), F32),
                   jax.ShapeDtypeStruct((nb, s, MEM_WIDTH), BF16)),
        grid=(nb, s // TM),
        in_specs=[
            pl.BlockSpec((None, TM, d), lambda b, i: (b, i, 0)),
            pl.BlockSpec((1, d), lambda b, i: (0, 0)),
            pl.BlockSpec((d, B_WIDTH), lambda b, i: (0, 0)),
            pl.BlockSpec((HEAD_DIM, 1), lambda b, i: (0, 0)),
            pl.BlockSpec((None, MEM_HEADS, ml, HEAD_DIM), lambda b, i: (b, 0, 0, 0)),
            pl.BlockSpec((None, MEM_HEADS, HEAD_DIM, ml), lambda b, i: (b, 0, 0, 0)),
            pl.BlockSpec((HEAD_DIM, 1), lambda b, i: (0, 0)),
        ],
        out_specs=(
            pl.BlockSpec((None, KV_GROUPS, HEAD_DIM, HPG * TM), lambda b, i: (b, 0, 0, i)),
            pl.BlockSpec((None, KV_GROUPS, GATE_ROWS, TM), lambda b, i: (b, 0, 0, i)),
            pl.BlockSpec((None, TM, MEM_WIDTH), lambda b, i: (b, i, 0)),
        ),
        compiler_params=_params(("arbitrary", "arbitrary")),
        name="b_inproj",
    )(x, norm_g.reshape(1, d), w.astype(BF16), q_gain.reshape(HEAD_DIM, 1), mk, mv_t,
      mem_q_gain.reshape(HEAD_DIM, 1))


FF_CHUNK = 1024
TM_POST = 1024


def _post_kernel(x_ref, ytok_ref, ymem_ref, woa_ref, wob_ref, g_ref, wup_ref, wdn_ref, o_ref, hm_ref):
    @pl.when(pl.program_id(1) == 0)
    def _():
        x1 = x_ref[...] + _dot(ytok_ref[...], woa_ref[...]) + _dot(ymem_ref[...], wob_ref[...])
        o_ref[...] = x1
        hm_ref[...] = _rms(x1, g_ref[...]).astype(hm_ref.dtype)

    a = _dot(hm_ref[...], wup_ref[...])
    a = jnp.square(jnp.maximum(a, 0.0)).astype(BF16)
    o_ref[...] += _dot(a, wdn_ref[...])


def _post(x, y_tok, y_mem, w_out, norm_g, w_up, w_down):
    n, d = x.shape
    ff = w_up.shape[1]
    nt = y_tok.shape[1]
    tm = min(TM_POST, n)
    const = lambda i, c: (0, 0)
    rows = lambda i, c: (i, 0)
    return pl.pallas_call(
        _post_kernel,
        out_shape=jax.ShapeDtypeStruct((n, d), F32),
        grid=(n // tm, ff // FF_CHUNK),
        in_specs=[
            pl.BlockSpec((tm, d), rows),
            pl.BlockSpec((tm, nt), rows),
            pl.BlockSpec((tm, MEM_WIDTH), rows),
            pl.BlockSpec((nt, d), const),
            pl.BlockSpec((MEM_WIDTH, d), const),
            pl.BlockSpec((1, d), const),
            pl.BlockSpec((d, FF_CHUNK), lambda i, c: (0, c)),
            pl.BlockSpec((FF_CHUNK, d), lambda i, c: (c, 0)),
        ],
        out_specs=pl.BlockSpec((tm, d), rows),
        scratch_shapes=[pltpu.VMEM((tm, d), BF16)],
        compiler_params=_params(("arbitrary", "arbitrary")),
        name="post",
    )(x, y_tok, y_mem, w_out[:nt].astype(BF16), w_out[nt:].astype(BF16), norm_g.reshape(1, d),
      w_up.astype(BF16), w_down.astype(BF16))


K_AUG = 2 * HEAD_DIM
KS_AUG = K_AUG + SEL_LANES
V_AUG = HEAD_DIM + 16


def _build_kv_kernel(x_ref, g_ref, w_ref, kg_ref, ucmp_ref, ks_ref, vs_ref, kw_ref, vw_ref, raw_scr):
    h = _rms(x_ref[...], g_ref[...]).astype(BF16)
    kv = _dot(h, w_ref[...])
    gw = KV_GROUPS * HEAD_DIM
    tm = kv.shape[0]
    for ty in range(2):
        raw_scr[ty] = kv[:, ty * gw:(ty + 1) * gw]
        for t in range(CMP_STRIDE):
            rows = raw_scr[ty, pl.ds(t, tm // CMP_STRIDE, stride=CMP_STRIDE), :]
            for g in range(KV_GROUPS):
                ucmp_ref[ty, g, :, t * HEAD_DIM:(t + 1) * HEAD_DIM] = (
                    rows[:, g * HEAD_DIM:(g + 1) * HEAD_DIM].astype(ucmp_ref.dtype))
    kg = kg_ref[...]
    ones = (lax.broadcasted_iota(jnp.int32, (tm, K_AUG - HEAD_DIM), 1) < 2).astype(ks_ref.dtype)
    pos = pl.program_id(1) * tm + lax.broadcasted_iota(jnp.int32, (tm, SEL_LANES), 0)
    blk = lax.broadcasted_iota(jnp.int32, (tm, SEL_LANES), 1)
    onehot = jnp.where(blk == jnp.right_shift(pos, 6), MASK_BIG, 0.0).astype(ks_ref.dtype)
    ones_row = (lax.broadcasted_iota(jnp.int32, (V_AUG - HEAD_DIM, tm), 0) == 0).astype(vs_ref.dtype)
    for g in range(KV_GROUPS):
        sl = lambda t: kv[:, t * gw + g * HEAD_DIM:t * gw + (g + 1) * HEAD_DIM]
        ks_ref[g, :, :HEAD_DIM] = _rms(sl(2), kg[1:2, :]).astype(ks_ref.dtype)
        ks_ref[g, :, HEAD_DIM:K_AUG] = ones
        ks_ref[g, :, K_AUG:] = onehot
        vs_ref[g, :HEAD_DIM, :] = jnp.transpose(sl(3)).astype(vs_ref.dtype)
        vs_ref[g, HEAD_DIM:, :] = ones_row
        kw_ref[g, :, :HEAD_DIM] = _rms(sl(4), kg[2:3, :]).astype(kw_ref.dtype)
        kw_ref[g, :, HEAD_DIM:] = ones
        vw_ref[g, :HEAD_DIM, :] = jnp.transpose(sl(5)).astype(vw_ref.dtype)
        vw_ref[g, HEAD_DIM:, :] = ones_row


def _build_kv(x, kv_norm, w_kv, k_gain):
    nb, s, d = x.shape
    gw = KV_GROUPS * HEAD_DIM
    ks = jax.ShapeDtypeStruct((nb, KV_GROUPS, s, KS_AUG), BF16)
    kw = jax.ShapeDtypeStruct((nb, KV_GROUPS, s, K_AUG), BF16)
    vt = jax.ShapeDtypeStruct((nb, KV_GROUPS, V_AUG, s), BF16)
    ks_spec = pl.BlockSpec((None, KV_GROUPS, TM, KS_AUG), lambda b, i: (b, 0, i, 0))
    kw_spec = pl.BlockSpec((None, KV_GROUPS, TM, K_AUG), lambda b, i: (b, 0, i, 0))
    v_spec = pl.BlockSpec((None, KV_GROUPS, V_AUG, TM), lambda b, i: (b, 0, 0, i))
    return pl.pallas_call(
        _build_kv_kernel,
        out_shape=(jax.ShapeDtypeStruct((2, nb, KV_GROUPS, s // CMP_STRIDE, CMP_STRIDE * HEAD_DIM), BF16),
                   ks, vt, kw, vt),
        grid=(nb, s // TM),
        in_specs=[
            pl.BlockSpec((None, TM, d), lambda b, i: (b, i, 0)),
            pl.BlockSpec((1, d), lambda b, i: (0, 0)),
            pl.BlockSpec((d, 6 * gw), lambda b, i: (0, 0)),
            pl.BlockSpec((N_BRANCH, HEAD_DIM), lambda b, i: (0, 0)),
        ],
        out_specs=(pl.BlockSpec((2, None, KV_GROUPS, TM // CMP_STRIDE, CMP_STRIDE * HEAD_DIM),
                                lambda b, i: (0, b, 0, i, 0)),
                   ks_spec, v_spec, kw_spec, v_spec),
        scratch_shapes=[pltpu.VMEM((2, TM, gw), F32)],
        compiler_params=_params(("arbitrary", "arbitrary")),
        name="build_kv",
    )(x, kv_norm.reshape(1, d), w_kv.astype(BF16), k_gain)


def _compress_kernel(u_ref, w1_ref, w2_ref, pos_ref, kg_ref, out_ref, out_t_ref, *, nc):
    half = CMP_STRIDE * HEAD_DIM
    u = u_ref[...]
    lo = _dot(u, w1_ref[:half, :])
    hi = _dot(u, w1_ref[half:, :])
    posb = _dot(jnp.broadcast_to(pos_ref[...], (8, 2 * half)).astype(BF16), w1_ref[...])[0:1, :]
    h1 = lo + pltpu.roll(hi, nc - 1, 0) + posb
    hid = 0.5 * h1 * (1.0 + jnp.tanh(math.sqrt(2.0 / math.pi) * (h1 + 0.044715 * (h1 * h1 * h1))))
    c = _dot(hid.astype(BF16), w2_ref[...])
    is_key = pl.program_id(0) == 0
    out = jnp.where(is_key, _rms(c, kg_ref[...]), c)
    out_ref[...] = out.astype(out_ref.dtype)
    out_t_ref[...] = jnp.transpose(out).astype(out_t_ref.dtype)


def _compress(u, cmp_pos, cmp_w1, cmp_w2, k_gain0):
    _, nb, _, nc, width = u.shape
    hidden = cmp_w1.shape[2]
    return pl.pallas_call(
        functools.partial(_compress_kernel, nc=nc),
        out_shape=(jax.ShapeDtypeStruct((2, nb, KV_GROUPS, nc, HEAD_DIM), BF16),
                   jax.ShapeDtypeStruct((2, nb, KV_GROUPS, HEAD_DIM, nc), BF16)),
        grid=(2, nb, KV_GROUPS),
        in_specs=[
            pl.BlockSpec((None, None, None, nc, width), lambda t, b, g: (t, b, g, 0, 0)),
            pl.BlockSpec((None, 2 * width, hidden), lambda t, b, g: (t, 0, 0)),
            pl.BlockSpec((None, hidden, HEAD_DIM), lambda t, b, g: (t, 0, 0)),
            pl.BlockSpec((None, 1, 2 * width), lambda t, b, g: (t, 0, 0)),
            pl.BlockSpec((1, HEAD_DIM), lambda t, b, g: (0, 0)),
        ],
        out_specs=(pl.BlockSpec((None, None, None, nc, HEAD_DIM), lambda t, b, g: (t, b, g, 0, 0)),
                   pl.BlockSpec((None, None, None, HEAD_DIM, nc), lambda t, b, g: (t, b, g, 0, 0))),
        compiler_params=_params(("arbitrary", "arbitrary", "arbitrary")),
        name="compress",
    )(u, cmp_w1.astype(BF16), cmp_w2.astype(BF16), cmp_pos.reshape(2, 1, 2 * width),
      k_gain0.reshape(1, HEAD_DIM))


def _overlap_np(nc, n_sel):
    c0 = np.arange(nc)[:, None] * CMP_STRIDE
    s0 = np.arange(SEL_LANES)[None, :] * SEL_BLOCK
    ov = np.minimum(c0 + CMP_LEN, s0 + SEL_BLOCK) - np.maximum(c0, s0)
    ov = np.clip(ov, 0, None).astype(np.float32) / CMP_LEN
    ov[nc - 1:, :] = 0.0
    ov[:, n_sel:] = 0.0
    return ov


def _select_topk(score, ntop):
    blk = lax.broadcasted_iota(jnp.int32, score.shape, 0).astype(F32)
    for _ in range(ntop):
        m = jnp.max(score, axis=0, keepdims=True)
        first = jnp.min(jnp.where(score == m, blk, float(SEL_LANES)), axis=0, keepdims=True)
        score = jnp.where(blk == first, -jnp.inf, score)
    return score == -jnp.inf


CMP_QSUB = 2


def _cmp_attn_kernel(*refs, nj, n_sel):
    qt_ref, kc_ref, vct_ref, ovt_ref = refs[:4]
    tab_refs = refs[4:4 + nj * CMP_QSUB]
    oct_ref, unsel_ref = refs[4 + nj * CMP_QSUB:]
    i = pl.program_id(2)
    nq = CMP_QSUB * TQ
    s = _dot(kc_ref[...], qt_ref[...])
    s = jnp.concatenate(
        [jnp.concatenate([s[jt * LANES:(jt + 1) * LANES, (qs * HPG + h) * TQ:(qs * HPG + h + 1) * TQ]
                          + tab_refs[qs * nj + jt][h]
                          for qs in range(CMP_QSUB) for h in range(HPG)], axis=1) for jt in range(nj)], axis=0)
    m = jnp.maximum(jnp.max(s, axis=0, keepdims=True), 0.5 * NEG)
    e = jnp.exp2(s - m)
    l = jnp.sum(e, axis=0, keepdims=True)
    p = e * (1.0 / jnp.where(l == 0.0, 1.0, l))
    oct_ref[...] = _dot(vct_ref[...], p.astype(BF16))

    psums = []
    for qs in range(CMP_QSUB):
        acc = p[:, qs * HPG * TQ:(qs * HPG + 1) * TQ]
        for h in range(1, HPG):
            acc = acc + p[:, (qs * HPG + h) * TQ:(qs * HPG + h + 1) * TQ]
        psums.append(acc)
    psum = jnp.concatenate(psums, axis=1)
    p_hi = psum.astype(BF16)
    p_lo = (psum - p_hi.astype(F32)).astype(BF16)
    imp_t = _dot(ovt_ref[...], p_hi) + _dot(ovt_ref[...], p_lo)

    tq = i * nq + lax.broadcasted_iota(jnp.int32, (SEL_LANES, nq), 1)
    j = lax.broadcasted_iota(jnp.int32, (SEL_LANES, nq), 0)
    back = jnp.right_shift(tq, 6) - j
    forced = (j == 0) | ((back >= 0) & (back < N_LOCAL_SEL))
    causal = back >= 0
    n_free = min(SEL_TOPK, n_sel) - (N_LOCAL_SEL + 1)
    picked = _select_topk(jnp.where(causal & ~forced, imp_t, NEG), n_free)
    sel = causal & (forced | picked)
    unsel_ref[...] = jnp.where(sel, 0.0, -1.0).astype(unsel_ref.dtype)


def _cmp_attn(q_t, kvc, kvc_t, tables):
    nb = q_t.shape[0]
    s = q_t.shape[3] // HPG
    nc = s // CMP_STRIDE
    nj = nc // LANES
    n_sel = s // SEL_BLOCK
    ov_t = jnp.asarray(_overlap_np(nc, n_sel).T, dtype=BF16)

    def tab_map(qs, jt):
        def index_map(b, g, i):
            didx = CMP_QSUB * i + qs - (LANES * CMP_STRIDE // TQ) * jt
            row = jnp.where(didx < 0, CMP_MASKED, jnp.minimum(didx, CMP_CONST))
            return (CMP_BASE + row, g, 0, 0)
        return index_map

    nq = CMP_QSUB * TQ
    qt_spec = pl.BlockSpec((None, None, HEAD_DIM, HPG * nq), lambda b, g, i: (b, g, 0, i))
    n_tabs = CMP_QSUB * nj
    return pl.pallas_call(
        functools.partial(_cmp_attn_kernel, nj=nj, n_sel=n_sel),
        out_shape=(jax.ShapeDtypeStruct((nb, KV_GROUPS, HEAD_DIM, HPG * s), F32),
                   jax.ShapeDtypeStruct((nb, KV_GROUPS, SEL_LANES, s), BF16)),
        grid=(nb, KV_GROUPS, s // nq),
        in_specs=[
            qt_spec,
            pl.BlockSpec((None, None, None, nc, HEAD_DIM), lambda b, g, i: (0, b, g, 0, 0)),
            pl.BlockSpec((None, None, None, HEAD_DIM, nc), lambda b, g, i: (1, b, g, 0, 0)),
            pl.BlockSpec((SEL_LANES, nc), lambda b, g, i: (0, 0)),
        ] + [pl.BlockSpec((None, HPG, LANES, TQ), tab_map(qs, jt))
             for qs in range(CMP_QSUB) for jt in range(nj)],
        out_specs=(
            qt_spec,
            pl.BlockSpec((None, None, SEL_LANES, nq), lambda b, g, i: (b, g, 0, i)),
        ),
        compiler_params=_params(("arbitrary", "arbitrary", "arbitrary")),
        name="cmp_attn",
    )(q_t, kvc, kvc_t, ov_t, *([tables] * n_tabs))


class _FlashStream:
    def __init__(self, q_aug, k_ref, vt_ref, bias_fn, s_scr, p_scr):
        self.q_aug, self.k_ref, self.vt_ref, self.bias_fn = q_aug, k_ref, vt_ref, bias_fn
        self.s_scr, self.p_scr = s_scr, p_scr

    def _scores(self, t):
        k0 = pl.multiple_of(jnp.maximum(t, 0) * TK_SEL, TK_SEL)
        return self.bias_fn(t, _dot(self.k_ref[pl.ds(k0, TK_SEL), :], self.q_aug))

    def _pv(self, t, slot):
        k0 = pl.multiple_of(jnp.maximum(t, 0) * TK_SEL, TK_SEL)
        return _dot(self.vt_ref[:, pl.ds(k0, TK_SEL)], self.p_scr[slot])

    def _softmax(self, slot, m):
        s = self.s_scr[slot]
        m_new = jnp.maximum(m, jnp.max(s, axis=0, keepdims=True))
        self.p_scr[slot] = jnp.exp2(s - m_new).astype(self.p_scr.dtype)
        return m_new, jnp.exp2(m - m_new)

    def start(self, u):
        lanes = self.q_aug.shape[1]
        self.s_scr[0] = self._scores(2 * u)
        self.p_scr[1] = jnp.zeros(self.p_scr.shape[1:], self.p_scr.dtype)
        return (jnp.full((1, lanes), NEG, F32), jnp.ones((1, lanes), F32), jnp.zeros((V_AUG, lanes), F32))

    def trip(self, u, carry, has_next):
        m, alpha, acc = carry
        a = 2 * u
        acc = alpha * acc + self._pv(a - 1, 1)
        m, alpha = self._softmax(0, m)
        self.s_scr[1] = self._scores(a + 1)
        acc = alpha * acc + self._pv(a, 0)
        m, alpha = self._softmax(1, m)
        if has_next:
            self.s_scr[0] = self._scores(a + 2)
        return m, alpha, acc

    def finish(self, u_last, carry):
        _, alpha, acc = carry
        acc = alpha * acc + self._pv(2 * u_last + 1, 1)
        return acc[:HEAD_DIM, :] / acc[HEAD_DIM:HEAD_DIM + 1, :]


def _add_tiles(s, tile_fn):
    rows = []
    for part in range(TK_SEL // LANES):
        cols = [s[part * LANES:(part + 1) * LANES, h * TQ:(h + 1) * TQ] + tile_fn(part, h)
                for h in range(HPG)]
        rows.append(jnp.concatenate(cols, axis=1))
    return jnp.concatenate(rows, axis=0)


def _sel_win_kernel(qt_ref, ks_ref, vs_ref, kw_ref, vw_ref, unsel_ref, tab_ref, cvec_ref,
                    oct_ref, gates_ref, y_ref, ss_scr, ps_scr, sw_scr, pw_scr):
    i = pl.program_id(2)
    parts = TK_SEL // LANES

    c = cvec_ref[...]
    c_hi = c.astype(BF16).astype(F32)
    c_lo = c - c_hi
    arow = lax.broadcasted_iota(jnp.int32, (K_AUG - HEAD_DIM, HPG * TQ), 0)
    aug = jnp.where(arow == 0, c_hi, jnp.where(arow == 1, c_lo, 0.0)).astype(BF16)
    q_win = jnp.concatenate([qt_ref[...], aug], axis=0)
    unsel_t = unsel_ref[...]
    q_sel = jnp.concatenate([q_win, jnp.concatenate([unsel_t] * HPG, axis=1)], axis=0)

    def sel_bias(t, s):
        def tile(part, h):
            delta = i - parts * t - part
            row = jnp.where(delta < 0, SEL_MASKED, jnp.minimum(delta, SEL_CONST))
            return tab_ref[row, h]

        return _add_tiles(s, tile)

    def win_bias(t, s):
        def tile(part, h):
            delta = i - parts * t - part
            hidden = (delta < 0) | (delta >= N_WIN_NEAR) | (t < 0)
            return tab_ref[jnp.where(hidden, SEL_MASKED, WIN_BASE + delta), h]

        return _add_tiles(s, tile)

    n_tiles = i // parts + 1
    n_pairs = (n_tiles + 1) // 2
    sel = _FlashStream(q_sel, ks_ref, vs_ref, sel_bias, ss_scr, ps_scr)
    win = _FlashStream(q_win, kw_ref, vw_ref, win_bias, sw_scr, pw_scr)
    win_u0 = n_pairs - 2
    carry_s = sel.start(0)
    carry_w = win.start(win_u0)
    carry_s = lax.fori_loop(0, n_pairs - 1, functools.partial(sel.trip, has_next=True), carry_s)
    carry_s = sel.trip(n_pairs - 1, carry_s, has_next=False)
    carry_w = win.trip(win_u0, carry_w, has_next=True)
    carry_w = win.trip(win_u0 + 1, carry_w, has_next=False)
    os_t = sel.finish(n_pairs - 1, carry_s)
    ow_t = win.finish(win_u0 + 1, carry_w)

    oc_t = oct_ref[...]
    gt = gates_ref[...]
    for hg in range(HPG):
        c0 = N_BRANCH * hg
        lanes = slice(hg * TQ, (hg + 1) * TQ)
        out_t = (gt[c0:c0 + 1, :] * oc_t[:, lanes] + gt[c0 + 1:c0 + 2, :] * os_t[:, lanes]
                 + gt[c0 + 2:c0 + 3, :] * ow_t[:, lanes])
        y_ref[:, hg * HEAD_DIM:(hg + 1) * HEAD_DIM] = jnp.transpose(out_t).astype(y_ref.dtype)


def _sel_win_attn(q_t, ks, vs_t, kw, vw_t, unsel, tables, cvec, oc_t, gates_t):
    nb = q_t.shape[0]
    s = q_t.shape[3] // HPG
    ks_spec = pl.BlockSpec((None, None, s, KS_AUG), lambda b, g, i: (b, g, 0, 0))
    kw_spec = pl.BlockSpec((None, None, s, K_AUG), lambda b, g, i: (b, g, 0, 0))
    v_spec = pl.BlockSpec((None, None, V_AUG, s), lambda b, g, i: (b, g, 0, 0))
    qt_spec = pl.BlockSpec((None, None, HEAD_DIM, HPG * TQ), lambda b, g, i: (b, g, 0, i))
    gw = HPG * HEAD_DIM
    return pl.pallas_call(
        _sel_win_kernel,
        out_shape=jax.ShapeDtypeStruct((nb, s, NSA_HEADS * HEAD_DIM), BF16),
        grid=(nb, KV_GROUPS, s // TQ),
        in_specs=[
            qt_spec,
            ks_spec, v_spec, kw_spec, v_spec,
            pl.BlockSpec((None, None, SEL_LANES, TQ), lambda b, g, i: (b, g, 0, i)),
            pl.BlockSpec((CMP_BASE, HPG, LANES, TQ), lambda b, g, i: (0, g, 0, 0)),
            pl.BlockSpec((None, 1, HPG * TQ), lambda b, g, i: (g, 0, 0)),
            qt_spec,
            pl.BlockSpec((None, None, GATE_ROWS, TQ), lambda b, g, i: (b, g, 0, i)),
        ],
        out_specs=pl.BlockSpec((None, TQ, gw), lambda b, g, i: (b, i, g)),
        scratch_shapes=[pltpu.VMEM((2, TK_SEL, HPG * TQ), F32), pltpu.VMEM((2, TK_SEL, HPG * TQ), BF16),
                        pltpu.VMEM((2, TK_SEL, HPG * TQ), F32), pltpu.VMEM((2, TK_SEL, HPG * TQ), BF16)],
        compiler_params=_params(("arbitrary", "arbitrary", "arbitrary")),
        name="sel_win_attn",
    )(q_t, ks, vs_t, kw, vw_t, unsel, tables, cvec, oc_t, gates_t)


def kernel(x, mem, mix_norm, a_w_in, a_conv_w, b_w_in, b_q_gain, kv_norm, w_kv_shared, k_gain,
           cmp_pos, cmp_w1, cmp_w2, rel_bias, mem_norm, mem_w_kv, mem_q_gain, mem_k_gain, w_out,
           mlp_norm, w_up, w_down):
    nb, s, d = x.shape
    depth = mix_norm.shape[0]
    n_a = a_w_in.shape[0]
    assert s % (LANES * CMP_STRIDE) == 0 and s // SEL_BLOCK <= SEL_LANES
    assert WIN_BASE + N_WIN_NEAR <= CMP_BASE

    mk, mv_t = _mem_kv(mem, mem_norm, mem_w_kv, mem_k_gain)
    tables = _bias_tables(rel_bias)
    cvec = jnp.repeat(rel_bias[REL_BUCKETS - 1].astype(F32) * LOG2E, TQ).reshape(KV_GROUPS, 1, HPG * TQ)
    shared = None
    for layer in range(depth):
        if layer < n_a:
            y_tok, y_mem = _mixer_a(x, mix_norm[layer], a_w_in[layer], a_conv_w[layer],
                                    mk[layer], mv_t[layer], mem_q_gain[layer])
        else:
            j = layer - n_a
            ks, vs_t, kw, vw_t, kvc, kvc_t = shared
            q_t, gates_t, y_mem = _b_inproj(x, mix_norm[layer], b_w_in[j], b_q_gain[j],
                                            mk[layer], mv_t[layer], mem_q_gain[layer])
            oc_t, unsel = _cmp_attn(q_t, kvc, kvc_t, tables)
            y_tok = _sel_win_attn(q_t, ks, vs_t, kw, vw_t, unsel, tables, cvec, oc_t, gates_t)
        x = _post(x.reshape(nb * s, d), y_tok.reshape(nb * s, -1), y_mem.reshape(nb * s, -1),
                  w_out[layer], mlp_norm[layer], w_up[layer], w_down[layer]).reshape(nb, s, d)
        if layer == n_a - 1:
            ucmp, ks, vs_t, kw, vw_t = _build_kv(x, kv_norm, w_kv_shared, k_gain)
            kvc, kvc_t = _compress(ucmp, cmp_pos, cmp_w1, cmp_w2, k_gain[0])
            shared = (ks, vs_t, kw, vw_t, kvc, kvc_t)
    return x
```

```python
import functools
import math

import numpy as np
import jax
import jax.numpy as jnp
from jax import lax
from jax.experimental import pallas as pl
from jax.experimental.pallas import tpu as pltpu

F32 = jnp.float32
BF16 = jnp.bfloat16

HEAD_DIM = 64
MEM_HEADS = 4
MEM_WIDTH = MEM_HEADS * HEAD_DIM
CONV_CH = 768
NSA_HEADS = 12
KV_GROUPS = 2
HPG = NSA_HEADS // KV_GROUPS
N_BRANCH = 3
CMP_LEN = 32
CMP_STRIDE = 16
SEL_BLOCK = 64
SEL_TOPK = 16
N_LOCAL_SEL = 2
WINDOW = 512
REL_BUCKETS = 32
REL_MAX_DIST = 1024
EPS = 1e-6
NEG = -1e30
FORCE_SCORE = 1e4
LOG2E = math.log2(math.e)
MASK_BIG = 2.0 ** 100

LANES = 128
SUBLANES = 8
VMEM_LIMIT_BYTES = 56 * 1024 * 1024

TQ = 128
TK_SEL = 256
SEL_LANES = 128
TM = 512

N_SEL_NEAR = 8
SEL_CONST = N_SEL_NEAR
SEL_MASKED = N_SEL_NEAR + 1
WIN_BASE = N_SEL_NEAR + 2
N_WIN_NEAR = WINDOW // LANES + 1
CMP_BASE = 16
N_CMP_NEAR = 23
CMP_CONST = N_CMP_NEAR
CMP_MASKED = N_CMP_NEAR + 1
N_CMP_TILES = N_CMP_NEAR + 2
N_TILES = CMP_BASE + N_CMP_TILES


def _rms(xf, g):
    ms = jnp.mean(xf * xf, axis=-1, keepdims=True)
    return xf * lax.rsqrt(ms + EPS) * g


def _dot(a, b):
    return jnp.dot(a, b, preferred_element_type=F32)


def _dot_nt(a, b):
    return lax.dot_general(a, b, (((1,), (1,)), ((), ())), preferred_element_type=F32)


def _params(sem):
    return pltpu.CompilerParams(dimension_semantics=sem, vmem_limit_bytes=VMEM_LIMIT_BYTES)


def _mem_kv_kernel(mem_ref, mnorm_ref, w_ref, kg_ref, mk_ref, mvt_ref, *, nb, ml):
    mh = _rms(mem_ref[...], mnorm_ref[...]).astype(BF16)
    kv = _dot(mh, w_ref[...])
    for h in range(MEM_HEADS):
        kh = _rms(kv[:, h * HEAD_DIM:(h + 1) * HEAD_DIM], kg_ref[...]).astype(BF16)
        vh = kv[:, MEM_WIDTH + h * HEAD_DIM:MEM_WIDTH + (h + 1) * HEAD_DIM]
        for b in range(nb):
            mk_ref[b, h] = kh[b * ml:(b + 1) * ml]
            mvt_ref[b, h] = jnp.transpose(vh[b * ml:(b + 1) * ml]).astype(BF16)


def _mem_kv(mem, mem_norm, mem_w_kv, mem_k_gain):
    nb, ml, d = mem.shape
    depth = mem_w_kv.shape[0]
    out = jax.ShapeDtypeStruct((depth, nb, MEM_HEADS, ml, HEAD_DIM), BF16)
    out_t = jax.ShapeDtypeStruct((depth, nb, MEM_HEADS, HEAD_DIM, ml), BF16)
    return pl.pallas_call(
        functools.partial(_mem_kv_kernel, nb=nb, ml=ml),
        out_shape=(out, out_t),
        grid=(depth,),
        in_specs=[
            pl.BlockSpec((nb * ml, d), lambda l: (0, 0)),
            pl.BlockSpec((1, d), lambda l: (0, 0)),
            pl.BlockSpec((None, d, 2 * MEM_WIDTH), lambda l: (l, 0, 0)),
            pl.BlockSpec((None, 1, HEAD_DIM), lambda l: (l, 0, 0)),
        ],
        out_specs=(
            pl.BlockSpec((None, nb, MEM_HEADS, ml, HEAD_DIM), lambda l: (l, 0, 0, 0, 0)),
            pl.BlockSpec((None, nb, MEM_HEADS, HEAD_DIM, ml), lambda l: (l, 0, 0, 0, 0)),
        ),
        compiler_params=_params(("arbitrary",)),
        name="mem_kv",
    )(mem.reshape(nb * ml, d), mem_norm.reshape(1, d), mem_w_kv.astype(BF16),
      mem_k_gain.reshape(depth, 1, HEAD_DIM))


def _rms_t(xt, g_col):
    ms = jnp.mean(xt * xt, axis=0, keepdims=True)
    return xt * lax.rsqrt(ms + EPS) * g_col


def _mem_attn(qm, mk_ref, mvt_ref, qg_col, ymem_ref):
    qm_t = jnp.transpose(qm)
    outs = []
    for h in range(MEM_HEADS):
        qh = _rms_t(qm_t[h * HEAD_DIM:(h + 1) * HEAD_DIM, :], qg_col) * (HEAD_DIM ** -0.5)
        lg = _dot(mk_ref[h], qh.astype(BF16))
        m = jnp.max(lg, axis=0, keepdims=True)
        e = jnp.exp(lg - m)
        l = jnp.sum(e, axis=0, keepdims=True)
        outs.append(_dot(mvt_ref[h], e.astype(BF16)) / l)
    ymem_ref[...] = jnp.transpose(jnp.concatenate(outs, axis=0)).astype(ymem_ref.dtype)


def _bucket_np(d):
    n = np.maximum(d, 0)
    max_exact = REL_BUCKETS // 2
    nf = np.maximum(n, 1).astype(np.float64)
    large = max_exact + (np.log(nf / max_exact) / math.log(REL_MAX_DIST / max_exact)
                         * (REL_BUCKETS - max_exact)).astype(np.int32)
    large = np.minimum(large, REL_BUCKETS - 1)
    return np.where(n < max_exact, n, large).astype(np.int32)


def _bucket_tiles():
    r = np.arange(TQ)[:, None]
    k = np.arange(LANES)[None, :]
    tiles = []
    for delta in range(N_SEL_NEAR):
        d = delta * LANES + r - k
        tiles.append(np.where(d >= 0, _bucket_np(d), -1).T)
    tiles.append(np.full((TQ, LANES), REL_BUCKETS - 1))
    tiles.append(np.full((TQ, LANES), -1))
    for delta in range(N_WIN_NEAR):
        d = delta * LANES + r - k
        tiles.append(np.where((d >= 0) & (d < WINDOW), _bucket_np(d), -1).T)
    while len(tiles) < CMP_BASE:
        tiles.append(np.full((TQ, LANES), -1))
    for delta in range(N_CMP_NEAR):
        d = delta * LANES + r - CMP_STRIDE * k - (CMP_LEN - 1)
        tiles.append(np.where(d >= 0, _bucket_np(d), -1).T)
    tiles.append(np.full((TQ, LANES), REL_BUCKETS - 1))
    tiles.append(np.full((TQ, LANES), -1))
    return np.stack(tiles).astype(np.int32)


def _bias_tables_kernel(tab_ref, ids_ref, out_ref):
    relative = pl.program_id(0) < CMP_BASE
    n_bits = REL_BUCKETS.bit_length() - 1
    for h in range(NSA_HEADS):
        sub = jnp.where(relative, tab_ref[REL_BUCKETS - 1, h], 0.0)
        leaves = [(jnp.full((SUBLANES, LANES), tab_ref[b, h], F32) - sub) * LOG2E for b in range(REL_BUCKETS)]
        for r0 in range(0, ids_ref.shape[0], SUBLANES):
            ids = ids_ref[r0:r0 + SUBLANES, :]
            level = leaves
            for k in range(n_bits):
                bit = jnp.bitwise_and(ids, 1 << k) != 0
                level = [jnp.where(bit, level[2 * j + 1], level[2 * j]) for j in range(len(level) // 2)]
            out_ref[h, r0:r0 + SUBLANES, :] = jnp.where(ids < 0, NEG, level[0])


def _bias_tables(rel_bias):
    ids = jnp.asarray(_bucket_tiles())
    return pl.pallas_call(
        _bias_tables_kernel,
        out_shape=jax.ShapeDtypeStruct((N_TILES, NSA_HEADS, TQ, LANES), F32),
        grid=(N_TILES,),
        in_specs=[
            pl.BlockSpec(memory_space=pltpu.SMEM),
            pl.BlockSpec((None, TQ, LANES), lambda t: (t, 0, 0)),
        ],
        out_specs=pl.BlockSpec((None, NSA_HEADS, TQ, LANES), lambda t: (t, 0, 0, 0)),
        compiler_params=_params(("arbitrary",)),
        name="bias_tables",
    )(rel_bias.astype(F32), ids)


def _mixer_a_kernel(x_ref, g_ref, w_ref, cw_ref, mk_ref, mvt_ref, qg_ref,
                    ytok_ref, ymem_ref, carry_ref, *, tm):
    @pl.when(pl.program_id(1) == 0)
    def _():
        carry_ref[...] = jnp.zeros_like(carry_ref)

    h = _rms(x_ref[...], g_ref[...]).astype(BF16)
    z = _dot(h, w_ref[...])
    gate_b = z[:, :CONV_CH]
    v = z[:, CONV_CH:2 * CONV_CH] * z[:, 2 * CONV_CH:3 * CONV_CH]
    prev = carry_ref[...]
    row = lax.broadcasted_iota(jnp.int32, (tm, CONV_CH), 0)
    v1 = jnp.where(row == 0, prev[7:8, :], pltpu.roll(v, 1, 0))
    v2 = jnp.where(row == 0, prev[6:7, :], jnp.where(row == 1, prev[7:8, :], pltpu.roll(v, 2, 0)))
    cw = cw_ref[...]
    y = gate_b * (cw[0:1, :] * v2 + cw[1:2, :] * v1 + cw[2:3, :] * v)
    carry_ref[...] = v[tm - 8:, :]
    ytok_ref[...] = y.astype(ytok_ref.dtype)
    _mem_attn(z[:, 3 * CONV_CH:], mk_ref, mvt_ref, qg_ref[...], ymem_ref)


def _mixer_a(x, norm_g, w_in, conv_w, mk, mv_t, mem_q_gain):
    nb, s, d = x.shape
    ml = mk.shape[2]
    n_in = w_in.shape[1]
    return pl.pallas_call(
        functools.partial(_mixer_a_kernel, tm=TM),
        out_shape=(jax.ShapeDtypeStruct((nb, s, CONV_CH), BF16),
                   jax.ShapeDtypeStruct((nb, s, MEM_WIDTH), BF16)),
        grid=(nb, s // TM),
        in_specs=[
            pl.BlockSpec((None, TM, d), lambda b, i: (b, i, 0)),
            pl.BlockSpec((1, d), lambda b, i: (0, 0)),
            pl.BlockSpec((d, n_in), lambda b, i: (0, 0)),
            pl.BlockSpec((3, CONV_CH), lambda b, i: (0, 0)),
            pl.BlockSpec((None, MEM_HEADS, ml, HEAD_DIM), lambda b, i: (b, 0, 0, 0)),
            pl.BlockSpec((None, MEM_HEADS, HEAD_DIM, ml), lambda b, i: (b, 0, 0, 0)),
            pl.BlockSpec((HEAD_DIM, 1), lambda b, i: (0, 0)),
        ],
        out_specs=(
            pl.BlockSpec((None, TM, CONV_CH), lambda b, i: (b, i, 0)),
            pl.BlockSpec((None, TM, MEM_WIDTH), lambda b, i: (b, i, 0)),
        ),
        scratch_shapes=[pltpu.VMEM((8, CONV_CH), F32)],
        compiler_params=_params(("arbitrary", "arbitrary")),
        name="mixer_a",
    )(x, norm_g.reshape(1, d), w_in.astype(BF16), conv_w, mk, mv_t, mem_q_gain.reshape(HEAD_DIM, 1))


B_GATE0 = NSA_HEADS * HEAD_DIM
B_QMEM0 = B_GATE0 + KV_GROUPS * LANES
B_WIDTH = B_QMEM0 + MEM_WIDTH


GATE_ROWS = 32


def _b_inproj_kernel(x_ref, g_ref, w_ref, qg_ref, mk_ref, mvt_ref, mqg_ref,
                     qt_ref, gates_ref, ymem_ref):
    h = _rms(x_ref[...], g_ref[...]).astype(BF16)
    z = _dot(h, w_ref[...])
    qg = qg_ref[...]
    tm = z.shape[0]
    for pair in range(NSA_HEADS // 2):
        z_t = jnp.transpose(z[:, pair * LANES:(pair + 1) * LANES])
        for half in range(2):
            qh_t = _rms_t(z_t[half * HEAD_DIM:(half + 1) * HEAD_DIM, :], qg) * (HEAD_DIM ** -0.5 * LOG2E)
            qh_t = qh_t.astype(qt_ref.dtype)
            g, hg = divmod(2 * pair + half, HPG)
            for qt in range(tm // TQ):
                lane0 = (qt * HPG + hg) * TQ
                qt_ref[g, :, lane0:lane0 + TQ] = qh_t[:, qt * TQ:(qt + 1) * TQ]
    for g in range(KV_GROUPS):
        gl = z[:, B_GATE0 + g * LANES:B_GATE0 + (g + 1) * LANES]
        gates_ref[g] = jnp.transpose(1.0 / (1.0 + jnp.exp(-gl)))[:GATE_ROWS, :]
    _mem_attn(z[:, B_QMEM0:], mk_ref, mvt_ref, mqg_ref[...], ymem_ref)


def _b_inproj(x, norm_g, w_in, q_gain, mk, mv_t, mem_q_gain):
    nb, s, d = x.shape
    ml = mk.shape[2]
    nq = NSA_HEADS * HEAD_DIM
    ng = HPG * N_BRANCH
    pad = jnp.zeros((d, LANES - ng), w_in.dtype)
    w = jnp.concatenate([w_in[:, :nq], w_in[:, nq:nq + ng], pad,
                         w_in[:, nq + ng:nq + 2 * ng], pad, w_in[:, nq + 2 * ng:]], axis=1)
    return pl.pallas_call(
        _b_inproj_kernel,
        out_shape=(jax.ShapeDtypeStruct((nb, KV_GROUPS, HEAD_DIM, HPG * s), BF16),
                   jax.ShapeDtypeStruct((nb, KV_GROUPS, GATE_ROWS, s), F32),
                   jax.ShapeDtypeStruct((nb, s, MEM_WIDTH), BF16)),
        grid=(nb, s // TM),
        in_specs=[
            pl.BlockSpec((None, TM, d), lambda b, i: (b, i, 0)),
            pl.BlockSpec((1, d), lambda b, i: (0, 0)),
            pl.BlockSpec((d, B_WIDTH), lambda b, i: (0, 0)),
            pl.BlockSpec((HEAD_DIM, 1), lambda b, i: (0, 0)),
            pl.BlockSpec((None, MEM_HEADS, ml, HEAD_DIM), lambda b, i: (b, 0, 0, 0)),
            pl.BlockSpec((None, MEM_HEADS, HEAD_DIM, ml), lambda b, i: (b, 0, 0, 0)),
            pl.BlockSpec((HEAD_DIM, 1), lambda b, i: (0, 0)),
        ],
        out_specs=(
            pl.BlockSpec((None, KV_GROUPS, HEAD_DIM, HPG * TM), lambda b, i: (b, 0, 0, i)),
            pl.BlockSpec((None, KV_GROUPS, GATE_ROWS, TM), lambda b, i: (b, 0, 0, i)),
            pl.BlockSpec((None, TM, MEM_WIDTH), lambda b, i: (b, i, 0)),
        ),
        compiler_params=_params(("arbitrary", "arbitrary")),
        name="b_inproj",
    )(x, norm_g.reshape(1, d), w.astype(BF16), q_gain.reshape(HEAD_DIM, 1), mk, mv_t,
      mem_q_gain.reshape(HEAD_DIM, 1))


FF_CHUNK = 1024
TM_POST = 1024


def _post_kernel(x_ref, ytok_ref, ymem_ref, woa_ref, wob_ref, g_ref, wup_ref, wdn_ref, o_ref, hm_ref):
    @pl.when(pl.program_id(1) == 0)
    def _():
        x1 = x_ref[...] + _dot(ytok_ref[...], woa_ref[...]) + _dot(ymem_ref[...], wob_ref[...])
        o_ref[...] = x1
        hm_ref[...] = _rms(x1, g_ref[...]).astype(hm_ref.dtype)

    a = _dot(hm_ref[...], wup_ref[...].astype(BF16))
    a = jnp.square(jnp.maximum(a, 0.0)).astype(BF16)
    o_ref[...] += _dot(a, wdn_ref[...].astype(BF16))


def _post(x, y_tok, y_mem, w_out, norm_g, w_up, w_down):
    n, d = x.shape
    ff = w_up.shape[1]
    nt = y_tok.shape[1]
    tm = min(TM_POST, n)
    const = lambda i, c: (0, 0)
    rows = lambda i, c: (i, 0)
    return pl.pallas_call(
        _post_kernel,
        out_shape=jax.ShapeDtypeStruct((n, d), F32),
        grid=(n // tm, ff // FF_CHUNK),
        in_specs=[
            pl.BlockSpec((tm, d), rows),
            pl.BlockSpec((tm, nt), rows),
            pl.BlockSpec((tm, MEM_WIDTH), rows),
            pl.BlockSpec((nt, d), const),
            pl.BlockSpec((MEM_WIDTH, d), const),
            pl.BlockSpec((1, d), const),
            pl.BlockSpec((d, FF_CHUNK), lambda i, c: (0, c)),
            pl.BlockSpec((FF_CHUNK, d), lambda i, c: (c, 0)),
        ],
        out_specs=pl.BlockSpec((tm, d), rows),
        scratch_shapes=[pltpu.VMEM((tm, d), BF16)],
        compiler_params=_params(("arbitrary", "arbitrary")),
        name="post",
    )(x, y_tok, y_mem, w_out[:nt].astype(BF16), w_out[nt:].astype(BF16), norm_g.reshape(1, d),
      w_up, w_down)


K_AUG = 2 * HEAD_DIM
KS_AUG = K_AUG + SEL_LANES
V_AUG = HEAD_DIM + 16


def _build_kv_kernel(x_ref, g_ref, w_ref, kg_ref, ucmp_ref, ks_ref, vs_ref, kw_ref, vw_ref, raw_scr):
    h = _rms(x_ref[...], g_ref[...]).astype(BF16)
    kv = _dot(h, w_ref[...])
    gw = KV_GROUPS * HEAD_DIM
    tm = kv.shape[0]
    for ty in range(2):
        raw_scr[ty] = kv[:, ty * gw:(ty + 1) * gw]
        for t in range(CMP_STRIDE):
            rows = raw_scr[ty, pl.ds(t, tm // CMP_STRIDE, stride=CMP_STRIDE), :]
            for g in range(KV_GROUPS):
                ucmp_ref[ty, g, :, t * HEAD_DIM:(t + 1) * HEAD_DIM] = (
                    rows[:, g * HEAD_DIM:(g + 1) * HEAD_DIM].astype(ucmp_ref.dtype))
    kg = kg_ref[...]
    ones = (lax.broadcasted_iota(jnp.int32, (tm, K_AUG - HEAD_DIM), 1) < 2).astype(ks_ref.dtype)
    pos = pl.program_id(1) * tm + lax.broadcasted_iota(jnp.int32, (tm, SEL_LANES), 0)
    blk = lax.broadcasted_iota(jnp.int32, (tm, SEL_LANES), 1)
    onehot = jnp.where(blk == jnp.right_shift(pos, 6), MASK_BIG, 0.0).astype(ks_ref.dtype)
    ones_row = (lax.broadcasted_iota(jnp.int32, (V_AUG - HEAD_DIM, tm), 0) == 0).astype(vs_ref.dtype)
    for g in range(KV_GROUPS):
        sl = lambda t: kv[:, t * gw + g * HEAD_DIM:t * gw + (g + 1) * HEAD_DIM]
        ks_ref[g, :, :HEAD_DIM] = _rms(sl(2), kg[1:2, :]).astype(ks_ref.dtype)
        ks_ref[g, :, HEAD_DIM:K_AUG] = ones
        ks_ref[g, :, K_AUG:] = onehot
        vs_ref[g, :HEAD_DIM, :] = jnp.transpose(sl(3)).astype(vs_ref.dtype)
        vs_ref[g, HEAD_DIM:, :] = ones_row
        kw_ref[g, :, :HEAD_DIM] = _rms(sl(4), kg[2:3, :]).astype(kw_ref.dtype)
        kw_ref[g, :, HEAD_DIM:] = ones
        vw_ref[g, :HEAD_DIM, :] = jnp.transpose(sl(5)).astype(vw_ref.dtype)
        vw_ref[g, HEAD_DIM:, :] = ones_row


def _build_kv(x, kv_norm, w_kv, k_gain):
    nb, s, d = x.shape
    gw = KV_GROUPS * HEAD_DIM
    ks = jax.ShapeDtypeStruct((nb, KV_GROUPS, s, KS_AUG), BF16)
    kw = jax.ShapeDtypeStruct((nb, KV_GROUPS, s, K_AUG), BF16)
    vt = jax.ShapeDtypeStruct((nb, KV_GROUPS, V_AUG, s), BF16)
    ks_spec = pl.BlockSpec((None, KV_GROUPS, TM, KS_AUG), lambda b, i: (b, 0, i, 0))
    kw_spec = pl.BlockSpec((None, KV_GROUPS, TM, K_AUG), lambda b, i: (b, 0, i, 0))
    v_spec = pl.BlockSpec((None, KV_GROUPS, V_AUG, TM), lambda b, i: (b, 0, 0, i))
    return pl.pallas_call(
        _build_kv_kernel,
        out_shape=(jax.ShapeDtypeStruct((2, nb, KV_GROUPS, s // CMP_STRIDE, CMP_STRIDE * HEAD_DIM), BF16),
                   ks, vt, kw, vt),
        grid=(nb, s // TM),
        in_specs=[
            pl.BlockSpec((None, TM, d), lambda b, i: (b, i, 0)),
            pl.BlockSpec((1, d), lambda b, i: (0, 0)),
            pl.BlockSpec((d, 6 * gw), lambda b, i: (0, 0)),
            pl.BlockSpec((N_BRANCH, HEAD_DIM), lambda b, i: (0, 0)),
        ],
        out_specs=(pl.BlockSpec((2, None, KV_GROUPS, TM // CMP_STRIDE, CMP_STRIDE * HEAD_DIM),
                                lambda b, i: (0, b, 0, i, 0)),
                   ks_spec, v_spec, kw_spec, v_spec),
        scratch_shapes=[pltpu.VMEM((2, TM, gw), F32)],
        compiler_params=_params(("arbitrary", "arbitrary")),
        name="build_kv",
    )(x, kv_norm.reshape(1, d), w_kv.astype(BF16), k_gain)


def _compress_kernel(u_ref, w1_ref, w2_ref, pos_ref, kg_ref, out_ref, out_t_ref, *, nc):
    half = CMP_STRIDE * HEAD_DIM
    u = u_ref[...]
    lo = _dot(u, w1_ref[:half, :])
    hi = _dot(u, w1_ref[half:, :])
    posb = _dot(jnp.broadcast_to(pos_ref[...], (8, 2 * half)).astype(BF16), w1_ref[...])[0:1, :]
    h1 = lo + pltpu.roll(hi, nc - 1, 0) + posb
    hid = 0.5 * h1 * (1.0 + jnp.tanh(math.sqrt(2.0 / math.pi) * (h1 + 0.044715 * (h1 * h1 * h1))))
    c = _dot(hid.astype(BF16), w2_ref[...])
    is_key = pl.program_id(0) == 0
    out = jnp.where(is_key, _rms(c, kg_ref[...]), c)
    out_ref[...] = out.astype(out_ref.dtype)
    out_t_ref[...] = jnp.transpose(out).astype(out_t_ref.dtype)


def _compress(u, cmp_pos, cmp_w1, cmp_w2, k_gain0):
    _, nb, _, nc, width = u.shape
    hidden = cmp_w1.shape[2]
    return pl.pallas_call(
        functools.partial(_compress_kernel, nc=nc),
        out_shape=(jax.ShapeDtypeStruct((2, nb, KV_GROUPS, nc, HEAD_DIM), BF16),
                   jax.ShapeDtypeStruct((2, nb, KV_GROUPS, HEAD_DIM, nc), BF16)),
        grid=(2, nb, KV_GROUPS),
        in_specs=[
            pl.BlockSpec((None, None, None, nc, width), lambda t, b, g: (t, b, g, 0, 0)),
            pl.BlockSpec((None, 2 * width, hidden), lambda t, b, g: (t, 0, 0)),
            pl.BlockSpec((None, hidden, HEAD_DIM), lambda t, b, g: (t, 0, 0)),
            pl.BlockSpec((None, 1, 2 * width), lambda t, b, g: (t, 0, 0)),
            pl.BlockSpec((1, HEAD_DIM), lambda t, b, g: (0, 0)),
        ],
        out_specs=(pl.BlockSpec((None, None, None, nc, HEAD_DIM), lambda t, b, g: (t, b, g, 0, 0)),
                   pl.BlockSpec((None, None, None, HEAD_DIM, nc), lambda t, b, g: (t, b, g, 0, 0))),
        compiler_params=_params(("arbitrary", "arbitrary", "arbitrary")),
        name="compress",
    )(u, cmp_w1.astype(BF16), cmp_w2.astype(BF16), cmp_pos.reshape(2, 1, 2 * width),
      k_gain0.reshape(1, HEAD_DIM))


def _overlap_np(nc, n_sel):
    c0 = np.arange(nc)[:, None] * CMP_STRIDE
    s0 = np.arange(SEL_LANES)[None, :] * SEL_BLOCK
    ov = np.minimum(c0 + CMP_LEN, s0 + SEL_BLOCK) - np.maximum(c0, s0)
    ov = np.clip(ov, 0, None).astype(np.float32) / CMP_LEN
    ov[nc - 1:, :] = 0.0
    ov[:, n_sel:] = 0.0
    return ov


def _select_topk(score, ntop):
    blk = lax.broadcasted_iota(jnp.int32, score.shape, 0).astype(F32)
    for _ in range(ntop):
        m = jnp.max(score, axis=0, keepdims=True)
        first = jnp.min(jnp.where(score == m, blk, float(SEL_LANES)), axis=0, keepdims=True)
        score = jnp.where(blk == first, -jnp.inf, score)
    return score == -jnp.inf


CMP_QSUB = 2


def _cmp_attn_kernel(*refs, nj, n_sel):
    qt_ref, kc_ref, vct_ref, ovt_ref = refs[:4]
    tab_refs = refs[4:4 + nj * CMP_QSUB]
    oct_ref, unsel_ref = refs[4 + nj * CMP_QSUB:]
    i = pl.program_id(2)
    nq = CMP_QSUB * TQ
    s = _dot(kc_ref[...], qt_ref[...])
    s = jnp.concatenate(
        [jnp.concatenate([s[jt * LANES:(jt + 1) * LANES, (qs * HPG + h) * TQ:(qs * HPG + h + 1) * TQ]
                          + tab_refs[qs * nj + jt][h]
                          for qs in range(CMP_QSUB) for h in range(HPG)], axis=1) for jt in range(nj)], axis=0)
    m = jnp.maximum(jnp.max(s, axis=0, keepdims=True), 0.5 * NEG)
    e = jnp.exp2(s - m)
    l = jnp.sum(e, axis=0, keepdims=True)
    p = e * (1.0 / jnp.where(l == 0.0, 1.0, l))
    oct_ref[...] = _dot(vct_ref[...], p.astype(BF16))

    psums = []
    for qs in range(CMP_QSUB):
        acc = p[:, qs * HPG * TQ:(qs * HPG + 1) * TQ]
        for h in range(1, HPG):
            acc = acc + p[:, (qs * HPG + h) * TQ:(qs * HPG + h + 1) * TQ]
        psums.append(acc)
    psum = jnp.concatenate(psums, axis=1)
    p_hi = psum.astype(BF16)
    p_lo = (psum - p_hi.astype(F32)).astype(BF16)
    imp_t = _dot(ovt_ref[...], p_hi) + _dot(ovt_ref[...], p_lo)

    tq = i * nq + lax.broadcasted_iota(jnp.int32, (SEL_LANES, nq), 1)
    j = lax.broadcasted_iota(jnp.int32, (SEL_LANES, nq), 0)
    back = jnp.right_shift(tq, 6) - j
    forced = (j == 0) | ((back >= 0) & (back < N_LOCAL_SEL))
    causal = back >= 0
    n_free = min(SEL_TOPK, n_sel) - (N_LOCAL_SEL + 1)
    picked = _select_topk(jnp.where(causal & ~forced, imp_t, NEG), n_free)
    sel = causal & (forced | picked)
    unsel_ref[...] = jnp.where(sel, 0.0, -1.0).astype(unsel_ref.dtype)


def _cmp_attn(q_t, kvc, kvc_t, tables):
    nb = q_t.shape[0]
    s = q_t.shape[3] // HPG
    nc = s // CMP_STRIDE
    nj = nc // LANES
    n_sel = s // SEL_BLOCK
    ov_t = jnp.asarray(_overlap_np(nc, n_sel).T, dtype=BF16)

    def tab_map(qs, jt):
        def index_map(b, g, i):
            didx = CMP_QSUB * i + qs - (LANES * CMP_STRIDE // TQ) * jt
            row = jnp.where(didx < 0, CMP_MASKED, jnp.minimum(didx, CMP_CONST))
            return (CMP_BASE + row, g, 0, 0)
        return index_map

    nq = CMP_QSUB * TQ
    qt_spec = pl.BlockSpec((None, None, HEAD_DIM, HPG * nq), lambda b, g, i: (b, g, 0, i))
    n_tabs = CMP_QSUB * nj
    return pl.pallas_call(
        functools.partial(_cmp_attn_kernel, nj=nj, n_sel=n_sel),
        out_shape=(jax.ShapeDtypeStruct((nb, KV_GROUPS, HEAD_DIM, HPG * s), F32),
                   jax.ShapeDtypeStruct((nb, KV_GROUPS, SEL_LANES, s), BF16)),
        grid=(nb, KV_GROUPS, s // nq),
        in_specs=[
            qt_spec,
            pl.BlockSpec((None, None, None, nc, HEAD_DIM), lambda b, g, i: (0, b, g, 0, 0)),
            pl.BlockSpec((None, None, None, HEAD_DIM, nc), lambda b, g, i: (1, b, g, 0, 0)),
            pl.BlockSpec((SEL_LANES, nc), lambda b, g, i: (0, 0)),
        ] + [pl.BlockSpec((None, HPG, LANES, TQ), tab_map(qs, jt))
             for qs in range(CMP_QSUB) for jt in range(nj)],
        out_specs=(
            qt_spec,
            pl.BlockSpec((None, None, SEL_LANES, nq), lambda b, g, i: (b, g, 0, i)),
        ),
        compiler_params=_params(("arbitrary", "arbitrary", "arbitrary")),
        name="cmp_attn",
    )(q_t, kvc, kvc_t, ov_t, *([tables] * n_tabs))


class _FlashStream:
    def __init__(self, q_aug, k_ref, vt_ref, bias_fn, s_scr, p_scr):
        self.q_aug, self.k_ref, self.vt_ref, self.bias_fn = q_aug, k_ref, vt_ref, bias_fn
        self.s_scr, self.p_scr = s_scr, p_scr

    def _scores(self, t):
        k0 = pl.multiple_of(jnp.maximum(t, 0) * TK_SEL, TK_SEL)
        return self.bias_fn(t, _dot(self.k_ref[pl.ds(k0, TK_SEL), :], self.q_aug))

    def _pv(self, t, slot):
        k0 = pl.multiple_of(jnp.maximum(t, 0) * TK_SEL, TK_SEL)
        return _dot(self.vt_ref[:, pl.ds(k0, TK_SEL)], self.p_scr[slot])

    def _softmax(self, slot, m):
        s = self.s_scr[slot]
        m_new = jnp.maximum(m, jnp.max(s, axis=0, keepdims=True))
        self.p_scr[slot] = jnp.exp2(s - m_new).astype(self.p_scr.dtype)
        return m_new, jnp.exp2(m - m_new)

    def start(self, u):
        lanes = self.q_aug.shape[1]
        self.s_scr[0] = self._scores(2 * u)
        self.p_scr[1] = jnp.zeros(self.p_scr.shape[1:], self.p_scr.dtype)
        return (jnp.full((1, lanes), NEG, F32), jnp.ones((1, lanes), F32), jnp.zeros((V_AUG, lanes), F32))

    def trip(self, u, carry, has_next):
        m, alpha, acc = carry
        a = 2 * u
        acc = alpha * acc + self._pv(a - 1, 1)
        m, alpha = self._softmax(0, m)
        self.s_scr[1] = self._scores(a + 1)
        acc = alpha * acc + self._pv(a, 0)
        m, alpha = self._softmax(1, m)
        if has_next:
            self.s_scr[0] = self._scores(a + 2)
        return m, alpha, acc

    def finish(self, u_last, carry):
        _, alpha, acc = carry
        acc = alpha * acc + self._pv(2 * u_last + 1, 1)
        return acc[:HEAD_DIM, :] / acc[HEAD_DIM:HEAD_DIM + 1, :]


def _add_tiles(s, tile_fn):
    rows = []
    for part in range(TK_SEL // LANES):
        cols = [s[part * LANES:(part + 1) * LANES, h * TQ:(h + 1) * TQ] + tile_fn(part, h)
                for h in range(HPG)]
        rows.append(jnp.concatenate(cols, axis=1))
    return jnp.concatenate(rows, axis=0)


def _sel_win_kernel(qt_ref, ks_ref, vs_ref, kw_ref, vw_ref, unsel_ref, tab_ref, cvec_ref,
                    oct_ref, gates_ref, y_ref, ss_scr, ps_scr, sw_scr, pw_scr):
    i = pl.program_id(2)
    parts = TK_SEL // LANES

    c = cvec_ref[...]
    c_hi = c.astype(BF16).astype(F32)
    c_lo = c - c_hi
    arow = lax.broadcasted_iota(jnp.int32, (K_AUG - HEAD_DIM, HPG * TQ), 0)
    aug = jnp.where(arow == 0, c_hi, jnp.where(arow == 1, c_lo, 0.0)).astype(BF16)
    q_win = jnp.concatenate([qt_ref[...], aug], axis=0)
    unsel_t = unsel_ref[...]
    q_sel = jnp.concatenate([q_win, jnp.concatenate([unsel_t] * HPG, axis=1)], axis=0)

    def sel_bias(t, s):
        def tile(part, h):
            delta = i - parts * t - part
            row = jnp.where(delta < 0, SEL_MASKED, jnp.minimum(delta, SEL_CONST))
            return tab_ref[row, h]

        return _add_tiles(s, tile)

    def win_bias(t, s):
        def tile(part, h):
            delta = i - parts * t - part
            hidden = (delta < 0) | (delta >= N_WIN_NEAR) | (t < 0)
            return tab_ref[jnp.where(hidden, SEL_MASKED, WIN_BASE + delta), h]

        return _add_tiles(s, tile)

    n_tiles = i // parts + 1
    n_pairs = (n_tiles + 1) // 2
    sel = _FlashStream(q_sel, ks_ref, vs_ref, sel_bias, ss_scr, ps_scr)
    win = _FlashStream(q_win, kw_ref, vw_ref, win_bias, sw_scr, pw_scr)
    win_u0 = n_pairs - 2
    carry_s = sel.start(0)
    carry_w = win.start(win_u0)
    carry_s = lax.fori_loop(0, n_pairs - 1, functools.partial(sel.trip, has_next=True), carry_s)
    carry_s = sel.trip(n_pairs - 1, carry_s, has_next=False)
    carry_w = win.trip(win_u0, carry_w, has_next=True)
    carry_w = win.trip(win_u0 + 1, carry_w, has_next=False)
    os_t = sel.finish(n_pairs - 1, carry_s)
    ow_t = win.finish(win_u0 + 1, carry_w)

    oc_t = oct_ref[...]
    gt = gates_ref[...]
    for hg in range(HPG):
        c0 = N_BRANCH * hg
        lanes = slice(hg * TQ, (hg + 1) * TQ)
        out_t = (gt[c0:c0 + 1, :] * oc_t[:, lanes] + gt[c0 + 1:c0 + 2, :] * os_t[:, lanes]
                 + gt[c0 + 2:c0 + 3, :] * ow_t[:, lanes])
        y_ref[:, hg * HEAD_DIM:(hg + 1) * HEAD_DIM] = jnp.transpose(out_t).astype(y_ref.dtype)


def _sel_win_attn(q_t, ks, vs_t, kw, vw_t, unsel, tables, cvec, oc_t, gates_t):
    nb = q_t.shape[0]
    s = q_t.shape[3] // HPG
    ks_spec = pl.BlockSpec((None, None, s, KS_AUG), lambda b, g, i: (b, g, 0, 0))
    kw_spec = pl.BlockSpec((None, None, s, K_AUG), lambda b, g, i: (b, g, 0, 0))
    v_spec = pl.BlockSpec((None, None, V_AUG, s), lambda b, g, i: (b, g, 0, 0))
    qt_spec = pl.BlockSpec((None, None, HEAD_DIM, HPG * TQ), lambda b, g, i: (b, g, 0, i))
    gw = HPG * HEAD_DIM
    return pl.pallas_call(
        _sel_win_kernel,
        out_shape=jax.ShapeDtypeStruct((nb, s, NSA_HEADS * HEAD_DIM), BF16),
        grid=(nb, KV_GROUPS, s // TQ),
        in_specs=[
            qt_spec,
            ks_spec, v_spec, kw_spec, v_spec,
            pl.BlockSpec((None, None, SEL_LANES, TQ), lambda b, g, i: (b, g, 0, i)),
            pl.BlockSpec((CMP_BASE, HPG, LANES, TQ), lambda b, g, i: (0, g, 0, 0)),
            pl.BlockSpec((None, 1, HPG * TQ), lambda b, g, i: (g, 0, 0)),
            qt_spec,
            pl.BlockSpec((None, None, GATE_ROWS, TQ), lambda b, g, i: (b, g, 0, i)),
        ],
        out_specs=pl.BlockSpec((None, TQ, gw), lambda b, g, i: (b, i, g)),
        scratch_shapes=[pltpu.VMEM((2, TK_SEL, HPG * TQ), F32), pltpu.VMEM((2, TK_SEL, HPG * TQ), BF16),
                        pltpu.VMEM((2, TK_SEL, HPG * TQ), F32), pltpu.VMEM((2, TK_SEL, HPG * TQ), BF16)],
        compiler_params=_params(("arbitrary", "arbitrary", "arbitrary")),
        name="sel_win_attn",
    )(q_t, ks, vs_t, kw, vw_t, unsel, tables, cvec, oc_t, gates_t)


def kernel(x, mem, mix_norm, a_w_in, a_conv_w, b_w_in, b_q_gain, kv_norm, w_kv_shared, k_gain,
           cmp_pos, cmp_w1, cmp_w2, rel_bias, mem_norm, mem_w_kv, mem_q_gain, mem_k_gain, w_out,
           mlp_norm, w_up, w_down):
    nb, s, d = x.shape
    depth = mix_norm.shape[0]
    n_a = a_w_in.shape[0]
    assert s % (LANES * CMP_STRIDE) == 0 and s // SEL_BLOCK <= SEL_LANES
    assert WIN_BASE + N_WIN_NEAR <= CMP_BASE

    mk, mv_t = _mem_kv(mem, mem_norm, mem_w_kv, mem_k_gain)
    tables = _bias_tables(rel_bias)
    cvec = jnp.repeat(rel_bias[REL_BUCKETS - 1].astype(F32) * LOG2E, TQ).reshape(KV_GROUPS, 1, HPG * TQ)
    shared = None
    for layer in range(depth):
        if layer < n_a:
            y_tok, y_mem = _mixer_a(x, mix_norm[layer], a_w_in[layer], a_conv_w[layer],
                                    mk[layer], mv_t[layer], mem_q_gain[layer])
        else:
            j = layer - n_a
            ks, vs_t, kw, vw_t, kvc, kvc_t = shared
            q_t, gates_t, y_mem = _b_inproj(x, mix_norm[layer], b_w_in[j], b_q_gain[j],
                                            mk[layer], mv_t[layer], mem_q_gain[layer])
            oc_t, unsel = _cmp_attn(q_t, kvc, kvc_t, tables)
            y_tok = _sel_win_attn(q_t, ks, vs_t, kw, vw_t, unsel, tables, cvec, oc_t, gates_t)
        x = _post(x.reshape(nb * s, d), y_tok.reshape(nb * s, -1), y_mem.reshape(nb * s, -1),
                  w_out[layer], mlp_norm[layer], w_up[layer], w_down[layer]).reshape(nb, s, d)
        if layer == n_a - 1:
            ucmp, ks, vs_t, kw, vw_t = _build_kv(x, kv_norm, w_kv_shared, k_gain)
            kvc, kvc_t = _compress(ucmp, cmp_pos, cmp_w1, cmp_w2, k_gain[0])
            shared = (ks, vs_t, kw, vw_t, kvc, kvc_t)
    return x
```

```python
import functools
import math

import numpy as np
import jax
import jax.numpy as jnp
from jax import lax
from jax.experimental import pallas as pl
from jax.experimental.pallas import tpu as pltpu

F32 = jnp.float32
BF16 = jnp.bfloat16

HEAD_DIM = 64
MEM_HEADS = 4
MEM_WIDTH = MEM_HEADS * HEAD_DIM
CONV_CH = 768
NSA_HEADS = 12
KV_GROUPS = 2
HPG = NSA_HEADS // KV_GROUPS
N_BRANCH = 3
CMP_LEN = 32
CMP_STRIDE = 16
SEL_BLOCK = 64
SEL_TOPK = 16
N_LOCAL_SEL = 2
WINDOW = 512
REL_BUCKETS = 32
REL_MAX_DIST = 1024
EPS = 1e-6
NEG = -1e30
FORCE_SCORE = 1e4
LOG2E = math.log2(math.e)
MASK_BIG = 2.0 ** 100

LANES = 128
SUBLANES = 8
VMEM_LIMIT_BYTES = 56 * 1024 * 1024

TQ = 128
TK_SEL = 256
SEL_LANES = 128
TM = 512

N_SEL_NEAR = 8
SEL_CONST = N_SEL_NEAR
SEL_MASKED = N_SEL_NEAR + 1
WIN_BASE = N_SEL_NEAR + 2
N_WIN_NEAR = WINDOW // LANES + 1
CMP_BASE = 16
N_CMP_NEAR = 23
CMP_CONST = N_CMP_NEAR
CMP_MASKED = N_CMP_NEAR + 1
N_CMP_TILES = N_CMP_NEAR + 2
N_TILES = CMP_BASE + N_CMP_TILES


def _rms(xf, g):
    ms = jnp.mean(xf * xf, axis=-1, keepdims=True)
    return xf * lax.rsqrt(ms + EPS) * g


def _dot(a, b):
    return jnp.dot(a, b, preferred_element_type=F32)


def _dot_nt(a, b):
    return lax.dot_general(a, b, (((1,), (1,)), ((), ())), preferred_element_type=F32)


def _params(sem):
    return pltpu.CompilerParams(dimension_semantics=sem, vmem_limit_bytes=VMEM_LIMIT_BYTES)


def _mem_kv_kernel(mem_ref, mnorm_ref, w_ref, kg_ref, mk_ref, mvt_ref, *, nb, ml):
    mh = _rms(mem_ref[...], mnorm_ref[...]).astype(BF16)
    kv = _dot(mh, w_ref[...])
    for h in range(MEM_HEADS):
        kh = _rms(kv[:, h * HEAD_DIM:(h + 1) * HEAD_DIM], kg_ref[...]).astype(BF16)
        vh = kv[:, MEM_WIDTH + h * HEAD_DIM:MEM_WIDTH + (h + 1) * HEAD_DIM]
        for b in range(nb):
            mk_ref[b, h] = kh[b * ml:(b + 1) * ml]
            mvt_ref[b, h] = jnp.transpose(vh[b * ml:(b + 1) * ml]).astype(BF16)


def _mem_kv(mem, mem_norm, mem_w_kv, mem_k_gain):
    nb, ml, d = mem.shape
    depth = mem_w_kv.shape[0]
    out = jax.ShapeDtypeStruct((depth, nb, MEM_HEADS, ml, HEAD_DIM), BF16)
    out_t = jax.ShapeDtypeStruct((depth, nb, MEM_HEADS, HEAD_DIM, ml), BF16)
    return pl.pallas_call(
        functools.partial(_mem_kv_kernel, nb=nb, ml=ml),
        out_shape=(out, out_t),
        grid=(depth,),
        in_specs=[
            pl.BlockSpec((nb * ml, d), lambda l: (0, 0)),
            pl.BlockSpec((1, d), lambda l: (0, 0)),
            pl.BlockSpec((None, d, 2 * MEM_WIDTH), lambda l: (l, 0, 0)),
            pl.BlockSpec((None, 1, HEAD_DIM), lambda l: (l, 0, 0)),
        ],
        out_specs=(
            pl.BlockSpec((None, nb, MEM_HEADS, ml, HEAD_DIM), lambda l: (l, 0, 0, 0, 0)),
            pl.BlockSpec((None, nb, MEM_HEADS, HEAD_DIM, ml), lambda l: (l, 0, 0, 0, 0)),
        ),
        compiler_params=_params(("arbitrary",)),
        name="mem_kv",
    )(mem.reshape(nb * ml, d), mem_norm.reshape(1, d), mem_w_kv.astype(BF16),
      mem_k_gain.reshape(depth, 1, HEAD_DIM))


def _rms_t(xt, g_col):
    ms = jnp.mean(xt * xt, axis=0, keepdims=True)
    return xt * lax.rsqrt(ms + EPS) * g_col


def _mem_attn(qm, mk_ref, mvt_ref, qg_col, ymem_ref):
    qm_t = jnp.transpose(qm)
    outs = []
    for h in range(MEM_HEADS):
        qh = _rms_t(qm_t[h * HEAD_DIM:(h + 1) * HEAD_DIM, :], qg_col) * (HEAD_DIM ** -0.5)
        lg = _dot(mk_ref[h], qh.astype(BF16))
        m = jnp.max(lg, axis=0, keepdims=True)
        e = jnp.exp(lg - m)
        l = jnp.sum(e, axis=0, keepdims=True)
        outs.append(_dot(mvt_ref[h], e.astype(BF16)) / l)
    ymem_ref[...] = jnp.transpose(jnp.concatenate(outs, axis=0)).astype(ymem_ref.dtype)


def _bucket_np(d):
    n = np.maximum(d, 0)
    max_exact = REL_BUCKETS // 2
    nf = np.maximum(n, 1).astype(np.float64)
    large = max_exact + (np.log(nf / max_exact) / math.log(REL_MAX_DIST / max_exact)
                         * (REL_BUCKETS - max_exact)).astype(np.int32)
    large = np.minimum(large, REL_BUCKETS - 1)
    return np.where(n < max_exact, n, large).astype(np.int32)


def _bucket_tiles():
    r = np.arange(TQ)[:, None]
    k = np.arange(LANES)[None, :]
    tiles = []
    for delta in range(N_SEL_NEAR):
        d = delta * LANES + r - k
        tiles.append(np.where(d >= 0, _bucket_np(d), -1).T)
    tiles.append(np.full((TQ, LANES), REL_BUCKETS - 1))
    tiles.append(np.full((TQ, LANES), -1))
    for delta in range(N_WIN_NEAR):
        d = delta * LANES + r - k
        tiles.append(np.where((d >= 0) & (d < WINDOW), _bucket_np(d), -1).T)
    while len(tiles) < CMP_BASE:
        tiles.append(np.full((TQ, LANES), -1))
    for delta in range(N_CMP_NEAR):
        d = delta * LANES + r - CMP_STRIDE * k - (CMP_LEN - 1)
        tiles.append(np.where(d >= 0, _bucket_np(d), -1).T)
    tiles.append(np.full((TQ, LANES), REL_BUCKETS - 1))
    tiles.append(np.full((TQ, LANES), -1))
    return np.stack(tiles).astype(np.int32)


def _bias_tables_kernel(tab_ref, ids_ref, out_ref):
    relative = pl.program_id(0) < CMP_BASE
    n_bits = REL_BUCKETS.bit_length() - 1
    for h in range(NSA_HEADS):
        sub = jnp.where(relative, tab_ref[REL_BUCKETS - 1, h], 0.0)
        leaves = [(jnp.full((SUBLANES, LANES), tab_ref[b, h], F32) - sub) * LOG2E for b in range(REL_BUCKETS)]
        for r0 in range(0, ids_ref.shape[0], SUBLANES):
            ids = ids_ref[r0:r0 + SUBLANES, :]
            level = leaves
            for k in range(n_bits):
                bit = jnp.bitwise_and(ids, 1 << k) != 0
                level = [jnp.where(bit, level[2 * j + 1], level[2 * j]) for j in range(len(level) // 2)]
            out_ref[h, r0:r0 + SUBLANES, :] = jnp.where(ids < 0, NEG, level[0])


def _bias_tables(rel_bias):
    ids = jnp.asarray(_bucket_tiles())
    return pl.pallas_call(
        _bias_tables_kernel,
        out_shape=jax.ShapeDtypeStruct((N_TILES, NSA_HEADS, TQ, LANES), F32),
        grid=(N_TILES,),
        in_specs=[
            pl.BlockSpec(memory_space=pltpu.SMEM),
            pl.BlockSpec((None, TQ, LANES), lambda t: (t, 0, 0)),
        ],
        out_specs=pl.BlockSpec((None, NSA_HEADS, TQ, LANES), lambda t: (t, 0, 0, 0)),
        compiler_params=_params(("arbitrary",)),
        name="bias_tables",
    )(rel_bias.astype(F32), ids)


def _mixer_a_kernel(x_ref, g_ref, w_ref, cw_ref, mk_ref, mvt_ref, qg_ref,
                    ytok_ref, ymem_ref, carry_ref, *, tm):
    @pl.when(pl.program_id(1) == 0)
    def _():
        carry_ref[...] = jnp.zeros_like(carry_ref)

    h = _rms(x_ref[...], g_ref[...]).astype(BF16)
    z = _dot(h, w_ref[...])
    gate_b = z[:, :CONV_CH]
    v = z[:, CONV_CH:2 * CONV_CH] * z[:, 2 * CONV_CH:3 * CONV_CH]
    prev = carry_ref[...]
    row = lax.broadcasted_iota(jnp.int32, (tm, CONV_CH), 0)
    v1 = jnp.where(row == 0, prev[7:8, :], pltpu.roll(v, 1, 0))
    v2 = jnp.where(row == 0, prev[6:7, :], jnp.where(row == 1, prev[7:8, :], pltpu.roll(v, 2, 0)))
    cw = cw_ref[...]
    y = gate_b * (cw[0:1, :] * v2 + cw[1:2, :] * v1 + cw[2:3, :] * v)
    carry_ref[...] = v[tm - 8:, :]
    ytok_ref[...] = y.astype(ytok_ref.dtype)
    _mem_attn(z[:, 3 * CONV_CH:], mk_ref, mvt_ref, qg_ref[...], ymem_ref)


def _mixer_a(x, norm_g, w_in, conv_w, mk, mv_t, mem_q_gain):
    nb, s, d = x.shape
    ml = mk.shape[2]
    n_in = w_in.shape[1]
    return pl.pallas_call(
        functools.partial(_mixer_a_kernel, tm=TM),
        out_shape=(jax.ShapeDtypeStruct((nb, s, CONV_CH), BF16),
                   jax.ShapeDtypeStruct((nb, s, MEM_WIDTH), BF16)),
        grid=(nb, s // TM),
        in_specs=[
            pl.BlockSpec((None, TM, d), lambda b, i: (b, i, 0)),
            pl.BlockSpec((1, d), lambda b, i: (0, 0)),
            pl.BlockSpec((d, n_in), lambda b, i: (0, 0)),
            pl.BlockSpec((3, CONV_CH), lambda b, i: (0, 0)),
            pl.BlockSpec((None, MEM_HEADS, ml, HEAD_DIM), lambda b, i: (b, 0, 0, 0)),
            pl.BlockSpec((None, MEM_HEADS, HEAD_DIM, ml), lambda b, i: (b, 0, 0, 0)),
            pl.BlockSpec((HEAD_DIM, 1), lambda b, i: (0, 0)),
        ],
        out_specs=(
            pl.BlockSpec((None, TM, CONV_CH), lambda b, i: (b, i, 0)),
            pl.BlockSpec((None, TM, MEM_WIDTH), lambda b, i: (b, i, 0)),
        ),
        scratch_shapes=[pltpu.VMEM((8, CONV_CH), F32)],
        compiler_params=_params(("arbitrary", "arbitrary")),
        name="mixer_a",
    )(x, norm_g.reshape(1, d), w_in.astype(BF16), conv_w, mk, mv_t, mem_q_gain.reshape(HEAD_DIM, 1))


B_GATE0 = NSA_HEADS * HEAD_DIM
B_QMEM0 = B_GATE0 + KV_GROUPS * LANES
B_WIDTH = B_QMEM0 + MEM_WIDTH


GATE_ROWS = 32


def _b_inproj_kernel(x_ref, g_ref, w_ref, qg_ref, mk_ref, mvt_ref, mqg_ref,
                     qt_ref, gates_ref, ymem_ref):
    h = _rms(x_ref[...], g_ref[...]).astype(BF16)
    z = _dot(h, w_ref[...])
    qg = qg_ref[...]
    tm = z.shape[0]
    for pair in range(NSA_HEADS // 2):
        z_t = jnp.transpose(z[:, pair * LANES:(pair + 1) * LANES])
        for half in range(2):
            qh_t = _rms_t(z_t[half * HEAD_DIM:(half + 1) * HEAD_DIM, :], qg) * (HEAD_DIM ** -0.5 * LOG2E)
            qh_t = qh_t.astype(qt_ref.dtype)
            g, hg = divmod(2 * pair + half, HPG)
            for qt in range(tm // TQ):
                lane0 = (qt * HPG + hg) * TQ
                qt_ref[g, :, lane0:lane0 + TQ] = qh_t[:, qt * TQ:(qt + 1) * TQ]
    for g in range(KV_GROUPS):
        gl = z[:, B_GATE0 + g * LANES:B_GATE0 + (g + 1) * LANES]
        gates_ref[g] = jnp.transpose(1.0 / (1.0 + jnp.exp(-gl)))[:GATE_ROWS, :]
    _mem_attn(z[:, B_QMEM0:], mk_ref, mvt_ref, mqg_ref[...], ymem_ref)


def _b_inproj(x, norm_g, w_in, q_gain, mk, mv_t, mem_q_gain):
    nb, s, d = x.shape
    ml = mk.shape[2]
    nq = NSA_HEADS * HEAD_DIM
    ng = HPG * N_BRANCH
    pad = jnp.zeros((d, LANES - ng), w_in.dtype)
    w = jnp.concatenate([w_in[:, :nq], w_in[:, nq:nq + ng], pad,
                         w_in[:, nq + ng:nq + 2 * ng], pad, w_in[:, nq + 2 * ng:]], axis=1)
    return pl.pallas_call(
        _b_inproj_kernel,
        out_shape=(jax.ShapeDtypeStruct((nb, KV_GROUPS, HEAD_DIM, HPG * s), BF16),
                   jax.ShapeDtypeStruct((nb, KV_GROUPS, GATE_ROWS, s), F32),
                   jax.ShapeDtypeStruct((nb, s, MEM_WIDTH), BF16)),
        grid=(nb, s // TM),
        in_specs=[
            pl.BlockSpec((None, TM, d), lambda b, i: (b, i, 0)),
            pl.BlockSpec((1, d), lambda b, i: (0, 0)),
            pl.BlockSpec((d, B_WIDTH), lambda b, i: (0, 0)),
            pl.BlockSpec((HEAD_DIM, 1), lambda b, i: (0, 0)),
            pl.BlockSpec((None, MEM_HEADS, ml, HEAD_DIM), lambda b, i: (b, 0, 0, 0)),
            pl.BlockSpec((None, MEM_HEADS, HEAD_DIM, ml), lambda b, i: (b, 0, 0, 0)),
            pl.BlockSpec((HEAD_DIM, 1), lambda b, i: (0, 0)),
        ],
        out_specs=(
            pl.BlockSpec((None, KV_GROUPS, HEAD_DIM, HPG * TM), lambda b, i: (b, 0, 0, i)),
            pl.BlockSpec((None, KV_GROUPS, GATE_ROWS, TM), lambda b, i: (b, 0, 0, i)),
            pl.BlockSpec((None, TM, MEM_WIDTH), lambda b, i: (b, i, 0)),
        ),
        compiler_params=_params(("arbitrary", "arbitrary")),
        name="b_inproj",
    )(x, norm_g.reshape(1, d), w.astype(BF16), q_gain.reshape(HEAD_DIM, 1), mk, mv_t,
      mem_q_gain.reshape(HEAD_DIM, 1))


FF_CHUNK = 1024
TM_POST = 1024


def _post_kernel(x_ref, ytok_ref, ymem_ref, woa_ref, wob_ref, g_ref, wup_ref, wdn_ref, o_ref, hm_ref):
    @pl.when(pl.program_id(1) == 0)
    def _():
        x1 = x_ref[...] + _dot(ytok_ref[...], woa_ref[...]) + _dot(ymem_ref[...], wob_ref[...])
        o_ref[...] = x1
        hm_ref[...] = _rms(x1, g_ref[...]).astype(hm_ref.dtype)

    a = _dot(hm_ref[...], wup_ref[...].astype(BF16))
    a = jnp.square(jnp.maximum(a, 0.0)).astype(BF16)
    o_ref[...] += _dot(a, wdn_ref[...].astype(BF16))


def _post(x, y_tok, y_mem, w_out, norm_g, w_up, w_down, layer):
    n, d = x.shape
    ff = w_up.shape[2]
    nt = y_tok.shape[1]
    assert nt % MEM_WIDTH == 0
    tm = min(TM_POST, n)
    const = lambda i, c: (0, 0)
    rows = lambda i, c: (i, 0)
    return pl.pallas_call(
        _post_kernel,
        out_shape=jax.ShapeDtypeStruct((n, d), F32),
        grid=(n // tm, ff // FF_CHUNK),
        in_specs=[
            pl.BlockSpec((tm, d), rows),
            pl.BlockSpec((tm, nt), rows),
            pl.BlockSpec((tm, MEM_WIDTH), rows),
            pl.BlockSpec((None, nt, d), lambda i, c: (layer, 0, 0)),
            pl.BlockSpec((None, MEM_WIDTH, d), lambda i, c: (layer, nt // MEM_WIDTH, 0)),
            pl.BlockSpec((1, d), const),
            pl.BlockSpec((None, d, FF_CHUNK), lambda i, c: (layer, 0, c)),
            pl.BlockSpec((None, FF_CHUNK, d), lambda i, c: (layer, c, 0)),
        ],
        out_specs=pl.BlockSpec((tm, d), rows),
        scratch_shapes=[pltpu.VMEM((tm, d), BF16)],
        compiler_params=_params(("arbitrary", "arbitrary")),
        name="post",
    )(x, y_tok, y_mem, w_out, w_out, norm_g.reshape(1, d), w_up, w_down)


K_AUG = 2 * HEAD_DIM
KS_AUG = K_AUG + SEL_LANES
V_AUG = HEAD_DIM + 16


def _build_kv_kernel(x_ref, g_ref, w_ref, kg_ref, ucmp_ref, ks_ref, vs_ref, kw_ref, vw_ref, raw_scr):
    h = _rms(x_ref[...], g_ref[...]).astype(BF16)
    kv = _dot(h, w_ref[...])
    gw = KV_GROUPS * HEAD_DIM
    tm = kv.shape[0]
    for ty in range(2):
        raw_scr[ty] = kv[:, ty * gw:(ty + 1) * gw]
        for t in range(CMP_STRIDE):
            rows = raw_scr[ty, pl.ds(t, tm // CMP_STRIDE, stride=CMP_STRIDE), :]
            for g in range(KV_GROUPS):
                ucmp_ref[ty, g, :, t * HEAD_DIM:(t + 1) * HEAD_DIM] = (
                    rows[:, g * HEAD_DIM:(g + 1) * HEAD_DIM].astype(ucmp_ref.dtype))
    kg = kg_ref[...]
    ones = (lax.broadcasted_iota(jnp.int32, (tm, K_AUG - HEAD_DIM), 1) < 2).astype(ks_ref.dtype)
    pos = pl.program_id(1) * tm + lax.broadcasted_iota(jnp.int32, (tm, SEL_LANES), 0)
    blk = lax.broadcasted_iota(jnp.int32, (tm, SEL_LANES), 1)
    onehot = jnp.where(blk == jnp.right_shift(pos, 6), MASK_BIG, 0.0).astype(ks_ref.dtype)
    ones_row = (lax.broadcasted_iota(jnp.int32, (V_AUG - HEAD_DIM, tm), 0) == 0).astype(vs_ref.dtype)
    for g in range(KV_GROUPS):
        sl = lambda t: kv[:, t * gw + g * HEAD_DIM:t * gw + (g + 1) * HEAD_DIM]
        ks_ref[g, :, :HEAD_DIM] = _rms(sl(2), kg[1:2, :]).astype(ks_ref.dtype)
        ks_ref[g, :, HEAD_DIM:K_AUG] = ones
        ks_ref[g, :, K_AUG:] = onehot
        vs_ref[g, :HEAD_DIM, :] = jnp.transpose(sl(3)).astype(vs_ref.dtype)
        vs_ref[g, HEAD_DIM:, :] = ones_row
        kw_ref[g, :, :HEAD_DIM] = _rms(sl(4), kg[2:3, :]).astype(kw_ref.dtype)
        kw_ref[g, :, HEAD_DIM:] = ones
        vw_ref[g, :HEAD_DIM, :] = jnp.transpose(sl(5)).astype(vw_ref.dtype)
        vw_ref[g, HEAD_DIM:, :] = ones_row


def _build_kv(x, kv_norm, w_kv, k_gain):
    nb, s, d = x.shape
    gw = KV_GROUPS * HEAD_DIM
    ks = jax.ShapeDtypeStruct((nb, KV_GROUPS, s, KS_AUG), BF16)
    kw = jax.ShapeDtypeStruct((nb, KV_GROUPS, s, K_AUG), BF16)
    vt = jax.ShapeDtypeStruct((nb, KV_GROUPS, V_AUG, s), BF16)
    ks_spec = pl.BlockSpec((None, KV_GROUPS, TM, KS_AUG), lambda b, i: (b, 0, i, 0))
    kw_spec = pl.BlockSpec((None, KV_GROUPS, TM, K_AUG), lambda b, i: (b, 0, i, 0))
    v_spec = pl.BlockSpec((None, KV_GROUPS, V_AUG, TM), lambda b, i: (b, 0, 0, i))
    return pl.pallas_call(
        _build_kv_kernel,
        out_shape=(jax.ShapeDtypeStruct((2, nb, KV_GROUPS, s // CMP_STRIDE, CMP_STRIDE * HEAD_DIM), BF16),
                   ks, vt, kw, vt),
        grid=(nb, s // TM),
        in_specs=[
            pl.BlockSpec((None, TM, d), lambda b, i: (b, i, 0)),
            pl.BlockSpec((1, d), lambda b, i: (0, 0)),
            pl.BlockSpec((d, 6 * gw), lambda b, i: (0, 0)),
            pl.BlockSpec((N_BRANCH, HEAD_DIM), lambda b, i: (0, 0)),
        ],
        out_specs=(pl.BlockSpec((2, None, KV_GROUPS, TM // CMP_STRIDE, CMP_STRIDE * HEAD_DIM),
                                lambda b, i: (0, b, 0, i, 0)),
                   ks_spec, v_spec, kw_spec, v_spec),
        scratch_shapes=[pltpu.VMEM((2, TM, gw), F32)],
        compiler_params=_params(("arbitrary", "arbitrary")),
        name="build_kv",
    )(x, kv_norm.reshape(1, d), w_kv.astype(BF16), k_gain)


def _compress_kernel(u_ref, w1_ref, w2_ref, pos_ref, kg_ref, out_ref, out_t_ref, *, nc):
    half = CMP_STRIDE * HEAD_DIM
    u = u_ref[...]
    lo = _dot(u, w1_ref[:half, :])
    hi = _dot(u, w1_ref[half:, :])
    posb = _dot(jnp.broadcast_to(pos_ref[...], (8, 2 * half)).astype(BF16), w1_ref[...])[0:1, :]
    h1 = lo + pltpu.roll(hi, nc - 1, 0) + posb
    hid = 0.5 * h1 * (1.0 + jnp.tanh(math.sqrt(2.0 / math.pi) * (h1 + 0.044715 * (h1 * h1 * h1))))
    c = _dot(hid.astype(BF16), w2_ref[...])
    is_key = pl.program_id(0) == 0
    out = jnp.where(is_key, _rms(c, kg_ref[...]), c)
    out_ref[...] = out.astype(out_ref.dtype)
    out_t_ref[...] = jnp.transpose(out).astype(out_t_ref.dtype)


def _compress(u, cmp_pos, cmp_w1, cmp_w2, k_gain0):
    _, nb, _, nc, width = u.shape
    hidden = cmp_w1.shape[2]
    return pl.pallas_call(
        functools.partial(_compress_kernel, nc=nc),
        out_shape=(jax.ShapeDtypeStruct((2, nb, KV_GROUPS, nc, HEAD_DIM), BF16),
                   jax.ShapeDtypeStruct((2, nb, KV_GROUPS, HEAD_DIM, nc), BF16)),
        grid=(2, nb, KV_GROUPS),
        in_specs=[
            pl.BlockSpec((None, None, None, nc, width), lambda t, b, g: (t, b, g, 0, 0)),
            pl.BlockSpec((None, 2 * width, hidden), lambda t, b, g: (t, 0, 0)),
            pl.BlockSpec((None, hidden, HEAD_DIM), lambda t, b, g: (t, 0, 0)),
            pl.BlockSpec((None, 1, 2 * width), lambda t, b, g: (t, 0, 0)),
            pl.BlockSpec((1, HEAD_DIM), lambda t, b, g: (0, 0)),
        ],
        out_specs=(pl.BlockSpec((None, None, None, nc, HEAD_DIM), lambda t, b, g: (t, b, g, 0, 0)),
                   pl.BlockSpec((None, None, None, HEAD_DIM, nc), lambda t, b, g: (t, b, g, 0, 0))),
        compiler_params=_params(("arbitrary", "arbitrary", "arbitrary")),
        name="compress",
    )(u, cmp_w1.astype(BF16), cmp_w2.astype(BF16), cmp_pos.reshape(2, 1, 2 * width),
      k_gain0.reshape(1, HEAD_DIM))


def _overlap_np(nc, n_sel):
    c0 = np.arange(nc)[:, None] * CMP_STRIDE
    s0 = np.arange(SEL_LANES)[None, :] * SEL_BLOCK
    ov = np.minimum(c0 + CMP_LEN, s0 + SEL_BLOCK) - np.maximum(c0, s0)
    ov = np.clip(ov, 0, None).astype(np.float32) / CMP_LEN
    ov[nc - 1:, :] = 0.0
    ov[:, n_sel:] = 0.0
    return ov


def _select_topk(score, ntop):
    blk = lax.broadcasted_iota(jnp.int32, score.shape, 0).astype(F32)
    for _ in range(ntop):
        m = jnp.max(score, axis=0, keepdims=True)
        first = jnp.min(jnp.where(score == m, blk, float(SEL_LANES)), axis=0, keepdims=True)
        score = jnp.where(blk == first, -jnp.inf, score)
    return score == -jnp.inf


CMP_QSUB = 2


def _cmp_attn_kernel(*refs, nj, n_sel):
    qt_ref, kc_ref, vct_ref, ovt_ref = refs[:4]
    tab_refs = refs[4:4 + nj * CMP_QSUB]
    oct_ref, unsel_ref = refs[4 + nj * CMP_QSUB:]
    i = pl.program_id(2)
    nq = CMP_QSUB * TQ
    s = _dot(kc_ref[...], qt_ref[...])
    s = jnp.concatenate(
        [jnp.concatenate([s[jt * LANES:(jt + 1) * LANES, (qs * HPG + h) * TQ:(qs * HPG + h + 1) * TQ]
                          + tab_refs[qs * nj + jt][h]
                          for qs in range(CMP_QSUB) for h in range(HPG)], axis=1) for jt in range(nj)], axis=0)
    m = jnp.maximum(jnp.max(s, axis=0, keepdims=True), 0.5 * NEG)
    e = jnp.exp2(s - m)
    l = jnp.sum(e, axis=0, keepdims=True)
    p = e * (1.0 / jnp.where(l == 0.0, 1.0, l))
    oct_ref[...] = _dot(vct_ref[...], p.astype(BF16))

    psums = []
    for qs in range(CMP_QSUB):
        acc = p[:, qs * HPG * TQ:(qs * HPG + 1) * TQ]
        for h in range(1, HPG):
            acc = acc + p[:, (qs * HPG + h) * TQ:(qs * HPG + h + 1) * TQ]
        psums.append(acc)
    psum = jnp.concatenate(psums, axis=1)
    p_hi = psum.astype(BF16)
    p_lo = (psum - p_hi.astype(F32)).astype(BF16)
    imp_t = _dot(ovt_ref[...], p_hi) + _dot(ovt_ref[...], p_lo)

    tq = i * nq + lax.broadcasted_iota(jnp.int32, (SEL_LANES, nq), 1)
    j = lax.broadcasted_iota(jnp.int32, (SEL_LANES, nq), 0)
    back = jnp.right_shift(tq, 6) - j
    forced = (j == 0) | ((back >= 0) & (back < N_LOCAL_SEL))
    causal = back >= 0
    n_free = min(SEL_TOPK, n_sel) - (N_LOCAL_SEL + 1)
    picked = _select_topk(jnp.where(causal & ~forced, imp_t, NEG), n_free)
    sel = causal & (forced | picked)
    unsel_ref[...] = jnp.where(sel, 0.0, -1.0).astype(unsel_ref.dtype)


def _cmp_attn(q_t, kvc, kvc_t, tables):
    nb = q_t.shape[0]
    s = q_t.shape[3] // HPG
    nc = s // CMP_STRIDE
    nj = nc // LANES
    n_sel = s // SEL_BLOCK
    ov_t = jnp.asarray(_overlap_np(nc, n_sel).T, dtype=BF16)

    def tab_map(qs, jt):
        def index_map(b, g, i):
            didx = CMP_QSUB * i + qs - (LANES * CMP_STRIDE // TQ) * jt
            row = jnp.where(didx < 0, CMP_MASKED, jnp.minimum(didx, CMP_CONST))
            return (CMP_BASE + row, g, 0, 0)
        return index_map

    nq = CMP_QSUB * TQ
    qt_spec = pl.BlockSpec((None, None, HEAD_DIM, HPG * nq), lambda b, g, i: (b, g, 0, i))
    n_tabs = CMP_QSUB * nj
    return pl.pallas_call(
        functools.partial(_cmp_attn_kernel, nj=nj, n_sel=n_sel),
        out_shape=(jax.ShapeDtypeStruct((nb, KV_GROUPS, HEAD_DIM, HPG * s), F32),
                   jax.ShapeDtypeStruct((nb, KV_GROUPS, SEL_LANES, s), BF16)),
        grid=(nb, KV_GROUPS, s // nq),
        in_specs=[
            qt_spec,
            pl.BlockSpec((None, None, None, nc, HEAD_DIM), lambda b, g, i: (0, b, g, 0, 0)),
            pl.BlockSpec((None, None, None, HEAD_DIM, nc), lambda b, g, i: (1, b, g, 0, 0)),
            pl.BlockSpec((SEL_LANES, nc), lambda b, g, i: (0, 0)),
        ] + [pl.BlockSpec((None, HPG, LANES, TQ), tab_map(qs, jt))
             for qs in range(CMP_QSUB) for jt in range(nj)],
        out_specs=(
            qt_spec,
            pl.BlockSpec((None, None, SEL_LANES, nq), lambda b, g, i: (b, g, 0, i)),
        ),
        compiler_params=_params(("arbitrary", "arbitrary", "arbitrary")),
        name="cmp_attn",
    )(q_t, kvc, kvc_t, ov_t, *([tables] * n_tabs))


class _FlashStream:
    def __init__(self, q_aug, k_ref, vt_ref, bias_fn, s_scr, p_scr):
        self.q_aug, self.k_ref, self.vt_ref, self.bias_fn = q_aug, k_ref, vt_ref, bias_fn
        self.s_scr, self.p_scr = s_scr, p_scr

    def _scores(self, t):
        k0 = pl.multiple_of(jnp.maximum(t, 0) * TK_SEL, TK_SEL)
        return self.bias_fn(t, _dot(self.k_ref[pl.ds(k0, TK_SEL), :], self.q_aug))

    def _pv(self, t, slot):
        k0 = pl.multiple_of(jnp.maximum(t, 0) * TK_SEL, TK_SEL)
        return _dot(self.vt_ref[:, pl.ds(k0, TK_SEL)], self.p_scr[slot])

    def _softmax(self, slot, m):
        s = self.s_scr[slot]
        m_new = jnp.maximum(m, jnp.max(s, axis=0, keepdims=True))
        self.p_scr[slot] = jnp.exp2(s - m_new).astype(self.p_scr.dtype)
        return m_new, jnp.exp2(m - m_new)

    def start(self, u):
        lanes = self.q_aug.shape[1]
        self.s_scr[0] = self._scores(2 * u)
        self.p_scr[1] = jnp.zeros(self.p_scr.shape[1:], self.p_scr.dtype)
        return (jnp.full((1, lanes), NEG, F32), jnp.ones((1, lanes), F32), jnp.zeros((V_AUG, lanes), F32))

    def trip(self, u, carry, has_next):
        m, alpha, acc = carry
        a = 2 * u
        acc = alpha * acc + self._pv(a - 1, 1)
        m, alpha = self._softmax(0, m)
        self.s_scr[1] = self._scores(a + 1)
        acc = alpha * acc + self._pv(a, 0)
        m, alpha = self._softmax(1, m)
        if has_next:
            self.s_scr[0] = self._scores(a + 2)
        return m, alpha, acc

    def finish(self, u_last, carry):
        _, alpha, acc = carry
        acc = alpha * acc + self._pv(2 * u_last + 1, 1)
        return acc[:HEAD_DIM, :] / acc[HEAD_DIM:HEAD_DIM + 1, :]


def _add_tiles(s, tile_fn):
    rows = []
    for part in range(TK_SEL // LANES):
        cols = [s[part * LANES:(part + 1) * LANES, h * TQ:(h + 1) * TQ] + tile_fn(part, h)
                for h in range(HPG)]
        rows.append(jnp.concatenate(cols, axis=1))
    return jnp.concatenate(rows, axis=0)


def _sel_win_kernel(qt_ref, ks_ref, vs_ref, kw_ref, vw_ref, unsel_ref, tab_ref, cvec_ref,
                    oct_ref, gates_ref, y_ref, ss_scr, ps_scr, sw_scr, pw_scr):
    i = pl.program_id(2)
    parts = TK_SEL // LANES

    c = cvec_ref[...]
    c_hi = c.astype(BF16).astype(F32)
    c_lo = c - c_hi
    arow = lax.broadcasted_iota(jnp.int32, (K_AUG - HEAD_DIM, HPG * TQ), 0)
    aug = jnp.where(arow == 0, c_hi, jnp.where(arow == 1, c_lo, 0.0)).astype(BF16)
    q_win = jnp.concatenate([qt_ref[...], aug], axis=0)
    unsel_t = unsel_ref[...]
    q_sel = jnp.concatenate([q_win, jnp.concatenate([unsel_t] * HPG, axis=1)], axis=0)

    def sel_bias(t, s):
        def tile(part, h):
            delta = i - parts * t - part
            row = jnp.where(delta < 0, SEL_MASKED, jnp.minimum(delta, SEL_CONST))
            return tab_ref[row, h]

        return _add_tiles(s, tile)

    def win_bias(t, s):
        def tile(part, h):
            delta = i - parts * t - part
            hidden = (delta < 0) | (delta >= N_WIN_NEAR) | (t < 0)
            return tab_ref[jnp.where(hidden, SEL_MASKED, WIN_BASE + delta), h]

        return _add_tiles(s, tile)

    n_tiles = i // parts + 1
    n_pairs = (n_tiles + 1) // 2
    sel = _FlashStream(q_sel, ks_ref, vs_ref, sel_bias, ss_scr, ps_scr)
    win = _FlashStream(q_win, kw_ref, vw_ref, win_bias, sw_scr, pw_scr)
    win_u0 = n_pairs - 2
    carry_s = sel.start(0)
    carry_w = win.start(win_u0)
    carry_s = lax.fori_loop(0, n_pairs - 1, functools.partial(sel.trip, has_next=True), carry_s)
    carry_s = sel.trip(n_pairs - 1, carry_s, has_next=False)
    carry_w = win.trip(win_u0, carry_w, has_next=True)
    carry_w = win.trip(win_u0 + 1, carry_w, has_next=False)
    os_t = sel.finish(n_pairs - 1, carry_s)
    ow_t = win.finish(win_u0 + 1, carry_w)

    oc_t = oct_ref[...]
    gt = gates_ref[...]
    for hg in range(HPG):
        c0 = N_BRANCH * hg
        lanes = slice(hg * TQ, (hg + 1) * TQ)
        out_t = (gt[c0:c0 + 1, :] * oc_t[:, lanes] + gt[c0 + 1:c0 + 2, :] * os_t[:, lanes]
                 + gt[c0 + 2:c0 + 3, :] * ow_t[:, lanes])
        y_ref[:, hg * HEAD_DIM:(hg + 1) * HEAD_DIM] = jnp.transpose(out_t).astype(y_ref.dtype)


def _sel_win_attn(q_t, ks, vs_t, kw, vw_t, unsel, tables, cvec, oc_t, gates_t):
    nb = q_t.shape[0]
    s = q_t.shape[3] // HPG
    ks_spec = pl.BlockSpec((None, None, s, KS_AUG), lambda b, g, i: (b, g, 0, 0))
    kw_spec = pl.BlockSpec((None, None, s, K_AUG), lambda b, g, i: (b, g, 0, 0))
    v_spec = pl.BlockSpec((None, None, V_AUG, s), lambda b, g, i: (b, g, 0, 0))
    qt_spec = pl.BlockSpec((None, None, HEAD_DIM, HPG * TQ), lambda b, g, i: (b, g, 0, i))
    gw = HPG * HEAD_DIM
    return pl.pallas_call(
        _sel_win_kernel,
        out_shape=jax.ShapeDtypeStruct((nb, s, NSA_HEADS * HEAD_DIM), BF16),
        grid=(nb, KV_GROUPS, s // TQ),
        in_specs=[
            qt_spec,
            ks_spec, v_spec, kw_spec, v_spec,
            pl.BlockSpec((None, None, SEL_LANES, TQ), lambda b, g, i: (b, g, 0, i)),
            pl.BlockSpec((CMP_BASE, HPG, LANES, TQ), lambda b, g, i: (0, g, 0, 0)),
            pl.BlockSpec((None, 1, HPG * TQ), lambda b, g, i: (g, 0, 0)),
            qt_spec,
            pl.BlockSpec((None, None, GATE_ROWS, TQ), lambda b, g, i: (b, g, 0, i)),
        ],
        out_specs=pl.BlockSpec((None, TQ, gw), lambda b, g, i: (b, i, g)),
        scratch_shapes=[pltpu.VMEM((2, TK_SEL, HPG * TQ), F32), pltpu.VMEM((2, TK_SEL, HPG * TQ), BF16),
                        pltpu.VMEM((2, TK_SEL, HPG * TQ), F32), pltpu.VMEM((2, TK_SEL, HPG * TQ), BF16)],
        compiler_params=_params(("arbitrary", "arbitrary", "arbitrary")),
        name="sel_win_attn",
    )(q_t, ks, vs_t, kw, vw_t, unsel, tables, cvec, oc_t, gates_t)


def kernel(x, mem, mix_norm, a_w_in, a_conv_w, b_w_in, b_q_gain, kv_norm, w_kv_shared, k_gain,
           cmp_pos, cmp_w1, cmp_w2, rel_bias, mem_norm, mem_w_kv, mem_q_gain, mem_k_gain, w_out,
           mlp_norm, w_up, w_down):
    nb, s, d = x.shape
    depth = mix_norm.shape[0]
    n_a = a_w_in.shape[0]
    assert s % (LANES * CMP_STRIDE) == 0 and s // SEL_BLOCK <= SEL_LANES
    assert WIN_BASE + N_WIN_NEAR <= CMP_BASE

    mk, mv_t = _mem_kv(mem, mem_norm, mem_w_kv, mem_k_gain)
    tables = _bias_tables(rel_bias)
    cvec = jnp.repeat(rel_bias[REL_BUCKETS - 1].astype(F32) * LOG2E, TQ).reshape(KV_GROUPS, 1, HPG * TQ)
    w_out_bf = w_out.astype(BF16)
    shared = None
    for layer in range(depth):
        if layer < n_a:
            y_tok, y_mem = _mixer_a(x, mix_norm[layer], a_w_in[layer], a_conv_w[layer],
                                    mk[layer], mv_t[layer], mem_q_gain[layer])
        else:
            j = layer - n_a
            ks, vs_t, kw, vw_t, kvc, kvc_t = shared
            q_t, gates_t, y_mem = _b_inproj(x, mix_norm[layer], b_w_in[j], b_q_gain[j],
                                            mk[layer], mv_t[layer], mem_q_gain[layer])
            oc_t, unsel = _cmp_attn(q_t, kvc, kvc_t, tables)
            y_tok = _sel_win_attn(q_t, ks, vs_t, kw, vw_t, unsel, tables, cvec, oc_t, gates_t)
        x = _post(x.reshape(nb * s, d), y_tok.reshape(nb * s, -1), y_mem.reshape(nb * s, -1),
                  w_out_bf, mlp_norm[layer], w_up, w_down, layer).reshape(nb, s, d)
        if layer == n_a - 1:
            ucmp, ks, vs_t, kw, vw_t = _build_kv(x, kv_norm, w_kv_shared, k_gain)
            kvc, kvc_t = _compress(ucmp, cmp_pos, cmp_w1, cmp_w2, k_gain[0])
            shared = (ks, vs_t, kw, vw_t, kvc, kvc_t)
    return x
```

```python
import functools
import math

import numpy as np
import jax
import jax.numpy as jnp
from jax import lax
from jax.experimental import pallas as pl
from jax.experimental.pallas import tpu as pltpu

F32 = jnp.float32
BF16 = jnp.bfloat16

HEAD_DIM = 64
MEM_HEADS = 4
MEM_WIDTH = MEM_HEADS * HEAD_DIM
CONV_CH = 768
NSA_HEADS = 12
KV_GROUPS = 2
HPG = NSA_HEADS // KV_GROUPS
N_BRANCH = 3
CMP_LEN = 32
CMP_STRIDE = 16
SEL_BLOCK = 64
SEL_TOPK = 16
N_LOCAL_SEL = 2
WINDOW = 512
REL_BUCKETS = 32
REL_MAX_DIST = 1024
EPS = 1e-6
NEG = -1e30
FORCE_SCORE = 1e4
LOG2E = math.log2(math.e)
MASK_BIG = 2.0 ** 100

LANES = 128
SUBLANES = 8
VMEM_LIMIT_BYTES = 56 * 1024 * 1024

TQ = 128
TK_SEL = 256
SEL_LANES = 128
TM = 512
WIN_TILES = max(((TQ * i - (WINDOW - 1)) % TK_SEL + WINDOW + TQ - 2) // TK_SEL + 1 for i in range(TK_SEL))

N_SEL_NEAR = 8
SEL_CONST = N_SEL_NEAR
SEL_MASKED = N_SEL_NEAR + 1
WIN_BASE = N_SEL_NEAR + 2
N_WIN_NEAR = WINDOW // LANES + 1
CMP_BASE = 16
N_CMP_NEAR = 23
CMP_CONST = N_CMP_NEAR
CMP_MASKED = N_CMP_NEAR + 1
N_CMP_TILES = N_CMP_NEAR + 2
N_TILES = CMP_BASE + N_CMP_TILES


def _rms(xf, g):
    ms = jnp.mean(xf * xf, axis=-1, keepdims=True)
    return xf * lax.rsqrt(ms + EPS) * g


def _dot(a, b):
    return jnp.dot(a, b, preferred_element_type=F32)


def _dot_nt(a, b):
    return lax.dot_general(a, b, (((1,), (1,)), ((), ())), preferred_element_type=F32)


def _params(sem):
    return pltpu.CompilerParams(dimension_semantics=sem, vmem_limit_bytes=VMEM_LIMIT_BYTES)


def _mem_kv_kernel(mem_ref, mnorm_ref, w_ref, kg_ref, mk_ref, mvt_ref, *, nb, ml):
    mh = _rms(mem_ref[...], mnorm_ref[...]).astype(BF16)
    kv = _dot(mh, w_ref[...])
    for h in range(MEM_HEADS):
        kh = _rms(kv[:, h * HEAD_DIM:(h + 1) * HEAD_DIM], kg_ref[...]).astype(BF16)
        vh = kv[:, MEM_WIDTH + h * HEAD_DIM:MEM_WIDTH + (h + 1) * HEAD_DIM]
        for b in range(nb):
            mk_ref[b, h] = kh[b * ml:(b + 1) * ml]
            mvt_ref[b, h] = jnp.transpose(vh[b * ml:(b + 1) * ml]).astype(BF16)


def _mem_kv(mem, mem_norm, mem_w_kv, mem_k_gain):
    nb, ml, d = mem.shape
    depth = mem_w_kv.shape[0]
    out = jax.ShapeDtypeStruct((depth, nb, MEM_HEADS, ml, HEAD_DIM), BF16)
    out_t = jax.ShapeDtypeStruct((depth, nb, MEM_HEADS, HEAD_DIM, ml), BF16)
    return pl.pallas_call(
        functools.partial(_mem_kv_kernel, nb=nb, ml=ml),
        out_shape=(out, out_t),
        grid=(depth,),
        in_specs=[
            pl.BlockSpec((nb * ml, d), lambda l: (0, 0)),
            pl.BlockSpec((1, d), lambda l: (0, 0)),
            pl.BlockSpec((None, d, 2 * MEM_WIDTH), lambda l: (l, 0, 0)),
            pl.BlockSpec((None, 1, HEAD_DIM), lambda l: (l, 0, 0)),
        ],
        out_specs=(
            pl.BlockSpec((None, nb, MEM_HEADS, ml, HEAD_DIM), lambda l: (l, 0, 0, 0, 0)),
            pl.BlockSpec((None, nb, MEM_HEADS, HEAD_DIM, ml), lambda l: (l, 0, 0, 0, 0)),
        ),
        compiler_params=_params(("arbitrary",)),
        name="mem_kv",
    )(mem.reshape(nb * ml, d), mem_norm.reshape(1, d), mem_w_kv.astype(BF16),
      mem_k_gain.reshape(depth, 1, HEAD_DIM))


def _rms_t(xt, g_col):
    ms = jnp.mean(xt * xt, axis=0, keepdims=True)
    return xt * lax.rsqrt(ms + EPS) * g_col


def _mem_attn(qm, mk_ref, mvt_ref, qg_col, ymem_ref):
    qm_t = jnp.transpose(qm)
    outs = []
    for h in range(MEM_HEADS):
        qh = _rms_t(qm_t[h * HEAD_DIM:(h + 1) * HEAD_DIM, :], qg_col) * (HEAD_DIM ** -0.5)
        lg = _dot(mk_ref[h], qh.astype(BF16))
        m = jnp.max(lg, axis=0, keepdims=True)
        e = jnp.exp(lg - m)
        l = jnp.sum(e, axis=0, keepdims=True)
        outs.append(_dot(mvt_ref[h], e.astype(BF16)) / l)
    ymem_ref[...] = jnp.transpose(jnp.concatenate(outs, axis=0)).astype(ymem_ref.dtype)


def _bucket_np(d):
    n = np.maximum(d, 0)
    max_exact = REL_BUCKETS // 2
    nf = np.maximum(n, 1).astype(np.float64)
    large = max_exact + (np.log(nf / max_exact) / math.log(REL_MAX_DIST / max_exact)
                         * (REL_BUCKETS - max_exact)).astype(np.int32)
    large = np.minimum(large, REL_BUCKETS - 1)
    return np.where(n < max_exact, n, large).astype(np.int32)


def _bucket_tiles():
    r = np.arange(TQ)[:, None]
    k = np.arange(LANES)[None, :]
    tiles = []
    for delta in range(N_SEL_NEAR):
        d = delta * LANES + r - k
        tiles.append(np.where(d >= 0, _bucket_np(d), -1).T)
    tiles.append(np.full((TQ, LANES), REL_BUCKETS - 1))
    tiles.append(np.full((TQ, LANES), -1))
    for delta in range(N_WIN_NEAR):
        d = delta * LANES + r - k
        tiles.append(np.where((d >= 0) & (d < WINDOW), _bucket_np(d), -1).T)
    while len(tiles) < CMP_BASE:
        tiles.append(np.full((TQ, LANES), -1))
    for delta in range(N_CMP_NEAR):
        d = delta * LANES + r - CMP_STRIDE * k - (CMP_LEN - 1)
        tiles.append(np.where(d >= 0, _bucket_np(d), -1).T)
    tiles.append(np.full((TQ, LANES), REL_BUCKETS - 1))
    tiles.append(np.full((TQ, LANES), -1))
    return np.stack(tiles).astype(np.int32)


def _bias_tables_kernel(tab_ref, ids_ref, out_ref):
    relative = pl.program_id(0) < CMP_BASE
    n_bits = REL_BUCKETS.bit_length() - 1
    for h in range(NSA_HEADS):
        sub = jnp.where(relative, tab_ref[REL_BUCKETS - 1, h], 0.0)
        leaves = [(jnp.full((SUBLANES, LANES), tab_ref[b, h], F32) - sub) * LOG2E for b in range(REL_BUCKETS)]
        for r0 in range(0, ids_ref.shape[0], SUBLANES):
            ids = ids_ref[r0:r0 + SUBLANES, :]
            level = leaves
            for k in range(n_bits):
                bit = jnp.bitwise_and(ids, 1 << k) != 0
                level = [jnp.where(bit, level[2 * j + 1], level[2 * j]) for j in range(len(level) // 2)]
            out_ref[h, r0:r0 + SUBLANES, :] = jnp.where(ids < 0, NEG, level[0])


def _bias_tables(rel_bias):
    ids = jnp.asarray(_bucket_tiles())
    return pl.pallas_call(
        _bias_tables_kernel,
        out_shape=jax.ShapeDtypeStruct((N_TILES, NSA_HEADS, TQ, LANES), F32),
        grid=(N_TILES,),
        in_specs=[
            pl.BlockSpec(memory_space=pltpu.SMEM),
            pl.BlockSpec((None, TQ, LANES), lambda t: (t, 0, 0)),
        ],
        out_specs=pl.BlockSpec((None, NSA_HEADS, TQ, LANES), lambda t: (t, 0, 0, 0)),
        compiler_params=_params(("arbitrary",)),
        name="bias_tables",
    )(rel_bias.astype(F32), ids)


def _mixer_a_kernel(x_ref, g_ref, w_ref, cw_ref, mk_ref, mvt_ref, qg_ref,
                    ytok_ref, ymem_ref, carry_ref, *, tm):
    @pl.when(pl.program_id(1) == 0)
    def _():
        carry_ref[...] = jnp.zeros_like(carry_ref)

    h = _rms(x_ref[...], g_ref[...]).astype(BF16)
    z = _dot(h, w_ref[...])
    gate_b = z[:, :CONV_CH]
    v = z[:, CONV_CH:2 * CONV_CH] * z[:, 2 * CONV_CH:3 * CONV_CH]
    prev = carry_ref[...]
    row = lax.broadcasted_iota(jnp.int32, (tm, CONV_CH), 0)
    v1 = jnp.where(row == 0, prev[7:8, :], pltpu.roll(v, 1, 0))
    v2 = jnp.where(row == 0, prev[6:7, :], jnp.where(row == 1, prev[7:8, :], pltpu.roll(v, 2, 0)))
    cw = cw_ref[...]
    y = gate_b * (cw[0:1, :] * v2 + cw[1:2, :] * v1 + cw[2:3, :] * v)
    carry_ref[...] = v[tm - 8:, :]
    ytok_ref[...] = y.astype(ytok_ref.dtype)
    _mem_attn(z[:, 3 * CONV_CH:], mk_ref, mvt_ref, qg_ref[...], ymem_ref)


def _mixer_a(x, norm_g, w_in, conv_w, mk, mv_t, mem_q_gain):
    nb, s, d = x.shape
    ml = mk.shape[2]
    n_in = w_in.shape[1]
    return pl.pallas_call(
        functools.partial(_mixer_a_kernel, tm=TM),
        out_shape=(jax.ShapeDtypeStruct((nb, s, CONV_CH), BF16),
                   jax.ShapeDtypeStruct((nb, s, MEM_WIDTH), BF16)),
        grid=(nb, s // TM),
        in_specs=[
            pl.BlockSpec((None, TM, d), lambda b, i: (b, i, 0)),
            pl.BlockSpec((1, d), lambda b, i: (0, 0)),
            pl.BlockSpec((d, n_in), lambda b, i: (0, 0)),
            pl.BlockSpec((3, CONV_CH), lambda b, i: (0, 0)),
            pl.BlockSpec((None, MEM_HEADS, ml, HEAD_DIM), lambda b, i: (b, 0, 0, 0)),
            pl.BlockSpec((None, MEM_HEADS, HEAD_DIM, ml), lambda b, i: (b, 0, 0, 0)),
            pl.BlockSpec((HEAD_DIM, 1), lambda b, i: (0, 0)),
        ],
        out_specs=(
            pl.BlockSpec((None, TM, CONV_CH), lambda b, i: (b, i, 0)),
            pl.BlockSpec((None, TM, MEM_WIDTH), lambda b, i: (b, i, 0)),
        ),
        scratch_shapes=[pltpu.VMEM((8, CONV_CH), F32)],
        compiler_params=_params(("arbitrary", "arbitrary")),
        name="mixer_a",
    )(x, norm_g.reshape(1, d), w_in.astype(BF16), conv_w, mk, mv_t, mem_q_gain.reshape(HEAD_DIM, 1))


B_GATE0 = NSA_HEADS * HEAD_DIM
B_QMEM0 = B_GATE0 + KV_GROUPS * LANES
B_WIDTH = B_QMEM0 + MEM_WIDTH


GATE_ROWS = 32


def _b_inproj_kernel(x_ref, g_ref, w_ref, qg_ref, mk_ref, mvt_ref, mqg_ref,
                     qt_ref, gates_ref, ymem_ref):
    h = _rms(x_ref[...], g_ref[...]).astype(BF16)
    z = _dot(h, w_ref[...])
    qg = qg_ref[...]
    tm = z.shape[0]
    for pair in range(NSA_HEADS // 2):
        z_t = jnp.transpose(z[:, pair * LANES:(pair + 1) * LANES])
        for half in range(2):
            qh_t = _rms_t(z_t[half * HEAD_DIM:(half + 1) * HEAD_DIM, :], qg) * (HEAD_DIM ** -0.5 * LOG2E)
            qh_t = qh_t.astype(qt_ref.dtype)
            g, hg = divmod(2 * pair + half, HPG)
            for qt in range(tm // TQ):
                lane0 = (qt * HPG + hg) * TQ
                qt_ref[g, :, lane0:lane0 + TQ] = qh_t[:, qt * TQ:(qt + 1) * TQ]
    for g in range(KV_GROUPS):
        gl = z[:, B_GATE0 + g * LANES:B_GATE0 + (g + 1) * LANES]
        gates_ref[g] = jnp.transpose(1.0 / (1.0 + jnp.exp(-gl)))[:GATE_ROWS, :]
    _mem_attn(z[:, B_QMEM0:], mk_ref, mvt_ref, mqg_ref[...], ymem_ref)


def _b_inproj(x, norm_g, w_in, q_gain, mk, mv_t, mem_q_gain):
    nb, s, d = x.shape
    ml = mk.shape[2]
    nq = NSA_HEADS * HEAD_DIM
    ng = HPG * N_BRANCH
    pad = jnp.zeros((d, LANES - ng), w_in.dtype)
    w = jnp.concatenate([w_in[:, :nq], w_in[:, nq:nq + ng], pad,
                         w_in[:, nq + ng:nq + 2 * ng], pad, w_in[:, nq + 2 * ng:]], axis=1)
    return pl.pallas_call(
        _b_inproj_kernel,
        out_shape=(jax.ShapeDtypeStruct((nb, KV_GROUPS, HEAD_DIM, HPG * s), BF16),
                   jax.ShapeDtypeStruct((nb, KV_GROUPS, GATE_ROWS, s), F32),
                   jax.ShapeDtypeStruct((nb, s, MEM_WIDTH), BF16)),
        grid=(nb, s // TM),
        in_specs=[
            pl.BlockSpec((None, TM, d), lambda b, i: (b, i, 0)),
            pl.BlockSpec((1, d), lambda b, i: (0, 0)),
            pl.BlockSpec((d, B_WIDTH), lambda b, i: (0, 0)),
            pl.BlockSpec((HEAD_DIM, 1), lambda b, i: (0, 0)),
            pl.BlockSpec((None, MEM_HEADS, ml, HEAD_DIM), lambda b, i: (b, 0, 0, 0)),
            pl.BlockSpec((None, MEM_HEADS, HEAD_DIM, ml), lambda b, i: (b, 0, 0, 0)),
            pl.BlockSpec((HEAD_DIM, 1), lambda b, i: (0, 0)),
        ],
        out_specs=(
            pl.BlockSpec((None, KV_GROUPS, HEAD_DIM, HPG * TM), lambda b, i: (b, 0, 0, i)),
            pl.BlockSpec((None, KV_GROUPS, GATE_ROWS, TM), lambda b, i: (b, 0, 0, i)),
            pl.BlockSpec((None, TM, MEM_WIDTH), lambda b, i: (b, i, 0)),
        ),
        compiler_params=_params(("arbitrary", "arbitrary")),
        name="b_inproj",
    )(x, norm_g.reshape(1, d), w.astype(BF16), q_gain.reshape(HEAD_DIM, 1), mk, mv_t,
      mem_q_gain.reshape(HEAD_DIM, 1))


FF_CHUNK = 1024
TM_POST = 1024


def _post_kernel(x_ref, ytok_ref, ymem_ref, woa_ref, wob_ref, g_ref, wup_ref, wdn_ref, o_ref, hm_ref):
    @pl.when(pl.program_id(1) == 0)
    def _():
        x1 = x_ref[...] + _dot(ytok_ref[...], woa_ref[...]) + _dot(ymem_ref[...], wob_ref[...])
        o_ref[...] = x1
        hm_ref[...] = _rms(x1, g_ref[...]).astype(hm_ref.dtype)

    a = _dot(hm_ref[...], wup_ref[...].astype(BF16))
    a = jnp.square(jnp.maximum(a, 0.0)).astype(BF16)
    o_ref[...] += _dot(a, wdn_ref[...].astype(BF16))


def _post(x, y_tok, y_mem, w_out, norm_g, w_up, w_down, layer):
    n, d = x.shape
    ff = w_up.shape[2]
    nt = y_tok.shape[1]
    assert nt % MEM_WIDTH == 0
    tm = min(TM_POST, n)
    const = lambda i, c: (0, 0)
    rows = lambda i, c: (i, 0)
    return pl.pallas_call(
        _post_kernel,
        out_shape=jax.ShapeDtypeStruct((n, d), F32),
        grid=(n // tm, ff // FF_CHUNK),
        in_specs=[
            pl.BlockSpec((tm, d), rows),
            pl.BlockSpec((tm, nt), rows),
            pl.BlockSpec((tm, MEM_WIDTH), rows),
            pl.BlockSpec((None, nt, d), lambda i, c: (layer, 0, 0)),
            pl.BlockSpec((None, MEM_WIDTH, d), lambda i, c: (layer, nt // MEM_WIDTH, 0)),
            pl.BlockSpec((1, d), const),
            pl.BlockSpec((None, d, FF_CHUNK), lambda i, c: (layer, 0, c)),
            pl.BlockSpec((None, FF_CHUNK, d), lambda i, c: (layer, c, 0)),
        ],
        out_specs=pl.BlockSpec((tm, d), rows),
        scratch_shapes=[pltpu.VMEM((tm, d), BF16)],
        compiler_params=_params(("arbitrary", "arbitrary")),
        name="post",
    )(x, y_tok, y_mem, w_out, w_out, norm_g.reshape(1, d), w_up, w_down)


K_AUG = 2 * HEAD_DIM
KS_AUG = K_AUG + SEL_LANES
V_AUG = HEAD_DIM + 16


def _build_kv_kernel(x_ref, g_ref, w_ref, kg_ref, ucmp_ref, ks_ref, vs_ref, kw_ref, vw_ref, raw_scr):
    h = _rms(x_ref[...], g_ref[...]).astype(BF16)
    kv = _dot(h, w_ref[...])
    gw = KV_GROUPS * HEAD_DIM
    tm = kv.shape[0]
    for ty in range(2):
        raw_scr[ty] = kv[:, ty * gw:(ty + 1) * gw]
        for t in range(CMP_STRIDE):
            rows = raw_scr[ty, pl.ds(t, tm // CMP_STRIDE, stride=CMP_STRIDE), :]
            for g in range(KV_GROUPS):
                ucmp_ref[ty, g, :, t * HEAD_DIM:(t + 1) * HEAD_DIM] = (
                    rows[:, g * HEAD_DIM:(g + 1) * HEAD_DIM].astype(ucmp_ref.dtype))
    kg = kg_ref[...]
    ones = (lax.broadcasted_iota(jnp.int32, (tm, K_AUG - HEAD_DIM), 1) < 2).astype(ks_ref.dtype)
    pos = pl.program_id(1) * tm + lax.broadcasted_iota(jnp.int32, (tm, SEL_LANES), 0)
    blk = lax.broadcasted_iota(jnp.int32, (tm, SEL_LANES), 1)
    onehot = jnp.where(blk == jnp.right_shift(pos, 6), MASK_BIG, 0.0).astype(ks_ref.dtype)
    ones_row = (lax.broadcasted_iota(jnp.int32, (V_AUG - HEAD_DIM, tm), 0) == 0).astype(vs_ref.dtype)
    for g in range(KV_GROUPS):
        sl = lambda t: kv[:, t * gw + g * HEAD_DIM:t * gw + (g + 1) * HEAD_DIM]
        ks_ref[g, :, :HEAD_DIM] = _rms(sl(2), kg[1:2, :]).astype(ks_ref.dtype)
        ks_ref[g, :, HEAD_DIM:K_AUG] = ones
        ks_ref[g, :, K_AUG:] = onehot
        vs_ref[g, :HEAD_DIM, :] = jnp.transpose(sl(3)).astype(vs_ref.dtype)
        vs_ref[g, HEAD_DIM:, :] = ones_row
        kw_ref[g, :, :HEAD_DIM] = _rms(sl(4), kg[2:3, :]).astype(kw_ref.dtype)
        kw_ref[g, :, HEAD_DIM:] = ones
        vw_ref[g, :HEAD_DIM, :] = jnp.transpose(sl(5)).astype(vw_ref.dtype)
        vw_ref[g, HEAD_DIM:, :] = ones_row


def _build_kv(x, kv_norm, w_kv, k_gain):
    nb, s, d = x.shape
    gw = KV_GROUPS * HEAD_DIM
    ks = jax.ShapeDtypeStruct((nb, KV_GROUPS, s, KS_AUG), BF16)
    kw = jax.ShapeDtypeStruct((nb, KV_GROUPS, s, K_AUG), BF16)
    vt = jax.ShapeDtypeStruct((nb, KV_GROUPS, V_AUG, s), BF16)
    ks_spec = pl.BlockSpec((None, KV_GROUPS, TM, KS_AUG), lambda b, i: (b, 0, i, 0))
    kw_spec = pl.BlockSpec((None, KV_GROUPS, TM, K_AUG), lambda b, i: (b, 0, i, 0))
    v_spec = pl.BlockSpec((None, KV_GROUPS, V_AUG, TM), lambda b, i: (b, 0, 0, i))
    return pl.pallas_call(
        _build_kv_kernel,
        out_shape=(jax.ShapeDtypeStruct((2, nb, KV_GROUPS, s // CMP_STRIDE, CMP_STRIDE * HEAD_DIM), BF16),
                   ks, vt, kw, vt),
        grid=(nb, s // TM),
        in_specs=[
            pl.BlockSpec((None, TM, d), lambda b, i: (b, i, 0)),
            pl.BlockSpec((1, d), lambda b, i: (0, 0)),
            pl.BlockSpec((d, 6 * gw), lambda b, i: (0, 0)),
            pl.BlockSpec((N_BRANCH, HEAD_DIM), lambda b, i: (0, 0)),
        ],
        out_specs=(pl.BlockSpec((2, None, KV_GROUPS, TM // CMP_STRIDE, CMP_STRIDE * HEAD_DIM),
                                lambda b, i: (0, b, 0, i, 0)),
                   ks_spec, v_spec, kw_spec, v_spec),
        scratch_shapes=[pltpu.VMEM((2, TM, gw), F32)],
        compiler_params=_params(("arbitrary", "arbitrary")),
        name="build_kv",
    )(x, kv_norm.reshape(1, d), w_kv.astype(BF16), k_gain)


def _compress_kernel(u_ref, w1_ref, w2_ref, pos_ref, kg_ref, out_ref, out_t_ref, *, nc):
    half = CMP_STRIDE * HEAD_DIM
    u = u_ref[...]
    lo = _dot(u, w1_ref[:half, :])
    hi = _dot(u, w1_ref[half:, :])
    posb = _dot(jnp.broadcast_to(pos_ref[...], (8, 2 * half)).astype(BF16), w1_ref[...])[0:1, :]
    h1 = lo + pltpu.roll(hi, nc - 1, 0) + posb
    hid = 0.5 * h1 * (1.0 + jnp.tanh(math.sqrt(2.0 / math.pi) * (h1 + 0.044715 * (h1 * h1 * h1))))
    c = _dot(hid.astype(BF16), w2_ref[...])
    is_key = pl.program_id(0) == 0
    out = jnp.where(is_key, _rms(c, kg_ref[...]), c)
    out_ref[...] = out.astype(out_ref.dtype)
    out_t_ref[...] = jnp.transpose(out).astype(out_t_ref.dtype)


def _compress(u, cmp_pos, cmp_w1, cmp_w2, k_gain0):
    _, nb, _, nc, width = u.shape
    hidden = cmp_w1.shape[2]
    return pl.pallas_call(
        functools.partial(_compress_kernel, nc=nc),
        out_shape=(jax.ShapeDtypeStruct((2, nb, KV_GROUPS, nc, HEAD_DIM), BF16),
                   jax.ShapeDtypeStruct((2, nb, KV_GROUPS, HEAD_DIM, nc), BF16)),
        grid=(2, nb, KV_GROUPS),
        in_specs=[
            pl.BlockSpec((None, None, None, nc, width), lambda t, b, g: (t, b, g, 0, 0)),
            pl.BlockSpec((None, 2 * width, hidden), lambda t, b, g: (t, 0, 0)),
            pl.BlockSpec((None, hidden, HEAD_DIM), lambda t, b, g: (t, 0, 0)),
            pl.BlockSpec((None, 1, 2 * width), lambda t, b, g: (t, 0, 0)),
            pl.BlockSpec((1, HEAD_DIM), lambda t, b, g: (0, 0)),
        ],
        out_specs=(pl.BlockSpec((None, None, None, nc, HEAD_DIM), lambda t, b, g: (t, b, g, 0, 0)),
                   pl.BlockSpec((None, None, None, HEAD_DIM, nc), lambda t, b, g: (t, b, g, 0, 0))),
        compiler_params=_params(("arbitrary", "arbitrary", "arbitrary")),
        name="compress",
    )(u, cmp_w1.astype(BF16), cmp_w2.astype(BF16), cmp_pos.reshape(2, 1, 2 * width),
      k_gain0.reshape(1, HEAD_DIM))


def _overlap_np(nc, n_sel):
    c0 = np.arange(nc)[:, None] * CMP_STRIDE
    s0 = np.arange(SEL_LANES)[None, :] * SEL_BLOCK
    ov = np.minimum(c0 + CMP_LEN, s0 + SEL_BLOCK) - np.maximum(c0, s0)
    ov = np.clip(ov, 0, None).astype(np.float32) / CMP_LEN
    ov[nc - 1:, :] = 0.0
    ov[:, n_sel:] = 0.0
    return ov


def _select_topk(score, ntop):
    blk = lax.broadcasted_iota(jnp.int32, score.shape, 0).astype(F32)
    for _ in range(ntop):
        m = jnp.max(score, axis=0, keepdims=True)
        first = jnp.min(jnp.where(score == m, blk, float(SEL_LANES)), axis=0, keepdims=True)
        score = jnp.where(blk == first, -jnp.inf, score)
    return score == -jnp.inf


CMP_QSUB = 2


def _cmp_attn_kernel(*refs, nj, n_sel):
    qt_ref, kc_ref, vct_ref, ovt_ref = refs[:4]
    tab_refs = refs[4:4 + nj * CMP_QSUB]
    oct_ref, unsel_ref = refs[4 + nj * CMP_QSUB:]
    i = pl.program_id(2)
    nq = CMP_QSUB * TQ
    s = _dot(kc_ref[...], qt_ref[...])
    s = jnp.concatenate(
        [jnp.concatenate([s[jt * LANES:(jt + 1) * LANES, (qs * HPG + h) * TQ:(qs * HPG + h + 1) * TQ]
                          + tab_refs[qs * nj + jt][h]
                          for qs in range(CMP_QSUB) for h in range(HPG)], axis=1) for jt in range(nj)], axis=0)
    m = jnp.maximum(jnp.max(s, axis=0, keepdims=True), 0.5 * NEG)
    e = jnp.exp2(s - m)
    l = jnp.sum(e, axis=0, keepdims=True)
    p = e * (1.0 / jnp.where(l == 0.0, 1.0, l))
    oct_ref[...] = _dot(vct_ref[...], p.astype(BF16))

    psums = []
    for qs in range(CMP_QSUB):
        acc = p[:, qs * HPG * TQ:(qs * HPG + 1) * TQ]
        for h in range(1, HPG):
            acc = acc + p[:, (qs * HPG + h) * TQ:(qs * HPG + h + 1) * TQ]
        psums.append(acc)
    psum = jnp.concatenate(psums, axis=1)
    p_hi = psum.astype(BF16)
    p_lo = (psum - p_hi.astype(F32)).astype(BF16)
    imp_t = _dot(ovt_ref[...], p_hi) + _dot(ovt_ref[...], p_lo)

    tq = i * nq + lax.broadcasted_iota(jnp.int32, (SEL_LANES, nq), 1)
    j = lax.broadcasted_iota(jnp.int32, (SEL_LANES, nq), 0)
    back = jnp.right_shift(tq, 6) - j
    forced = (j == 0) | ((back >= 0) & (back < N_LOCAL_SEL))
    causal = back >= 0
    n_free = min(SEL_TOPK, n_sel) - (N_LOCAL_SEL + 1)
    picked = _select_topk(jnp.where(causal & ~forced, imp_t, NEG), n_free)
    sel = causal & (forced | picked)
    unsel_ref[...] = jnp.where(sel, 0.0, -1.0).astype(unsel_ref.dtype)


def _cmp_attn(q_t, kvc, kvc_t, tables):
    nb = q_t.shape[0]
    s = q_t.shape[3] // HPG
    nc = s // CMP_STRIDE
    nj = nc // LANES
    n_sel = s // SEL_BLOCK
    ov_t = jnp.asarray(_overlap_np(nc, n_sel).T, dtype=BF16)

    def tab_map(qs, jt):
        def index_map(b, g, i):
            didx = CMP_QSUB * i + qs - (LANES * CMP_STRIDE // TQ) * jt
            row = jnp.where(didx < 0, CMP_MASKED, jnp.minimum(didx, CMP_CONST))
            return (CMP_BASE + row, g, 0, 0)
        return index_map

    nq = CMP_QSUB * TQ
    qt_spec = pl.BlockSpec((None, None, HEAD_DIM, HPG * nq), lambda b, g, i: (b, g, 0, i))
    n_tabs = CMP_QSUB * nj
    return pl.pallas_call(
        functools.partial(_cmp_attn_kernel, nj=nj, n_sel=n_sel),
        out_shape=(jax.ShapeDtypeStruct((nb, KV_GROUPS, HEAD_DIM, HPG * s), F32),
                   jax.ShapeDtypeStruct((nb, KV_GROUPS, SEL_LANES, s), BF16)),
        grid=(nb, KV_GROUPS, s // nq),
        in_specs=[
            qt_spec,
            pl.BlockSpec((None, None, None, nc, HEAD_DIM), lambda b, g, i: (0, b, g, 0, 0)),
            pl.BlockSpec((None, None, None, HEAD_DIM, nc), lambda b, g, i: (1, b, g, 0, 0)),
            pl.BlockSpec((SEL_LANES, nc), lambda b, g, i: (0, 0)),
        ] + [pl.BlockSpec((None, HPG, LANES, TQ), tab_map(qs, jt))
             for qs in range(CMP_QSUB) for jt in range(nj)],
        out_specs=(
            qt_spec,
            pl.BlockSpec((None, None, SEL_LANES, nq), lambda b, g, i: (b, g, 0, i)),
        ),
        compiler_params=_params(("arbitrary", "arbitrary", "arbitrary")),
        name="cmp_attn",
    )(q_t, kvc, kvc_t, ov_t, *([tables] * n_tabs))


class _FlashStream:
    def __init__(self, q_aug, k_ref, vt_ref, bias_fn, s_scr, p_scr):
        self.q_aug, self.k_ref, self.vt_ref, self.bias_fn = q_aug, k_ref, vt_ref, bias_fn
        self.s_scr, self.p_scr = s_scr, p_scr

    def _scores(self, t):
        k0 = pl.multiple_of(jnp.maximum(t, 0) * TK_SEL, TK_SEL)
        return self.bias_fn(t, _dot(self.k_ref[pl.ds(k0, TK_SEL), :], self.q_aug))

    def _pv(self, t, slot):
        k0 = pl.multiple_of(jnp.maximum(t, 0) * TK_SEL, TK_SEL)
        return _dot(self.vt_ref[:, pl.ds(k0, TK_SEL)], self.p_scr[slot])

    def _softmax(self, slot, m):
        s = self.s_scr[slot]
        m_new = jnp.maximum(m, jnp.max(s, axis=0, keepdims=True))
        self.p_scr[slot] = jnp.exp2(s - m_new).astype(self.p_scr.dtype)
        return m_new, jnp.exp2(m - m_new)

    def start(self, a):
        lanes = self.q_aug.shape[1]
        self.s_scr[0] = self._scores(a)
        self.p_scr[1] = jnp.zeros(self.p_scr.shape[1:], self.p_scr.dtype)
        return (jnp.full((1, lanes), NEG, F32), jnp.ones((1, lanes), F32), jnp.zeros((V_AUG, lanes), F32))

    def trip(self, a, carry, has_next):
        m, alpha, acc = carry
        acc = alpha * acc + self._pv(a - 1, 1)
        m, alpha = self._softmax(0, m)
        self.s_scr[1] = self._scores(a + 1)
        acc = alpha * acc + self._pv(a, 0)
        m, alpha = self._softmax(1, m)
        if has_next:
            self.s_scr[0] = self._scores(a + 2)
        return m, alpha, acc

    @staticmethod
    def _normalised(acc):
        return acc[:HEAD_DIM, :] / acc[HEAD_DIM:HEAD_DIM + 1, :]

    def finish_pair(self, a, carry):
        _, alpha, acc = carry
        return self._normalised(alpha * acc + self._pv(a + 1, 1))

    def finish_single(self, a, carry):
        m, alpha, acc = carry
        acc = alpha * acc + self._pv(a - 1, 1)
        _, alpha = self._softmax(0, m)
        return self._normalised(alpha * acc + self._pv(a, 0))


def _add_tiles(s, tile_fn):
    rows = []
    for part in range(TK_SEL // LANES):
        cols = [s[part * LANES:(part + 1) * LANES, h * TQ:(h + 1) * TQ] + tile_fn(part, h)
                for h in range(HPG)]
        rows.append(jnp.concatenate(cols, axis=1))
    return jnp.concatenate(rows, axis=0)


def _sel_win_kernel(qt_ref, ks_ref, vs_ref, kw_ref, vw_ref, unsel_ref, tab_ref, cvec_ref,
                    oct_ref, gates_ref, y_ref, ss_scr, ps_scr, sw_scr, pw_scr):
    i = pl.program_id(2)
    parts = TK_SEL // LANES

    c = cvec_ref[...]
    c_hi = c.astype(BF16).astype(F32)
    c_lo = c - c_hi
    arow = lax.broadcasted_iota(jnp.int32, (K_AUG - HEAD_DIM, HPG * TQ), 0)
    aug = jnp.where(arow == 0, c_hi, jnp.where(arow == 1, c_lo, 0.0)).astype(BF16)
    q_win = jnp.concatenate([qt_ref[...], aug], axis=0)
    unsel_t = unsel_ref[...]
    q_sel = jnp.concatenate([q_win, jnp.concatenate([unsel_t] * HPG, axis=1)], axis=0)

    def sel_bias(t, s):
        def tile(part, h):
            delta = i - parts * t - part
            row = jnp.where(delta < 0, SEL_MASKED, jnp.minimum(delta, SEL_CONST))
            return tab_ref[row, h]

        return _add_tiles(s, tile)

    def win_bias(t, s):
        def tile(part, h):
            delta = i - parts * t - part
            hidden = (delta < 0) | (delta >= N_WIN_NEAR) | (t < 0)
            return tab_ref[jnp.where(hidden, SEL_MASKED, WIN_BASE + delta), h]

        return _add_tiles(s, tile)

    n_tiles = i // parts + 1
    n_pairs = (n_tiles + 1) // 2
    sel = _FlashStream(q_sel, ks_ref, vs_ref, sel_bias, ss_scr, ps_scr)
    win = _FlashStream(q_win, kw_ref, vw_ref, win_bias, sw_scr, pw_scr)
    win_a0 = n_tiles - WIN_TILES
    last_pair = 2 * (n_pairs - 1)
    carry_s = sel.start(0)
    carry_w = win.start(win_a0)
    carry_s = lax.fori_loop(0, n_pairs - 1, lambda u, c: sel.trip(2 * u, c, has_next=True), carry_s)
    carry_s = sel.trip(last_pair, carry_s, has_next=False)
    carry_w = win.trip(win_a0, carry_w, has_next=True)
    os_t = sel.finish_pair(last_pair, carry_s)
    ow_t = win.finish_single(win_a0 + 2, carry_w)

    oc_t = oct_ref[...]
    gt = gates_ref[...]
    for hg in range(HPG):
        c0 = N_BRANCH * hg
        lanes = slice(hg * TQ, (hg + 1) * TQ)
        out_t = (gt[c0:c0 + 1, :] * oc_t[:, lanes] + gt[c0 + 1:c0 + 2, :] * os_t[:, lanes]
                 + gt[c0 + 2:c0 + 3, :] * ow_t[:, lanes])
        y_ref[:, hg * HEAD_DIM:(hg + 1) * HEAD_DIM] = jnp.transpose(out_t).astype(y_ref.dtype)


def _sel_win_attn(q_t, ks, vs_t, kw, vw_t, unsel, tables, cvec, oc_t, gates_t):
    nb = q_t.shape[0]
    s = q_t.shape[3] // HPG
    ks_spec = pl.BlockSpec((None, None, s, KS_AUG), lambda b, g, i: (b, g, 0, 0))
    kw_spec = pl.BlockSpec((None, None, s, K_AUG), lambda b, g, i: (b, g, 0, 0))
    v_spec = pl.BlockSpec((None, None, V_AUG, s), lambda b, g, i: (b, g, 0, 0))
    qt_spec = pl.BlockSpec((None, None, HEAD_DIM, HPG * TQ), lambda b, g, i: (b, g, 0, i))
    gw = HPG * HEAD_DIM
    return pl.pallas_call(
        _sel_win_kernel,
        out_shape=jax.ShapeDtypeStruct((nb, s, NSA_HEADS * HEAD_DIM), BF16),
        grid=(nb, KV_GROUPS, s // TQ),
        in_specs=[
            qt_spec,
            ks_spec, v_spec, kw_spec, v_spec,
            pl.BlockSpec((None, None, SEL_LANES, TQ), lambda b, g, i: (b, g, 0, i)),
            pl.BlockSpec((CMP_BASE, HPG, LANES, TQ), lambda b, g, i: (0, g, 0, 0)),
            pl.BlockSpec((None, 1, HPG * TQ), lambda b, g, i: (g, 0, 0)),
            qt_spec,
            pl.BlockSpec((None, None, GATE_ROWS, TQ), lambda b, g, i: (b, g, 0, i)),
        ],
        out_specs=pl.BlockSpec((None, TQ, gw), lambda b, g, i: (b, i, g)),
        scratch_shapes=[pltpu.VMEM((2, TK_SEL, HPG * TQ), F32), pltpu.VMEM((2, TK_SEL, HPG * TQ), BF16),
                        pltpu.VMEM((2, TK_SEL, HPG * TQ), F32), pltpu.VMEM((2, TK_SEL, HPG * TQ), BF16)],
        compiler_params=_params(("arbitrary", "arbitrary", "arbitrary")),
        name="sel_win_attn",
    )(q_t, ks, vs_t, kw, vw_t, unsel, tables, cvec, oc_t, gates_t)


def kernel(x, mem, mix_norm, a_w_in, a_conv_w, b_w_in, b_q_gain, kv_norm, w_kv_shared, k_gain,
           cmp_pos, cmp_w1, cmp_w2, rel_bias, mem_norm, mem_w_kv, mem_q_gain, mem_k_gain, w_out,
           mlp_norm, w_up, w_down):
    nb, s, d = x.shape
    depth = mix_norm.shape[0]
    n_a = a_w_in.shape[0]
    assert s % (LANES * CMP_STRIDE) == 0 and s // SEL_BLOCK <= SEL_LANES
    assert WIN_BASE + N_WIN_NEAR <= CMP_BASE
    assert WIN_TILES == 3

    mk, mv_t = _mem_kv(mem, mem_norm, mem_w_kv, mem_k_gain)
    tables = _bias_tables(rel_bias)
    cvec = jnp.repeat(rel_bias[REL_BUCKETS - 1].astype(F32) * LOG2E, TQ).reshape(KV_GROUPS, 1, HPG * TQ)
    w_out_bf = w_out.astype(BF16)
    shared = None
    for layer in range(depth):
        if layer < n_a:
            y_tok, y_mem = _mixer_a(x, mix_norm[layer], a_w_in[layer], a_conv_w[layer],
                                    mk[layer], mv_t[layer], mem_q_gain[layer])
        else:
            j = layer - n_a
            ks, vs_t, kw, vw_t, kvc, kvc_t = shared
            q_t, gates_t, y_mem = _b_inproj(x, mix_norm[layer], b_w_in[j], b_q_gain[j],
                                            mk[layer], mv_t[layer], mem_q_gain[layer])
            oc_t, unsel = _cmp_attn(q_t, kvc, kvc_t, tables)
            y_tok = _sel_win_attn(q_t, ks, vs_t, kw, vw_t, unsel, tables, cvec, oc_t, gates_t)
        x = _post(x.reshape(nb * s, d), y_tok.reshape(nb * s, -1), y_mem.reshape(nb * s, -1),
                  w_out_bf, mlp_norm[layer], w_up, w_down, layer).reshape(nb, s, d)
        if layer == n_a - 1:
            ucmp, ks, vs_t, kw, vw_t = _build_kv(x, kv_norm, w_kv_shared, k_gain)
            kvc, kvc_t = _compress(ucmp, cmp_pos, cmp_w1, cmp_w2, k_gain[0])
            shared = (ks, vs_t, kw, vw_t, kvc, kvc_t)
    return x
```

```python
import functools
import math

import numpy as np
import jax
import jax.numpy as jnp
from jax import lax
from jax.experimental import pallas as pl
from jax.experimental.pallas import tpu as pltpu

F32 = jnp.float32
BF16 = jnp.bfloat16

HEAD_DIM = 64
MEM_HEADS = 4
MEM_WIDTH = MEM_HEADS * HEAD_DIM
CONV_CH = 768
NSA_HEADS = 12
KV_GROUPS = 2
HPG = NSA_HEADS // KV_GROUPS
N_BRANCH = 3
CMP_LEN = 32
CMP_STRIDE = 16
SEL_BLOCK = 64
SEL_TOPK = 16
N_LOCAL_SEL = 2
WINDOW = 512
REL_BUCKETS = 32
REL_MAX_DIST = 1024
EPS = 1e-6
NEG = -1e30
FORCE_SCORE = 1e4
LOG2E = math.log2(math.e)
MASK_BIG = 2.0 ** 100

LANES = 128
SUBLANES = 8
VMEM_LIMIT_BYTES = 56 * 1024 * 1024

TQ = 128
TK_SEL = 256
SEL_LANES = 128
TM = 512
WIN_TILES = max(((TQ * i - (WINDOW - 1)) % TK_SEL + WINDOW + TQ - 2) // TK_SEL + 1 for i in range(TK_SEL))

N_SEL_NEAR = 8
SEL_CONST = N_SEL_NEAR
SEL_MASKED = N_SEL_NEAR + 1
WIN_BASE = N_SEL_NEAR + 2
N_WIN_NEAR = WINDOW // LANES + 1
CMP_BASE = 16
N_CMP_NEAR = 23
CMP_CONST = N_CMP_NEAR
CMP_MASKED = N_CMP_NEAR + 1
N_CMP_TILES = N_CMP_NEAR + 2
N_TILES = CMP_BASE + N_CMP_TILES


def _rms(xf, g):
    ms = jnp.mean(xf * xf, axis=-1, keepdims=True)
    return xf * lax.rsqrt(ms + EPS) * g


def _dot(a, b):
    return jnp.dot(a, b, preferred_element_type=F32)


def _dot_nt(a, b):
    return lax.dot_general(a, b, (((1,), (1,)), ((), ())), preferred_element_type=F32)


def _params(sem):
    return pltpu.CompilerParams(dimension_semantics=sem, vmem_limit_bytes=VMEM_LIMIT_BYTES)


def _mem_kv_kernel(mem_ref, mnorm_ref, w_ref, kg_ref, mk_ref, mvt_ref, *, nb, ml):
    mh = _rms(mem_ref[...], mnorm_ref[...]).astype(BF16)
    kv = _dot(mh, w_ref[...])
    for h in range(MEM_HEADS):
        kh = _rms(kv[:, h * HEAD_DIM:(h + 1) * HEAD_DIM], kg_ref[...]).astype(BF16)
        vh = kv[:, MEM_WIDTH + h * HEAD_DIM:MEM_WIDTH + (h + 1) * HEAD_DIM]
        for b in range(nb):
            mk_ref[b, h] = kh[b * ml:(b + 1) * ml]
            mvt_ref[b, h] = jnp.transpose(vh[b * ml:(b + 1) * ml]).astype(BF16)


def _mem_kv(mem, mem_norm, mem_w_kv, mem_k_gain):
    nb, ml, d = mem.shape
    depth = mem_w_kv.shape[0]
    out = jax.ShapeDtypeStruct((depth, nb, MEM_HEADS, ml, HEAD_DIM), BF16)
    out_t = jax.ShapeDtypeStruct((depth, nb, MEM_HEADS, HEAD_DIM, ml), BF16)
    return pl.pallas_call(
        functools.partial(_mem_kv_kernel, nb=nb, ml=ml),
        out_shape=(out, out_t),
        grid=(depth,),
        in_specs=[
            pl.BlockSpec((nb * ml, d), lambda l: (0, 0)),
            pl.BlockSpec((1, d), lambda l: (0, 0)),
            pl.BlockSpec((None, d, 2 * MEM_WIDTH), lambda l: (l, 0, 0)),
            pl.BlockSpec((None, 1, HEAD_DIM), lambda l: (l, 0, 0)),
        ],
        out_specs=(
            pl.BlockSpec((None, nb, MEM_HEADS, ml, HEAD_DIM), lambda l: (l, 0, 0, 0, 0)),
            pl.BlockSpec((None, nb, MEM_HEADS, HEAD_DIM, ml), lambda l: (l, 0, 0, 0, 0)),
        ),
        compiler_params=_params(("arbitrary",)),
        name="mem_kv",
    )(mem.reshape(nb * ml, d), mem_norm.reshape(1, d), mem_w_kv.astype(BF16),
      mem_k_gain.reshape(depth, 1, HEAD_DIM))


def _rms_t(xt, g_col):
    ms = jnp.mean(xt * xt, axis=0, keepdims=True)
    return xt * lax.rsqrt(ms + EPS) * g_col


def _mem_attn(qm, mk_ref, mvt_ref, qg_col, ymem_ref):
    qm_t = jnp.transpose(qm)
    outs = []
    for h in range(MEM_HEADS):
        qh = _rms_t(qm_t[h * HEAD_DIM:(h + 1) * HEAD_DIM, :], qg_col) * (HEAD_DIM ** -0.5)
        lg = _dot(mk_ref[h], qh.astype(BF16))
        m = jnp.max(lg, axis=0, keepdims=True)
        e = jnp.exp(lg - m)
        l = jnp.sum(e, axis=0, keepdims=True)
        outs.append(_dot(mvt_ref[h], e.astype(BF16)) / l)
    ymem_ref[...] = jnp.transpose(jnp.concatenate(outs, axis=0)).astype(ymem_ref.dtype)


def _bucket_np(d):
    n = np.maximum(d, 0)
    max_exact = REL_BUCKETS // 2
    nf = np.maximum(n, 1).astype(np.float64)
    large = max_exact + (np.log(nf / max_exact) / math.log(REL_MAX_DIST / max_exact)
                         * (REL_BUCKETS - max_exact)).astype(np.int32)
    large = np.minimum(large, REL_BUCKETS - 1)
    return np.where(n < max_exact, n, large).astype(np.int32)


def _bucket_tiles():
    r = np.arange(TQ)[:, None]
    k = np.arange(LANES)[None, :]
    tiles = []
    for delta in range(N_SEL_NEAR):
        d = delta * LANES + r - k
        tiles.append(np.where(d >= 0, _bucket_np(d), -1).T)
    tiles.append(np.full((TQ, LANES), REL_BUCKETS - 1))
    tiles.append(np.full((TQ, LANES), -1))
    for delta in range(N_WIN_NEAR):
        d = delta * LANES + r - k
        tiles.append(np.where((d >= 0) & (d < WINDOW), _bucket_np(d), -1).T)
    while len(tiles) < CMP_BASE:
        tiles.append(np.full((TQ, LANES), -1))
    for delta in range(N_CMP_NEAR):
        d = delta * LANES + r - CMP_STRIDE * k - (CMP_LEN - 1)
        tiles.append(np.where(d >= 0, _bucket_np(d), -1).T)
    tiles.append(np.full((TQ, LANES), REL_BUCKETS - 1))
    tiles.append(np.full((TQ, LANES), -1))
    return np.stack(tiles).astype(np.int32)


def _bias_tables_kernel(tab_ref, ids_ref, out_ref):
    relative = pl.program_id(0) < CMP_BASE
    n_bits = REL_BUCKETS.bit_length() - 1
    for h in range(NSA_HEADS):
        sub = jnp.where(relative, tab_ref[REL_BUCKETS - 1, h], 0.0)
        leaves = [(jnp.full((SUBLANES, LANES), tab_ref[b, h], F32) - sub) * LOG2E for b in range(REL_BUCKETS)]
        for r0 in range(0, ids_ref.shape[0], SUBLANES):
            ids = ids_ref[r0:r0 + SUBLANES, :]
            level = leaves
            for k in range(n_bits):
                bit = jnp.bitwise_and(ids, 1 << k) != 0
                level = [jnp.where(bit, level[2 * j + 1], level[2 * j]) for j in range(len(level) // 2)]
            out_ref[h, r0:r0 + SUBLANES, :] = jnp.where(ids < 0, NEG, level[0])


def _bias_tables(rel_bias):
    ids = jnp.asarray(_bucket_tiles())
    return pl.pallas_call(
        _bias_tables_kernel,
        out_shape=jax.ShapeDtypeStruct((N_TILES, NSA_HEADS, TQ, LANES), F32),
        grid=(N_TILES,),
        in_specs=[
            pl.BlockSpec(memory_space=pltpu.SMEM),
            pl.BlockSpec((None, TQ, LANES), lambda t: (t, 0, 0)),
        ],
        out_specs=pl.BlockSpec((None, NSA_HEADS, TQ, LANES), lambda t: (t, 0, 0, 0)),
        compiler_params=_params(("arbitrary",)),
        name="bias_tables",
    )(rel_bias.astype(F32), ids)


def _mixer_a_kernel(x_ref, g_ref, w_ref, cw_ref, mk_ref, mvt_ref, qg_ref,
                    ytok_ref, ymem_ref, carry_ref, *, tm):
    @pl.when(pl.program_id(1) == 0)
    def _():
        carry_ref[...] = jnp.zeros_like(carry_ref)

    h = _rms(x_ref[...], g_ref[...]).astype(BF16)
    z = _dot(h, w_ref[...])
    gate_b = z[:, :CONV_CH]
    v = z[:, CONV_CH:2 * CONV_CH] * z[:, 2 * CONV_CH:3 * CONV_CH]
    prev = carry_ref[...]
    row = lax.broadcasted_iota(jnp.int32, (tm, CONV_CH), 0)
    v1 = jnp.where(row == 0, prev[7:8, :], pltpu.roll(v, 1, 0))
    v2 = jnp.where(row == 0, prev[6:7, :], jnp.where(row == 1, prev[7:8, :], pltpu.roll(v, 2, 0)))
    cw = cw_ref[...]
    y = gate_b * (cw[0:1, :] * v2 + cw[1:2, :] * v1 + cw[2:3, :] * v)
    carry_ref[...] = v[tm - 8:, :]
    ytok_ref[...] = y.astype(ytok_ref.dtype)
    _mem_attn(z[:, 3 * CONV_CH:], mk_ref, mvt_ref, qg_ref[...], ymem_ref)


def _mixer_a(x, norm_g, w_in, conv_w, mk, mv_t, mem_q_gain):
    nb, s, d = x.shape
    ml = mk.shape[2]
    n_in = w_in.shape[1]
    return pl.pallas_call(
        functools.partial(_mixer_a_kernel, tm=TM),
        out_shape=(jax.ShapeDtypeStruct((nb, s, CONV_CH), BF16),
                   jax.ShapeDtypeStruct((nb, s, MEM_WIDTH), BF16)),
        grid=(nb, s // TM),
        in_specs=[
            pl.BlockSpec((None, TM, d), lambda b, i: (b, i, 0)),
            pl.BlockSpec((1, d), lambda b, i: (0, 0)),
            pl.BlockSpec((d, n_in), lambda b, i: (0, 0)),
            pl.BlockSpec((3, CONV_CH), lambda b, i: (0, 0)),
            pl.BlockSpec((None, MEM_HEADS, ml, HEAD_DIM), lambda b, i: (b, 0, 0, 0)),
            pl.BlockSpec((None, MEM_HEADS, HEAD_DIM, ml), lambda b, i: (b, 0, 0, 0)),
            pl.BlockSpec((HEAD_DIM, 1), lambda b, i: (0, 0)),
        ],
        out_specs=(
            pl.BlockSpec((None, TM, CONV_CH), lambda b, i: (b, i, 0)),
            pl.BlockSpec((None, TM, MEM_WIDTH), lambda b, i: (b, i, 0)),
        ),
        scratch_shapes=[pltpu.VMEM((8, CONV_CH), F32)],
        compiler_params=_params(("arbitrary", "arbitrary")),
        name="mixer_a",
    )(x, norm_g.reshape(1, d), w_in.astype(BF16), conv_w, mk, mv_t, mem_q_gain.reshape(HEAD_DIM, 1))


B_GATE0 = NSA_HEADS * HEAD_DIM
B_QMEM0 = B_GATE0 + KV_GROUPS * LANES
B_WIDTH = B_QMEM0 + MEM_WIDTH


GATE_ROWS = 32


def _b_inproj_kernel(x_ref, g_ref, w_ref, qg_ref, mk_ref, mvt_ref, mqg_ref,
                     qt_ref, gates_ref, ymem_ref):
    h = _rms(x_ref[...], g_ref[...]).astype(BF16)
    z = _dot(h, w_ref[...])
    qg = qg_ref[...]
    tm = z.shape[0]
    for pair in range(NSA_HEADS // 2):
        z_t = jnp.transpose(z[:, pair * LANES:(pair + 1) * LANES])
        for half in range(2):
            qh_t = _rms_t(z_t[half * HEAD_DIM:(half + 1) * HEAD_DIM, :], qg) * (HEAD_DIM ** -0.5 * LOG2E)
            qh_t = qh_t.astype(qt_ref.dtype)
            g, hg = divmod(2 * pair + half, HPG)
            for qt in range(tm // TQ):
                lane0 = (qt * HPG + hg) * TQ
                qt_ref[g, :, lane0:lane0 + TQ] = qh_t[:, qt * TQ:(qt + 1) * TQ]
    for g in range(KV_GROUPS):
        gl = z[:, B_GATE0 + g * LANES:B_GATE0 + (g + 1) * LANES]
        gates_ref[g] = jnp.transpose(1.0 / (1.0 + jnp.exp(-gl)))[:GATE_ROWS, :]
    _mem_attn(z[:, B_QMEM0:], mk_ref, mvt_ref, mqg_ref[...], ymem_ref)


def _b_inproj(x, norm_g, w_in, q_gain, mk, mv_t, mem_q_gain):
    nb, s, d = x.shape
    ml = mk.shape[2]
    nq = NSA_HEADS * HEAD_DIM
    ng = HPG * N_BRANCH
    pad = jnp.zeros((d, LANES - ng), w_in.dtype)
    w = jnp.concatenate([w_in[:, :nq], w_in[:, nq:nq + ng], pad,
                         w_in[:, nq + ng:nq + 2 * ng], pad, w_in[:, nq + 2 * ng:]], axis=1)
    return pl.pallas_call(
        _b_inproj_kernel,
        out_shape=(jax.ShapeDtypeStruct((nb, KV_GROUPS, HEAD_DIM, HPG * s), BF16),
                   jax.ShapeDtypeStruct((nb, KV_GROUPS, GATE_ROWS, s), F32),
                   jax.ShapeDtypeStruct((nb, s, MEM_WIDTH), BF16)),
        grid=(nb, s // TM),
        in_specs=[
            pl.BlockSpec((None, TM, d), lambda b, i: (b, i, 0)),
            pl.BlockSpec((1, d), lambda b, i: (0, 0)),
            pl.BlockSpec((d, B_WIDTH), lambda b, i: (0, 0)),
            pl.BlockSpec((HEAD_DIM, 1), lambda b, i: (0, 0)),
            pl.BlockSpec((None, MEM_HEADS, ml, HEAD_DIM), lambda b, i: (b, 0, 0, 0)),
            pl.BlockSpec((None, MEM_HEADS, HEAD_DIM, ml), lambda b, i: (b, 0, 0, 0)),
            pl.BlockSpec((HEAD_DIM, 1), lambda b, i: (0, 0)),
        ],
        out_specs=(
            pl.BlockSpec((None, KV_GROUPS, HEAD_DIM, HPG * TM), lambda b, i: (b, 0, 0, i)),
            pl.BlockSpec((None, KV_GROUPS, GATE_ROWS, TM), lambda b, i: (b, 0, 0, i)),
            pl.BlockSpec((None, TM, MEM_WIDTH), lambda b, i: (b, i, 0)),
        ),
        compiler_params=_params(("arbitrary", "arbitrary")),
        name="b_inproj",
    )(x, norm_g.reshape(1, d), w.astype(BF16), q_gain.reshape(HEAD_DIM, 1), mk, mv_t,
      mem_q_gain.reshape(HEAD_DIM, 1))


FF_CHUNK = 1024
TM_POST = 1024


def _post_kernel(x_ref, ytok_ref, ymem_ref, woa_ref, wob_ref, g_ref, wup_ref, wdn_ref, o_ref, hm_ref):
    @pl.when(pl.program_id(1) == 0)
    def _():
        x1 = x_ref[...] + _dot(ytok_ref[...], woa_ref[...]) + _dot(ymem_ref[...], wob_ref[...])
        o_ref[...] = x1
        hm_ref[...] = _rms(x1, g_ref[...]).astype(hm_ref.dtype)

    a = _dot(hm_ref[...], wup_ref[...].astype(BF16))
    a = jnp.square(jnp.maximum(a, 0.0)).astype(BF16)
    o_ref[...] += _dot(a, wdn_ref[...].astype(BF16))


def _post(x, y_tok, y_mem, w_out, norm_g, w_up, w_down, layer):
    n, d = x.shape
    ff = w_up.shape[2]
    nt = y_tok.shape[1]
    assert nt % MEM_WIDTH == 0
    tm = min(TM_POST, n)
    const = lambda i, c: (0, 0)
    rows = lambda i, c: (i, 0)
    return pl.pallas_call(
        _post_kernel,
        out_shape=jax.ShapeDtypeStruct((n, d), F32),
        grid=(n // tm, ff // FF_CHUNK),
        in_specs=[
            pl.BlockSpec((tm, d), rows),
            pl.BlockSpec((tm, nt), rows),
            pl.BlockSpec((tm, MEM_WIDTH), rows),
            pl.BlockSpec((None, nt, d), lambda i, c: (layer, 0, 0)),
            pl.BlockSpec((None, MEM_WIDTH, d), lambda i, c: (layer, nt // MEM_WIDTH, 0)),
            pl.BlockSpec((1, d), const),
            pl.BlockSpec((None, d, FF_CHUNK), lambda i, c: (layer, 0, c)),
            pl.BlockSpec((None, FF_CHUNK, d), lambda i, c: (layer, c, 0)),
        ],
        out_specs=pl.BlockSpec((tm, d), rows),
        scratch_shapes=[pltpu.VMEM((tm, d), BF16)],
        compiler_params=_params(("arbitrary", "arbitrary")),
        name="post",
    )(x, y_tok, y_mem, w_out, w_out, norm_g.reshape(1, d), w_up, w_down)


K_AUG = 2 * HEAD_DIM
KS_AUG = K_AUG + SEL_LANES
V_AUG = HEAD_DIM + 16


def _build_kv_kernel(x_ref, g_ref, w_ref, kg_ref, ucmp_ref, ks_ref, vs_ref, kw_ref, vw_ref, raw_scr):
    h = _rms(x_ref[...], g_ref[...]).astype(BF16)
    kv = _dot(h, w_ref[...])
    gw = KV_GROUPS * HEAD_DIM
    tm = kv.shape[0]
    for ty in range(2):
        raw_scr[ty] = kv[:, ty * gw:(ty + 1) * gw]
        for t in range(CMP_STRIDE):
            rows = raw_scr[ty, pl.ds(t, tm // CMP_STRIDE, stride=CMP_STRIDE), :]
            for g in range(KV_GROUPS):
                ucmp_ref[ty, g, :, t * HEAD_DIM:(t + 1) * HEAD_DIM] = (
                    rows[:, g * HEAD_DIM:(g + 1) * HEAD_DIM].astype(ucmp_ref.dtype))
    kg = kg_ref[...]
    ones = (lax.broadcasted_iota(jnp.int32, (tm, K_AUG - HEAD_DIM), 1) < 2).astype(ks_ref.dtype)
    pos = pl.program_id(1) * tm + lax.broadcasted_iota(jnp.int32, (tm, SEL_LANES), 0)
    blk = lax.broadcasted_iota(jnp.int32, (tm, SEL_LANES), 1)
    onehot = jnp.where(blk == jnp.right_shift(pos, 6), MASK_BIG, 0.0).astype(ks_ref.dtype)
    ones_row = (lax.broadcasted_iota(jnp.int32, (V_AUG - HEAD_DIM, tm), 0) == 0).astype(vs_ref.dtype)
    for g in range(KV_GROUPS):
        sl = lambda t: kv[:, t * gw + g * HEAD_DIM:t * gw + (g + 1) * HEAD_DIM]
        ks_ref[g, :, :HEAD_DIM] = _rms(sl(2), kg[1:2, :]).astype(ks_ref.dtype)
        ks_ref[g, :, HEAD_DIM:K_AUG] = ones
        ks_ref[g, :, K_AUG:] = onehot
        vs_ref[g, :HEAD_DIM, :] = jnp.transpose(sl(3)).astype(vs_ref.dtype)
        vs_ref[g, HEAD_DIM:, :] = ones_row
        kw_ref[g, :, :HEAD_DIM] = _rms(sl(4), kg[2:3, :]).astype(kw_ref.dtype)
        kw_ref[g, :, HEAD_DIM:] = ones
        vw_ref[g, :HEAD_DIM, :] = jnp.transpose(sl(5)).astype(vw_ref.dtype)
        vw_ref[g, HEAD_DIM:, :] = ones_row


def _build_kv(x, kv_norm, w_kv, k_gain):
    nb, s, d = x.shape
    gw = KV_GROUPS * HEAD_DIM
    ks = jax.ShapeDtypeStruct((nb, KV_GROUPS, s, KS_AUG), BF16)
    kw = jax.ShapeDtypeStruct((nb, KV_GROUPS, s, K_AUG), BF16)
    vt = jax.ShapeDtypeStruct((nb, KV_GROUPS, V_AUG, s), BF16)
    ks_spec = pl.BlockSpec((None, KV_GROUPS, TM, KS_AUG), lambda b, i: (b, 0, i, 0))
    kw_spec = pl.BlockSpec((None, KV_GROUPS, TM, K_AUG), lambda b, i: (b, 0, i, 0))
    v_spec = pl.BlockSpec((None, KV_GROUPS, V_AUG, TM), lambda b, i: (b, 0, 0, i))
    return pl.pallas_call(
        _build_kv_kernel,
        out_shape=(jax.ShapeDtypeStruct((2, nb, KV_GROUPS, s // CMP_STRIDE, CMP_STRIDE * HEAD_DIM), BF16),
                   ks, vt, kw, vt),
        grid=(nb, s // TM),
        in_specs=[
            pl.BlockSpec((None, TM, d), lambda b, i: (b, i, 0)),
            pl.BlockSpec((1, d), lambda b, i: (0, 0)),
            pl.BlockSpec((d, 6 * gw), lambda b, i: (0, 0)),
            pl.BlockSpec((N_BRANCH, HEAD_DIM), lambda b, i: (0, 0)),
        ],
        out_specs=(pl.BlockSpec((2, None, KV_GROUPS, TM // CMP_STRIDE, CMP_STRIDE * HEAD_DIM),
                                lambda b, i: (0, b, 0, i, 0)),
                   ks_spec, v_spec, kw_spec, v_spec),
        scratch_shapes=[pltpu.VMEM((2, TM, gw), F32)],
        compiler_params=_params(("arbitrary", "arbitrary")),
        name="build_kv",
    )(x, kv_norm.reshape(1, d), w_kv.astype(BF16), k_gain)


def _compress_kernel(u_ref, w1_ref, w2_ref, pos_ref, kg_ref, out_ref, out_t_ref, *, nc):
    half = CMP_STRIDE * HEAD_DIM
    u = u_ref[...]
    lo = _dot(u, w1_ref[:half, :])
    hi = _dot(u, w1_ref[half:, :])
    posb = _dot(jnp.broadcast_to(pos_ref[...], (8, 2 * half)).astype(BF16), w1_ref[...])[0:1, :]
    h1 = lo + pltpu.roll(hi, nc - 1, 0) + posb
    hid = 0.5 * h1 * (1.0 + jnp.tanh(math.sqrt(2.0 / math.pi) * (h1 + 0.044715 * (h1 * h1 * h1))))
    c = _dot(hid.astype(BF16), w2_ref[...])
    is_key = pl.program_id(0) == 0
    out = jnp.where(is_key, _rms(c, kg_ref[...]), c)
    out_ref[...] = out.astype(out_ref.dtype)
    out_t_ref[...] = jnp.transpose(out).astype(out_t_ref.dtype)


def _compress(u, cmp_pos, cmp_w1, cmp_w2, k_gain0):
    _, nb, _, nc, width = u.shape
    hidden = cmp_w1.shape[2]
    return pl.pallas_call(
        functools.partial(_compress_kernel, nc=nc),
        out_shape=(jax.ShapeDtypeStruct((2, nb, KV_GROUPS, nc, HEAD_DIM), BF16),
                   jax.ShapeDtypeStruct((2, nb, KV_GROUPS, HEAD_DIM, nc), BF16)),
        grid=(2, nb, KV_GROUPS),
        in_specs=[
            pl.BlockSpec((None, None, None, nc, width), lambda t, b, g: (t, b, g, 0, 0)),
            pl.BlockSpec((None, 2 * width, hidden), lambda t, b, g: (t, 0, 0)),
            pl.BlockSpec((None, hidden, HEAD_DIM), lambda t, b, g: (t, 0, 0)),
            pl.BlockSpec((None, 1, 2 * width), lambda t, b, g: (t, 0, 0)),
            pl.BlockSpec((1, HEAD_DIM), lambda t, b, g: (0, 0)),
        ],
        out_specs=(pl.BlockSpec((None, None, None, nc, HEAD_DIM), lambda t, b, g: (t, b, g, 0, 0)),
                   pl.BlockSpec((None, None, None, HEAD_DIM, nc), lambda t, b, g: (t, b, g, 0, 0))),
        compiler_params=_params(("arbitrary", "arbitrary", "arbitrary")),
        name="compress",
    )(u, cmp_w1.astype(BF16), cmp_w2.astype(BF16), cmp_pos.reshape(2, 1, 2 * width),
      k_gain0.reshape(1, HEAD_DIM))


def _overlap_np(nc, n_sel):
    c0 = np.arange(nc)[:, None] * CMP_STRIDE
    s0 = np.arange(SEL_LANES)[None, :] * SEL_BLOCK
    ov = np.minimum(c0 + CMP_LEN, s0 + SEL_BLOCK) - np.maximum(c0, s0)
    ov = np.clip(ov, 0, None).astype(np.float32) / CMP_LEN
    ov[nc - 1:, :] = 0.0
    ov[:, n_sel:] = 0.0
    return ov


def _select_topk(score, ntop):
    blk = lax.broadcasted_iota(jnp.int32, score.shape, 0).astype(F32)
    for _ in range(ntop):
        m = jnp.max(score, axis=0, keepdims=True)
        first = jnp.min(jnp.where(score == m, blk, float(SEL_LANES)), axis=0, keepdims=True)
        score = jnp.where(blk == first, -jnp.inf, score)
    return score == -jnp.inf


CMP_QSUB = 2
TOPK_GROUP = 32


def _cmp_attn_kernel(*refs, nj, n_sel):
    qt_ref, kc_ref, vct_ref, ovt_ref = refs[:4]
    tab_refs = refs[4:4 + nj * CMP_QSUB]
    oct_ref, unsel_ref = refs[4 + nj * CMP_QSUB:]
    i = pl.program_id(2)
    nq = CMP_QSUB * TQ
    def attend(nk):
        rows = nk * LANES
        s = _dot(kc_ref[:rows, :], qt_ref[...])
        s = jnp.concatenate(
            [jnp.concatenate([s[jt * LANES:(jt + 1) * LANES, (qs * HPG + h) * TQ:(qs * HPG + h + 1) * TQ]
                              + tab_refs[qs * nj + jt][h]
                              for qs in range(CMP_QSUB) for h in range(HPG)], axis=1) for jt in range(nk)], axis=0)
        m = jnp.maximum(jnp.max(s, axis=0, keepdims=True), 0.5 * NEG)
        e = jnp.exp2(s - m)
        l = jnp.sum(e, axis=0, keepdims=True)
        p = e * (1.0 / jnp.where(l == 0.0, 1.0, l))
        oct_ref[...] = _dot(vct_ref[:, :rows], p.astype(BF16))

        psums = []
        for qs in range(CMP_QSUB):
            acc = p[:, qs * HPG * TQ:(qs * HPG + 1) * TQ]
            for h in range(1, HPG):
                acc = acc + p[:, (qs * HPG + h) * TQ:(qs * HPG + h + 1) * TQ]
            psums.append(acc)
        psum = jnp.concatenate(psums, axis=1)
        p_hi = psum.astype(BF16)
        p_lo = (psum - p_hi.astype(F32)).astype(BF16)
        return _dot(ovt_ref[:, :rows], p_hi) + _dot(ovt_ref[:, :rows], p_lo)

    last_visible = (nq * (i + 1) - CMP_LEN) // CMP_STRIDE
    imp_t = lax.switch(last_visible // LANES, [functools.partial(attend, nk) for nk in range(1, nj + 1)])

    tq = i * nq + lax.broadcasted_iota(jnp.int32, (SEL_LANES, nq), 1)
    j = lax.broadcasted_iota(jnp.int32, (SEL_LANES, nq), 0)
    back = jnp.right_shift(tq, 6) - j
    forced = (j == 0) | ((back >= 0) & (back < N_LOCAL_SEL))
    causal = back >= 0
    n_free = min(SEL_TOPK, n_sel) - (N_LOCAL_SEL + 1)
    score = jnp.where(causal & ~forced, imp_t, NEG)

    def pick(n_groups):
        rows = n_groups * TOPK_GROUP
        top = _select_topk(score[:rows, :], n_free).astype(F32)
        if rows == SEL_LANES:
            return top
        return jnp.concatenate([top, jnp.zeros((SEL_LANES - rows, nq), F32)], axis=0)

    causal_blocks = (nq * (i + 1)) // SEL_BLOCK
    picked = lax.switch((causal_blocks - 1) // TOPK_GROUP,
                        [functools.partial(pick, n) for n in range(1, SEL_LANES // TOPK_GROUP + 1)])
    sel = causal & (forced | (picked > 0.0))
    unsel_ref[...] = jnp.where(sel, 0.0, -1.0).astype(unsel_ref.dtype)


def _cmp_attn(q_t, kvc, kvc_t, tables):
    nb = q_t.shape[0]
    s = q_t.shape[3] // HPG
    nc = s // CMP_STRIDE
    nj = nc // LANES
    n_sel = s // SEL_BLOCK
    ov_t = jnp.asarray(_overlap_np(nc, n_sel).T, dtype=BF16)

    def tab_map(qs, jt):
        def index_map(b, g, i):
            didx = CMP_QSUB * i + qs - (LANES * CMP_STRIDE // TQ) * jt
            row = jnp.where(didx < 0, CMP_MASKED, jnp.minimum(didx, CMP_CONST))
            return (CMP_BASE + row, g, 0, 0)
        return index_map

    nq = CMP_QSUB * TQ
    qt_spec = pl.BlockSpec((None, None, HEAD_DIM, HPG * nq), lambda b, g, i: (b, g, 0, i))
    n_tabs = CMP_QSUB * nj
    return pl.pallas_call(
        functools.partial(_cmp_attn_kernel, nj=nj, n_sel=n_sel),
        out_shape=(jax.ShapeDtypeStruct((nb, KV_GROUPS, HEAD_DIM, HPG * s), F32),
                   jax.ShapeDtypeStruct((nb, KV_GROUPS, SEL_LANES, s), BF16)),
        grid=(nb, KV_GROUPS, s // nq),
        in_specs=[
            qt_spec,
            pl.BlockSpec((None, None, None, nc, HEAD_DIM), lambda b, g, i: (0, b, g, 0, 0)),
            pl.BlockSpec((None, None, None, HEAD_DIM, nc), lambda b, g, i: (1, b, g, 0, 0)),
            pl.BlockSpec((SEL_LANES, nc), lambda b, g, i: (0, 0)),
        ] + [pl.BlockSpec((None, HPG, LANES, TQ), tab_map(qs, jt))
             for qs in range(CMP_QSUB) for jt in range(nj)],
        out_specs=(
            qt_spec,
            pl.BlockSpec((None, None, SEL_LANES, nq), lambda b, g, i: (b, g, 0, i)),
        ),
        compiler_params=_params(("arbitrary", "arbitrary", "arbitrary")),
        name="cmp_attn",
    )(q_t, kvc, kvc_t, ov_t, *([tables] * n_tabs))


class _FlashStream:
    def __init__(self, q_aug, k_ref, vt_ref, bias_fn, s_scr, p_scr):
        self.q_aug, self.k_ref, self.vt_ref, self.bias_fn = q_aug, k_ref, vt_ref, bias_fn
        self.s_scr, self.p_scr = s_scr, p_scr

    def _scores(self, t):
        k0 = pl.multiple_of(jnp.maximum(t, 0) * TK_SEL, TK_SEL)
        return self.bias_fn(t, _dot(self.k_ref[pl.ds(k0, TK_SEL), :], self.q_aug))

    def _pv(self, t, slot):
        k0 = pl.multiple_of(jnp.maximum(t, 0) * TK_SEL, TK_SEL)
        return _dot(self.vt_ref[:, pl.ds(k0, TK_SEL)], self.p_scr[slot])

    def _softmax(self, slot, m):
        s = self.s_scr[slot]
        m_new = jnp.maximum(m, jnp.max(s, axis=0, keepdims=True))
        self.p_scr[slot] = jnp.exp2(s - m_new).astype(self.p_scr.dtype)
        return m_new, jnp.exp2(m - m_new)

    def start(self, a):
        lanes = self.q_aug.shape[1]
        self.s_scr[0] = self._scores(a)
        self.p_scr[1] = jnp.zeros(self.p_scr.shape[1:], self.p_scr.dtype)
        return (jnp.full((1, lanes), NEG, F32), jnp.ones((1, lanes), F32), jnp.zeros((V_AUG, lanes), F32))

    def trip(self, a, carry, has_next):
        m, alpha, acc = carry
        acc = alpha * acc + self._pv(a - 1, 1)
        m, alpha = self._softmax(0, m)
        self.s_scr[1] = self._scores(a + 1)
        acc = alpha * acc + self._pv(a, 0)
        m, alpha = self._softmax(1, m)
        if has_next:
            self.s_scr[0] = self._scores(a + 2)
        return m, alpha, acc

    @staticmethod
    def _normalised(acc):
        return acc[:HEAD_DIM, :] / acc[HEAD_DIM:HEAD_DIM + 1, :]

    def finish_pair(self, a, carry):
        _, alpha, acc = carry
        return self._normalised(alpha * acc + self._pv(a + 1, 1))

    def finish_single(self, a, carry):
        m, alpha, acc = carry
        acc = alpha * acc + self._pv(a - 1, 1)
        _, alpha = self._softmax(0, m)
        return self._normalised(alpha * acc + self._pv(a, 0))


def _add_tiles(s, tile_fn):
    rows = []
    for part in range(TK_SEL // LANES):
        cols = [s[part * LANES:(part + 1) * LANES, h * TQ:(h + 1) * TQ] + tile_fn(part, h)
                for h in range(HPG)]
        rows.append(jnp.concatenate(cols, axis=1))
    return jnp.concatenate(rows, axis=0)


def _sel_win_kernel(qt_ref, ks_ref, vs_ref, kw_ref, vw_ref, unsel_ref, tab_ref, cvec_ref,
                    oct_ref, gates_ref, y_ref, ss_scr, ps_scr, sw_scr, pw_scr):
    i = pl.program_id(2)
    parts = TK_SEL // LANES

    c = cvec_ref[...]
    c_hi = c.astype(BF16).astype(F32)
    c_lo = c - c_hi
    arow = lax.broadcasted_iota(jnp.int32, (K_AUG - HEAD_DIM, HPG * TQ), 0)
    aug = jnp.where(arow == 0, c_hi, jnp.where(arow == 1, c_lo, 0.0)).astype(BF16)
    q_win = jnp.concatenate([qt_ref[...], aug], axis=0)
    unsel_t = unsel_ref[...]
    q_sel = jnp.concatenate([q_win, jnp.concatenate([unsel_t] * HPG, axis=1)], axis=0)

    def sel_bias(t, s):
        def tile(part, h):
            delta = i - parts * t - part
            row = jnp.where(delta < 0, SEL_MASKED, jnp.minimum(delta, SEL_CONST))
            return tab_ref[row, h]

        return _add_tiles(s, tile)

    def win_bias(t, s):
        def tile(part, h):
            delta = i - parts * t - part
            hidden = (delta < 0) | (delta >= N_WIN_NEAR) | (t < 0)
            return tab_ref[jnp.where(hidden, SEL_MASKED, WIN_BASE + delta), h]

        return _add_tiles(s, tile)

    n_tiles = i // parts + 1
    n_pairs = (n_tiles + 1) // 2
    sel = _FlashStream(q_sel, ks_ref, vs_ref, sel_bias, ss_scr, ps_scr)
    win = _FlashStream(q_win, kw_ref, vw_ref, win_bias, sw_scr, pw_scr)
    win_a0 = n_tiles - WIN_TILES
    last_pair = 2 * (n_pairs - 1)
    carry_s = sel.start(0)
    carry_w = win.start(win_a0)
    carry_s = lax.fori_loop(0, n_pairs - 1, lambda u, c: sel.trip(2 * u, c, has_next=True), carry_s)
    carry_s = sel.trip(last_pair, carry_s, has_next=False)
    carry_w = win.trip(win_a0, carry_w, has_next=True)
    os_t = sel.finish_pair(last_pair, carry_s)
    ow_t = win.finish_single(win_a0 + 2, carry_w)

    oc_t = oct_ref[...]
    gt = gates_ref[...]
    for hg in range(HPG):
        c0 = N_BRANCH * hg
        lanes = slice(hg * TQ, (hg + 1) * TQ)
        out_t = (gt[c0:c0 + 1, :] * oc_t[:, lanes] + gt[c0 + 1:c0 + 2, :] * os_t[:, lanes]
                 + gt[c0 + 2:c0 + 3, :] * ow_t[:, lanes])
        y_ref[:, hg * HEAD_DIM:(hg + 1) * HEAD_DIM] = jnp.transpose(out_t).astype(y_ref.dtype)


def _sel_win_attn(q_t, ks, vs_t, kw, vw_t, unsel, tables, cvec, oc_t, gates_t):
    nb = q_t.shape[0]
    s = q_t.shape[3] // HPG
    ks_spec = pl.BlockSpec((None, None, s, KS_AUG), lambda b, g, i: (b, g, 0, 0))
    kw_spec = pl.BlockSpec((None, None, s, K_AUG), lambda b, g, i: (b, g, 0, 0))
    v_spec = pl.BlockSpec((None, None, V_AUG, s), lambda b, g, i: (b, g, 0, 0))
    qt_spec = pl.BlockSpec((None, None, HEAD_DIM, HPG * TQ), lambda b, g, i: (b, g, 0, i))
    gw = HPG * HEAD_DIM
    return pl.pallas_call(
        _sel_win_kernel,
        out_shape=jax.ShapeDtypeStruct((nb, s, NSA_HEADS * HEAD_DIM), BF16),
        grid=(nb, KV_GROUPS, s // TQ),
        in_specs=[
            qt_spec,
            ks_spec, v_spec, kw_spec, v_spec,
            pl.BlockSpec((None, None, SEL_LANES, TQ), lambda b, g, i: (b, g, 0, i)),
            pl.BlockSpec((CMP_BASE, HPG, LANES, TQ), lambda b, g, i: (0, g, 0, 0)),
            pl.BlockSpec((None, 1, HPG * TQ), lambda b, g, i: (g, 0, 0)),
            qt_spec,
            pl.BlockSpec((None, None, GATE_ROWS, TQ), lambda b, g, i: (b, g, 0, i)),
        ],
        out_specs=pl.BlockSpec((None, TQ, gw), lambda b, g, i: (b, i, g)),
        scratch_shapes=[pltpu.VMEM((2, TK_SEL, HPG * TQ), F32), pltpu.VMEM((2, TK_SEL, HPG * TQ), BF16),
                        pltpu.VMEM((2, TK_SEL, HPG * TQ), F32), pltpu.VMEM((2, TK_SEL, HPG * TQ), BF16)],
        compiler_params=_params(("arbitrary", "arbitrary", "arbitrary")),
        name="sel_win_attn",
    )(q_t, ks, vs_t, kw, vw_t, unsel, tables, cvec, oc_t, gates_t)


def kernel(x, mem, mix_norm, a_w_in, a_conv_w, b_w_in, b_q_gain, kv_norm, w_kv_shared, k_gain,
           cmp_pos, cmp_w1, cmp_w2, rel_bias, mem_norm, mem_w_kv, mem_q_gain, mem_k_gain, w_out,
           mlp_norm, w_up, w_down):
    nb, s, d = x.shape
    depth = mix_norm.shape[0]
    n_a = a_w_in.shape[0]
    assert s % (LANES * CMP_STRIDE) == 0 and s // SEL_BLOCK <= SEL_LANES
    assert WIN_BASE + N_WIN_NEAR <= CMP_BASE
    assert WIN_TILES == 3

    mk, mv_t = _mem_kv(mem, mem_norm, mem_w_kv, mem_k_gain)
    tables = _bias_tables(rel_bias)
    cvec = jnp.repeat(rel_bias[REL_BUCKETS - 1].astype(F32) * LOG2E, TQ).reshape(KV_GROUPS, 1, HPG * TQ)
    w_out_bf = w_out.astype(BF16)
    shared = None
    for layer in range(depth):
        if layer < n_a:
            y_tok, y_mem = _mixer_a(x, mix_norm[layer], a_w_in[layer], a_conv_w[layer],
                                    mk[layer], mv_t[layer], mem_q_gain[layer])
        else:
            j = layer - n_a
            ks, vs_t, kw, vw_t, kvc, kvc_t = shared
            q_t, gates_t, y_mem = _b_inproj(x, mix_norm[layer], b_w_in[j], b_q_gain[j],
                                            mk[layer], mv_t[layer], mem_q_gain[layer])
            oc_t, unsel = _cmp_attn(q_t, kvc, kvc_t, tables)
            y_tok = _sel_win_attn(q_t, ks, vs_t, kw, vw_t, unsel, tables, cvec, oc_t, gates_t)
        x = _post(x.reshape(nb * s, d), y_tok.reshape(nb * s, -1), y_mem.reshape(nb * s, -1),
                  w_out_bf, mlp_norm[layer], w_up, w_down, layer).reshape(nb, s, d)
        if layer == n_a - 1:
            ucmp, ks, vs_t, kw, vw_t = _build_kv(x, kv_norm, w_kv_shared, k_gain)
            kvc, kvc_t = _compress(ucmp, cmp_pos, cmp_w1, cmp_w2, k_gain[0])
            shared = (ks, vs_t, kw, vw_t, kvc, kvc_t)
    return x
```

```python
import functools
import math

import numpy as np
import jax
import jax.numpy as jnp
from jax import lax
from jax.experimental import pallas as pl
from jax.experimental.pallas import tpu as pltpu

F32 = jnp.float32
BF16 = jnp.bfloat16

HEAD_DIM = 64
MEM_HEADS = 4
MEM_WIDTH = MEM_HEADS * HEAD_DIM
CONV_CH = 768
NSA_HEADS = 12
KV_GROUPS = 2
HPG = NSA_HEADS // KV_GROUPS
N_BRANCH = 3
CMP_LEN = 32
CMP_STRIDE = 16
SEL_BLOCK = 64
SEL_SHIFT = SEL_BLOCK.bit_length() - 1
SEL_TOPK = 16
N_LOCAL_SEL = 2
WINDOW = 512
REL_BUCKETS = 32
REL_MAX_DIST = 1024
EPS = 1e-6
NEG = -1e30
LOG2E = math.log2(math.e)
MASK_BIG = 2.0 ** 100

LANES = 128
SUBLANES = 8
VMEM_LIMIT_BYTES = 56 * 1024 * 1024

TQ = 128
TK_SEL = 256
SEL_LANES = 128
TM = 512
WIN_TILES = max(((TQ * i - (WINDOW - 1)) % TK_SEL + WINDOW + TQ - 2) // TK_SEL + 1 for i in range(TK_SEL))

N_SEL_NEAR = 8
SEL_CONST = N_SEL_NEAR
SEL_MASKED = N_SEL_NEAR + 1
WIN_BASE = N_SEL_NEAR + 2
N_WIN_NEAR = WINDOW // LANES + 1
CMP_BASE = 16
N_CMP_NEAR = 23
CMP_CONST = N_CMP_NEAR
CMP_MASKED = N_CMP_NEAR + 1
N_CMP_TILES = N_CMP_NEAR + 2
N_TILES = CMP_BASE + N_CMP_TILES


def _rms(xf, g):
    ms = jnp.mean(xf * xf, axis=-1, keepdims=True)
    return xf * lax.rsqrt(ms + EPS) * g


def _dot(a, b):
    return jnp.dot(a, b, preferred_element_type=F32)


def _params(sem):
    return pltpu.CompilerParams(dimension_semantics=sem, vmem_limit_bytes=VMEM_LIMIT_BYTES)


def _mem_kv_kernel(mem_ref, mnorm_ref, w_ref, kg_ref, mk_ref, mvt_ref, *, nb, ml):
    mh = _rms(mem_ref[...], mnorm_ref[...]).astype(BF16)
    kv = _dot(mh, w_ref[...])
    for h in range(MEM_HEADS):
        kh = _rms(kv[:, h * HEAD_DIM:(h + 1) * HEAD_DIM], kg_ref[...]).astype(BF16)
        vh = kv[:, MEM_WIDTH + h * HEAD_DIM:MEM_WIDTH + (h + 1) * HEAD_DIM]
        for b in range(nb):
            mk_ref[b, h] = kh[b * ml:(b + 1) * ml]
            mvt_ref[b, h] = jnp.transpose(vh[b * ml:(b + 1) * ml]).astype(BF16)


def _mem_kv(mem, mem_norm, mem_w_kv, mem_k_gain):
    nb, ml, d = mem.shape
    depth = mem_w_kv.shape[0]
    out = jax.ShapeDtypeStruct((depth, nb, MEM_HEADS, ml, HEAD_DIM), BF16)
    out_t = jax.ShapeDtypeStruct((depth, nb, MEM_HEADS, HEAD_DIM, ml), BF16)
    return pl.pallas_call(
        functools.partial(_mem_kv_kernel, nb=nb, ml=ml),
        out_shape=(out, out_t),
        grid=(depth,),
        in_specs=[
            pl.BlockSpec((nb * ml, d), lambda l: (0, 0)),
            pl.BlockSpec((1, d), lambda l: (0, 0)),
            pl.BlockSpec((None, d, 2 * MEM_WIDTH), lambda l: (l, 0, 0)),
            pl.BlockSpec((None, 1, HEAD_DIM), lambda l: (l, 0, 0)),
        ],
        out_specs=(
            pl.BlockSpec((None, nb, MEM_HEADS, ml, HEAD_DIM), lambda l: (l, 0, 0, 0, 0)),
            pl.BlockSpec((None, nb, MEM_HEADS, HEAD_DIM, ml), lambda l: (l, 0, 0, 0, 0)),
        ),
        compiler_params=_params(("arbitrary",)),
        name="mem_kv",
    )(mem.reshape(nb * ml, d), mem_norm.reshape(1, d), mem_w_kv.astype(BF16),
      mem_k_gain.reshape(depth, 1, HEAD_DIM))


def _rms_t(xt, g_col):
    ms = jnp.mean(xt * xt, axis=0, keepdims=True)
    return xt * lax.rsqrt(ms + EPS) * g_col


def _mem_attn(qm, mk_ref, mvt_ref, qg_col, ymem_ref):
    qm_t = jnp.transpose(qm)
    outs = []
    for h in range(MEM_HEADS):
        qh = _rms_t(qm_t[h * HEAD_DIM:(h + 1) * HEAD_DIM, :], qg_col) * (HEAD_DIM ** -0.5)
        lg = _dot(mk_ref[h], qh.astype(BF16))
        m = jnp.max(lg, axis=0, keepdims=True)
        e = jnp.exp(lg - m)
        l = jnp.sum(e, axis=0, keepdims=True)
        outs.append(_dot(mvt_ref[h], e.astype(BF16)) / l)
    ymem_ref[...] = jnp.transpose(jnp.concatenate(outs, axis=0)).astype(ymem_ref.dtype)


def _bucket_np(d):
    n = np.maximum(d, 0)
    max_exact = REL_BUCKETS // 2
    nf = np.maximum(n, 1).astype(np.float64)
    large = max_exact + (np.log(nf / max_exact) / math.log(REL_MAX_DIST / max_exact)
                         * (REL_BUCKETS - max_exact)).astype(np.int32)
    large = np.minimum(large, REL_BUCKETS - 1)
    return np.where(n < max_exact, n, large).astype(np.int32)


def _bucket_tiles():
    r = np.arange(TQ)[:, None]
    k = np.arange(LANES)[None, :]
    last = REL_BUCKETS - 1
    assert _bucket_np(np.array(N_SEL_NEAR * LANES - (LANES - 1))) == last
    assert _bucket_np(np.array(N_CMP_NEAR * LANES - CMP_STRIDE * (LANES - 1) - (CMP_LEN - 1))) == last
    tiles = []
    for delta in range(N_SEL_NEAR):
        d = delta * LANES + r - k
        tiles.append(np.where(d >= 0, _bucket_np(d), -1).T)
    tiles.append(np.full((TQ, LANES), REL_BUCKETS - 1))
    tiles.append(np.full((TQ, LANES), -1))
    for delta in range(N_WIN_NEAR):
        d = delta * LANES + r - k
        tiles.append(np.where((d >= 0) & (d < WINDOW), _bucket_np(d), -1).T)
    while len(tiles) < CMP_BASE:
        tiles.append(np.full((TQ, LANES), -1))
    for delta in range(N_CMP_NEAR):
        d = delta * LANES + r - CMP_STRIDE * k - (CMP_LEN - 1)
        tiles.append(np.where(d >= 0, _bucket_np(d), -1).T)
    tiles.append(np.full((TQ, LANES), REL_BUCKETS - 1))
    tiles.append(np.full((TQ, LANES), -1))
    return np.stack(tiles).astype(np.int32)


def _bias_tables_kernel(tab_ref, ids_ref, out_ref):
    relative = pl.program_id(0) < CMP_BASE
    n_bits = REL_BUCKETS.bit_length() - 1
    for h in range(NSA_HEADS):
        sub = jnp.where(relative, tab_ref[REL_BUCKETS - 1, h], 0.0)
        leaves = [(jnp.full((SUBLANES, LANES), tab_ref[b, h], F32) - sub) * LOG2E for b in range(REL_BUCKETS)]
        for r0 in range(0, ids_ref.shape[0], SUBLANES):
            ids = ids_ref[r0:r0 + SUBLANES, :]
            level = leaves
            for k in range(n_bits):
                bit = jnp.bitwise_and(ids, 1 << k) != 0
                level = [jnp.where(bit, level[2 * j + 1], level[2 * j]) for j in range(len(level) // 2)]
            out_ref[h, r0:r0 + SUBLANES, :] = jnp.where(ids < 0, NEG, level[0])


def _bias_tables(rel_bias):
    ids = jnp.asarray(_bucket_tiles())
    return pl.pallas_call(
        _bias_tables_kernel,
        out_shape=jax.ShapeDtypeStruct((N_TILES, NSA_HEADS, TQ, LANES), F32),
        grid=(N_TILES,),
        in_specs=[
            pl.BlockSpec(memory_space=pltpu.SMEM),
            pl.BlockSpec((None, TQ, LANES), lambda t: (t, 0, 0)),
        ],
        out_specs=pl.BlockSpec((None, NSA_HEADS, TQ, LANES), lambda t: (t, 0, 0, 0)),
        compiler_params=_params(("arbitrary",)),
        name="bias_tables",
    )(rel_bias.astype(F32), ids)


def _mixer_a_kernel(x_ref, g_ref, w_ref, cw_ref, mk_ref, mvt_ref, qg_ref,
                    ytok_ref, ymem_ref, carry_ref, *, tm):
    @pl.when(pl.program_id(1) == 0)
    def _():
        carry_ref[...] = jnp.zeros_like(carry_ref)

    h = _rms(x_ref[...], g_ref[...]).astype(BF16)
    z = _dot(h, w_ref[...])
    gate_b = z[:, :CONV_CH]
    v = z[:, CONV_CH:2 * CONV_CH] * z[:, 2 * CONV_CH:3 * CONV_CH]
    prev = carry_ref[...]
    last1 = prev[SUBLANES - 1:SUBLANES, :]
    last2 = prev[SUBLANES - 2:SUBLANES - 1, :]
    row = lax.broadcasted_iota(jnp.int32, (tm, CONV_CH), 0)
    v1 = jnp.where(row == 0, last1, pltpu.roll(v, 1, 0))
    v2 = jnp.where(row == 0, last2, jnp.where(row == 1, last1, pltpu.roll(v, 2, 0)))
    cw = cw_ref[...]
    y = gate_b * (cw[0:1, :] * v2 + cw[1:2, :] * v1 + cw[2:3, :] * v)
    carry_ref[...] = v[tm - SUBLANES:, :]
    ytok_ref[...] = y.astype(ytok_ref.dtype)
    _mem_attn(z[:, 3 * CONV_CH:], mk_ref, mvt_ref, qg_ref[...], ymem_ref)


def _mixer_a(x, norm_g, w_in, conv_w, mk, mv_t, mem_q_gain):
    nb, s, d = x.shape
    ml = mk.shape[2]
    n_in = w_in.shape[1]
    return pl.pallas_call(
        functools.partial(_mixer_a_kernel, tm=TM),
        out_shape=(jax.ShapeDtypeStruct((nb, s, CONV_CH), BF16),
                   jax.ShapeDtypeStruct((nb, s, MEM_WIDTH), BF16)),
        grid=(nb, s // TM),
        in_specs=[
            pl.BlockSpec((None, TM, d), lambda b, i: (b, i, 0)),
            pl.BlockSpec((1, d), lambda b, i: (0, 0)),
            pl.BlockSpec((d, n_in), lambda b, i: (0, 0)),
            pl.BlockSpec((3, CONV_CH), lambda b, i: (0, 0)),
            pl.BlockSpec((None, MEM_HEADS, ml, HEAD_DIM), lambda b, i: (b, 0, 0, 0)),
            pl.BlockSpec((None, MEM_HEADS, HEAD_DIM, ml), lambda b, i: (b, 0, 0, 0)),
            pl.BlockSpec((HEAD_DIM, 1), lambda b, i: (0, 0)),
        ],
        out_specs=(
            pl.BlockSpec((None, TM, CONV_CH), lambda b, i: (b, i, 0)),
            pl.BlockSpec((None, TM, MEM_WIDTH), lambda b, i: (b, i, 0)),
        ),
        scratch_shapes=[pltpu.VMEM((SUBLANES, CONV_CH), F32)],
        compiler_params=_params(("arbitrary", "arbitrary")),
        name="mixer_a",
    )(x, norm_g.reshape(1, d), w_in.astype(BF16), conv_w, mk, mv_t, mem_q_gain.reshape(HEAD_DIM, 1))


B_GATE0 = NSA_HEADS * HEAD_DIM
B_QMEM0 = B_GATE0 + KV_GROUPS * LANES
B_WIDTH = B_QMEM0 + MEM_WIDTH


GATE_ROWS = 32


def _b_inproj_kernel(x_ref, g_ref, w_ref, qg_ref, mk_ref, mvt_ref, mqg_ref,
                     qt_ref, gates_ref, ymem_ref):
    h = _rms(x_ref[...], g_ref[...]).astype(BF16)
    z = _dot(h, w_ref[...])
    qg = qg_ref[...]
    tm = z.shape[0]
    for pair in range(NSA_HEADS // 2):
        z_t = jnp.transpose(z[:, pair * LANES:(pair + 1) * LANES])
        for half in range(2):
            qh_t = _rms_t(z_t[half * HEAD_DIM:(half + 1) * HEAD_DIM, :], qg) * (HEAD_DIM ** -0.5 * LOG2E)
            qh_t = qh_t.astype(qt_ref.dtype)
            g, hg = divmod(2 * pair + half, HPG)
            for qt in range(tm // TQ):
                lane0 = (qt * HPG + hg) * TQ
                qt_ref[g, :, lane0:lane0 + TQ] = qh_t[:, qt * TQ:(qt + 1) * TQ]
    for g in range(KV_GROUPS):
        gl = z[:, B_GATE0 + g * LANES:B_GATE0 + (g + 1) * LANES]
        gates_ref[g] = jnp.transpose(1.0 / (1.0 + jnp.exp(-gl)))[:GATE_ROWS, :]
    _mem_attn(z[:, B_QMEM0:], mk_ref, mvt_ref, mqg_ref[...], ymem_ref)


def _b_inproj(x, norm_g, w_in, q_gain, mk, mv_t, mem_q_gain):
    nb, s, d = x.shape
    ml = mk.shape[2]
    nq = NSA_HEADS * HEAD_DIM
    ng = HPG * N_BRANCH
    pad = jnp.zeros((d, LANES - ng), w_in.dtype)
    w = jnp.concatenate([w_in[:, :nq], w_in[:, nq:nq + ng], pad,
                         w_in[:, nq + ng:nq + 2 * ng], pad, w_in[:, nq + 2 * ng:]], axis=1)
    return pl.pallas_call(
        _b_inproj_kernel,
        out_shape=(jax.ShapeDtypeStruct((nb, KV_GROUPS, HEAD_DIM, HPG * s), BF16),
                   jax.ShapeDtypeStruct((nb, KV_GROUPS, GATE_ROWS, s), F32),
                   jax.ShapeDtypeStruct((nb, s, MEM_WIDTH), BF16)),
        grid=(nb, s // TM),
        in_specs=[
            pl.BlockSpec((None, TM, d), lambda b, i: (b, i, 0)),
            pl.BlockSpec((1, d), lambda b, i: (0, 0)),
            pl.BlockSpec((d, B_WIDTH), lambda b, i: (0, 0)),
            pl.BlockSpec((HEAD_DIM, 1), lambda b, i: (0, 0)),
            pl.BlockSpec((None, MEM_HEADS, ml, HEAD_DIM), lambda b, i: (b, 0, 0, 0)),
            pl.BlockSpec((None, MEM_HEADS, HEAD_DIM, ml), lambda b, i: (b, 0, 0, 0)),
            pl.BlockSpec((HEAD_DIM, 1), lambda b, i: (0, 0)),
        ],
        out_specs=(
            pl.BlockSpec((None, KV_GROUPS, HEAD_DIM, HPG * TM), lambda b, i: (b, 0, 0, i)),
            pl.BlockSpec((None, KV_GROUPS, GATE_ROWS, TM), lambda b, i: (b, 0, 0, i)),
            pl.BlockSpec((None, TM, MEM_WIDTH), lambda b, i: (b, i, 0)),
        ),
        compiler_params=_params(("arbitrary", "arbitrary")),
        name="b_inproj",
    )(x, norm_g.reshape(1, d), w.astype(BF16), q_gain.reshape(HEAD_DIM, 1), mk, mv_t,
      mem_q_gain.reshape(HEAD_DIM, 1))


FF_CHUNK = 1024
TM_POST = 1024


def _post_kernel(x_ref, ytok_ref, ymem_ref, woa_ref, wob_ref, g_ref, wup_ref, wdn_ref, o_ref, hm_ref):
    @pl.when(pl.program_id(1) == 0)
    def _():
        x1 = x_ref[...] + _dot(ytok_ref[...], woa_ref[...]) + _dot(ymem_ref[...], wob_ref[...])
        o_ref[...] = x1
        hm_ref[...] = _rms(x1, g_ref[...]).astype(hm_ref.dtype)

    a = _dot(hm_ref[...], wup_ref[...].astype(BF16))
    a = jnp.square(jnp.maximum(a, 0.0)).astype(BF16)
    o_ref[...] += _dot(a, wdn_ref[...].astype(BF16))


def _post(x, y_tok, y_mem, w_out, norm_g, w_up, w_down, layer):
    n, d = x.shape
    ff = w_up.shape[2]
    nt = y_tok.shape[1]
    assert nt % MEM_WIDTH == 0
    tm = min(TM_POST, n)
    const = lambda i, c: (0, 0)
    rows = lambda i, c: (i, 0)
    return pl.pallas_call(
        _post_kernel,
        out_shape=jax.ShapeDtypeStruct((n, d), F32),
        grid=(n // tm, ff // FF_CHUNK),
        in_specs=[
            pl.BlockSpec((tm, d), rows),
            pl.BlockSpec((tm, nt), rows),
            pl.BlockSpec((tm, MEM_WIDTH), rows),
            pl.BlockSpec((None, nt, d), lambda i, c: (layer, 0, 0)),
            pl.BlockSpec((None, MEM_WIDTH, d), lambda i, c: (layer, nt // MEM_WIDTH, 0)),
            pl.BlockSpec((1, d), const),
            pl.BlockSpec((None, d, FF_CHUNK), lambda i, c: (layer, 0, c)),
            pl.BlockSpec((None, FF_CHUNK, d), lambda i, c: (layer, c, 0)),
        ],
        out_specs=pl.BlockSpec((tm, d), rows),
        scratch_shapes=[pltpu.VMEM((tm, d), BF16)],
        compiler_params=_params(("arbitrary", "arbitrary")),
        name="post",
    )(x, y_tok, y_mem, w_out, w_out, norm_g.reshape(1, d), w_up, w_down)


K_AUG = 2 * HEAD_DIM
KS_AUG = K_AUG + SEL_LANES
V_AUG = HEAD_DIM + 16


def _build_kv_kernel(x_ref, g_ref, w_ref, kg_ref, ucmp_ref, ks_ref, vs_ref, kw_ref, vw_ref, raw_scr):
    h = _rms(x_ref[...], g_ref[...]).astype(BF16)
    kv = _dot(h, w_ref[...])
    gw = KV_GROUPS * HEAD_DIM
    tm = kv.shape[0]
    for ty in range(2):
        raw_scr[ty] = kv[:, ty * gw:(ty + 1) * gw]
        for t in range(CMP_STRIDE):
            rows = raw_scr[ty, pl.ds(t, tm // CMP_STRIDE, stride=CMP_STRIDE), :]
            for g in range(KV_GROUPS):
                ucmp_ref[ty, g, :, t * HEAD_DIM:(t + 1) * HEAD_DIM] = (
                    rows[:, g * HEAD_DIM:(g + 1) * HEAD_DIM].astype(ucmp_ref.dtype))
    kg = kg_ref[...]
    ones = (lax.broadcasted_iota(jnp.int32, (tm, K_AUG - HEAD_DIM), 1) < 2).astype(ks_ref.dtype)
    pos = pl.program_id(1) * tm + lax.broadcasted_iota(jnp.int32, (tm, SEL_LANES), 0)
    blk = lax.broadcasted_iota(jnp.int32, (tm, SEL_LANES), 1)
    onehot = jnp.where(blk == jnp.right_shift(pos, SEL_SHIFT), MASK_BIG, 0.0).astype(ks_ref.dtype)
    ones_row = (lax.broadcasted_iota(jnp.int32, (V_AUG - HEAD_DIM, tm), 0) == 0).astype(vs_ref.dtype)
    for g in range(KV_GROUPS):
        sl = lambda t: kv[:, t * gw + g * HEAD_DIM:t * gw + (g + 1) * HEAD_DIM]
        ks_ref[g, :, :HEAD_DIM] = _rms(sl(2), kg[1:2, :]).astype(ks_ref.dtype)
        ks_ref[g, :, HEAD_DIM:K_AUG] = ones
        ks_ref[g, :, K_AUG:] = onehot
        vs_ref[g, :HEAD_DIM, :] = jnp.transpose(sl(3)).astype(vs_ref.dtype)
        vs_ref[g, HEAD_DIM:, :] = ones_row
        kw_ref[g, :, :HEAD_DIM] = _rms(sl(4), kg[2:3, :]).astype(kw_ref.dtype)
        kw_ref[g, :, HEAD_DIM:] = ones
        vw_ref[g, :HEAD_DIM, :] = jnp.transpose(sl(5)).astype(vw_ref.dtype)
        vw_ref[g, HEAD_DIM:, :] = ones_row


def _build_kv(x, kv_norm, w_kv, k_gain):
    nb, s, d = x.shape
    gw = KV_GROUPS * HEAD_DIM
    ks = jax.ShapeDtypeStruct((nb, KV_GROUPS, s, KS_AUG), BF16)
    kw = jax.ShapeDtypeStruct((nb, KV_GROUPS, s, K_AUG), BF16)
    vt = jax.ShapeDtypeStruct((nb, KV_GROUPS, V_AUG, s), BF16)
    ks_spec = pl.BlockSpec((None, KV_GROUPS, TM, KS_AUG), lambda b, i: (b, 0, i, 0))
    kw_spec = pl.BlockSpec((None, KV_GROUPS, TM, K_AUG), lambda b, i: (b, 0, i, 0))
    v_spec = pl.BlockSpec((None, KV_GROUPS, V_AUG, TM), lambda b, i: (b, 0, 0, i))
    return pl.pallas_call(
        _build_kv_kernel,
        out_shape=(jax.ShapeDtypeStruct((2, nb, KV_GROUPS, s // CMP_STRIDE, CMP_STRIDE * HEAD_DIM), BF16),
                   ks, vt, kw, vt),
        grid=(nb, s // TM),
        in_specs=[
            pl.BlockSpec((None, TM, d), lambda b, i: (b, i, 0)),
            pl.BlockSpec((1, d), lambda b, i: (0, 0)),
            pl.BlockSpec((d, 6 * gw), lambda b, i: (0, 0)),
            pl.BlockSpec((N_BRANCH, HEAD_DIM), lambda b, i: (0, 0)),
        ],
        out_specs=(pl.BlockSpec((2, None, KV_GROUPS, TM // CMP_STRIDE, CMP_STRIDE * HEAD_DIM),
                                lambda b, i: (0, b, 0, i, 0)),
                   ks_spec, v_spec, kw_spec, v_spec),
        scratch_shapes=[pltpu.VMEM((2, TM, gw), F32)],
        compiler_params=_params(("arbitrary", "arbitrary")),
        name="build_kv",
    )(x, kv_norm.reshape(1, d), w_kv.astype(BF16), k_gain)


def _compress_kernel(u_ref, w1_ref, w2_ref, pos_ref, kg_ref, out_ref, out_t_ref, *, nc):
    half = CMP_STRIDE * HEAD_DIM
    u = u_ref[...]
    lo = _dot(u, w1_ref[:half, :])
    hi = _dot(u, w1_ref[half:, :])
    posb = _dot(jnp.broadcast_to(pos_ref[...], (8, 2 * half)).astype(BF16), w1_ref[...])[0:1, :]
    h1 = lo + pltpu.roll(hi, nc - 1, 0) + posb
    hid = 0.5 * h1 * (1.0 + jnp.tanh(math.sqrt(2.0 / math.pi) * (h1 + 0.044715 * (h1 * h1 * h1))))
    c = _dot(hid.astype(BF16), w2_ref[...])
    is_key = pl.program_id(0) == 0
    out = jnp.where(is_key, _rms(c, kg_ref[...]), c)
    out_ref[...] = out.astype(out_ref.dtype)
    out_t_ref[...] = jnp.transpose(out).astype(out_t_ref.dtype)


def _compress(u, cmp_pos, cmp_w1, cmp_w2, k_gain0):
    _, nb, _, nc, width = u.shape
    hidden = cmp_w1.shape[2]
    return pl.pallas_call(
        functools.partial(_compress_kernel, nc=nc),
        out_shape=(jax.ShapeDtypeStruct((2, nb, KV_GROUPS, nc, HEAD_DIM), BF16),
                   jax.ShapeDtypeStruct((2, nb, KV_GROUPS, HEAD_DIM, nc), BF16)),
        grid=(2, nb, KV_GROUPS),
        in_specs=[
            pl.BlockSpec((None, None, None, nc, width), lambda t, b, g: (t, b, g, 0, 0)),
            pl.BlockSpec((None, 2 * width, hidden), lambda t, b, g: (t, 0, 0)),
            pl.BlockSpec((None, hidden, HEAD_DIM), lambda t, b, g: (t, 0, 0)),
            pl.BlockSpec((None, 1, 2 * width), lambda t, b, g: (t, 0, 0)),
            pl.BlockSpec((1, HEAD_DIM), lambda t, b, g: (0, 0)),
        ],
        out_specs=(pl.BlockSpec((None, None, None, nc, HEAD_DIM), lambda t, b, g: (t, b, g, 0, 0)),
                   pl.BlockSpec((None, None, None, HEAD_DIM, nc), lambda t, b, g: (t, b, g, 0, 0))),
        compiler_params=_params(("arbitrary", "arbitrary", "arbitrary")),
        name="compress",
    )(u, cmp_w1.astype(BF16), cmp_w2.astype(BF16), cmp_pos.reshape(2, 1, 2 * width),
      k_gain0.reshape(1, HEAD_DIM))


def _overlap_np(nc, n_sel):
    c0 = np.arange(nc)[:, None] * CMP_STRIDE
    s0 = np.arange(SEL_LANES)[None, :] * SEL_BLOCK
    ov = np.minimum(c0 + CMP_LEN, s0 + SEL_BLOCK) - np.maximum(c0, s0)
    ov = np.clip(ov, 0, None).astype(np.float32) / CMP_LEN
    ov[nc - 1:, :] = 0.0
    ov[:, n_sel:] = 0.0
    return ov


def _select_topk(score, ntop):
    blk = lax.broadcasted_iota(jnp.int32, score.shape, 0).astype(F32)
    for _ in range(ntop):
        m = jnp.max(score, axis=0, keepdims=True)
        first = jnp.min(jnp.where(score == m, blk, float(SEL_LANES)), axis=0, keepdims=True)
        score = jnp.where(blk == first, -jnp.inf, score)
    return score == -jnp.inf


CMP_QSUB = 2
TOPK_GROUP = 32


def _cmp_attn_kernel(*refs, nj, n_sel):
    qt_ref, kc_ref, vct_ref, ovt_ref = refs[:4]
    tab_refs = refs[4:4 + nj * CMP_QSUB]
    oct_ref, unsel_ref = refs[4 + nj * CMP_QSUB:]
    i = pl.program_id(2)
    nq = CMP_QSUB * TQ
    def attend(nk):
        rows = nk * LANES
        s = _dot(kc_ref[:rows, :], qt_ref[...])
        s = jnp.concatenate(
            [jnp.concatenate([s[jt * LANES:(jt + 1) * LANES, (qs * HPG + h) * TQ:(qs * HPG + h + 1) * TQ]
                              + tab_refs[qs * nj + jt][h]
                              for qs in range(CMP_QSUB) for h in range(HPG)], axis=1) for jt in range(nk)], axis=0)
        m = jnp.maximum(jnp.max(s, axis=0, keepdims=True), 0.5 * NEG)
        e = jnp.exp2(s - m)
        l = jnp.sum(e, axis=0, keepdims=True)
        p = e * (1.0 / jnp.where(l == 0.0, 1.0, l))
        oct_ref[...] = _dot(vct_ref[:, :rows], p.astype(BF16))

        psums = []
        for qs in range(CMP_QSUB):
            acc = p[:, qs * HPG * TQ:(qs * HPG + 1) * TQ]
            for h in range(1, HPG):
                acc = acc + p[:, (qs * HPG + h) * TQ:(qs * HPG + h + 1) * TQ]
            psums.append(acc)
        psum = jnp.concatenate(psums, axis=1)
        p_hi = psum.astype(BF16)
        p_lo = (psum - p_hi.astype(F32)).astype(BF16)
        return _dot(ovt_ref[:, :rows], p_hi) + _dot(ovt_ref[:, :rows], p_lo)

    last_visible = (nq * (i + 1) - CMP_LEN) // CMP_STRIDE
    imp_t = lax.switch(last_visible // LANES, [functools.partial(attend, nk) for nk in range(1, nj + 1)])

    tq = i * nq + lax.broadcasted_iota(jnp.int32, (SEL_LANES, nq), 1)
    j = lax.broadcasted_iota(jnp.int32, (SEL_LANES, nq), 0)
    back = jnp.right_shift(tq, SEL_SHIFT) - j
    forced = (j == 0) | ((back >= 0) & (back < N_LOCAL_SEL))
    causal = back >= 0
    n_free = min(SEL_TOPK, n_sel) - (N_LOCAL_SEL + 1)
    score = jnp.where(causal & ~forced, imp_t, NEG)

    def pick(n_groups):
        rows = n_groups * TOPK_GROUP
        top = _select_topk(score[:rows, :], n_free).astype(F32)
        if rows == SEL_LANES:
            return top
        return jnp.concatenate([top, jnp.zeros((SEL_LANES - rows, nq), F32)], axis=0)

    causal_blocks = (nq * (i + 1)) // SEL_BLOCK
    picked = lax.switch((causal_blocks - 1) // TOPK_GROUP,
                        [functools.partial(pick, n) for n in range(1, SEL_LANES // TOPK_GROUP + 1)])
    sel = causal & (forced | (picked > 0.0))
    unsel_ref[...] = jnp.where(sel, 0.0, -1.0).astype(unsel_ref.dtype)


def _cmp_attn(q_t, kvc, kvc_t, tables):
    nb = q_t.shape[0]
    s = q_t.shape[3] // HPG
    nc = s // CMP_STRIDE
    nj = nc // LANES
    n_sel = s // SEL_BLOCK
    ov_t = jnp.asarray(_overlap_np(nc, n_sel).T, dtype=BF16)

    def tab_map(qs, jt):
        def index_map(b, g, i):
            didx = CMP_QSUB * i + qs - (LANES * CMP_STRIDE // TQ) * jt
            row = jnp.where(didx < 0, CMP_MASKED, jnp.minimum(didx, CMP_CONST))
            return (CMP_BASE + row, g, 0, 0)
        return index_map

    nq = CMP_QSUB * TQ
    qt_spec = pl.BlockSpec((None, None, HEAD_DIM, HPG * nq), lambda b, g, i: (b, g, 0, i))
    n_tabs = CMP_QSUB * nj
    return pl.pallas_call(
        functools.partial(_cmp_attn_kernel, nj=nj, n_sel=n_sel),
        out_shape=(jax.ShapeDtypeStruct((nb, KV_GROUPS, HEAD_DIM, HPG * s), F32),
                   jax.ShapeDtypeStruct((nb, KV_GROUPS, SEL_LANES, s), BF16)),
        grid=(nb, KV_GROUPS, s // nq),
        in_specs=[
            qt_spec,
            pl.BlockSpec((None, None, None, nc, HEAD_DIM), lambda b, g, i: (0, b, g, 0, 0)),
            pl.BlockSpec((None, None, None, HEAD_DIM, nc), lambda b, g, i: (1, b, g, 0, 0)),
            pl.BlockSpec((SEL_LANES, nc), lambda b, g, i: (0, 0)),
        ] + [pl.BlockSpec((None, HPG, LANES, TQ), tab_map(qs, jt))
             for qs in range(CMP_QSUB) for jt in range(nj)],
        out_specs=(
            qt_spec,
            pl.BlockSpec((None, None, SEL_LANES, nq), lambda b, g, i: (b, g, 0, i)),
        ),
        compiler_params=_params(("arbitrary", "arbitrary", "arbitrary")),
        name="cmp_attn",
    )(q_t, kvc, kvc_t, ov_t, *([tables] * n_tabs))


class _FlashStream:
    def __init__(self, q_aug, k_ref, vt_ref, bias_fn, s_scr, p_scr):
        self.q_aug, self.k_ref, self.vt_ref, self.bias_fn = q_aug, k_ref, vt_ref, bias_fn
        self.s_scr, self.p_scr = s_scr, p_scr

    def _scores(self, t):
        k0 = pl.multiple_of(jnp.maximum(t, 0) * TK_SEL, TK_SEL)
        return self.bias_fn(t, _dot(self.k_ref[pl.ds(k0, TK_SEL), :], self.q_aug))

    def _pv(self, t, slot):
        k0 = pl.multiple_of(jnp.maximum(t, 0) * TK_SEL, TK_SEL)
        return _dot(self.vt_ref[:, pl.ds(k0, TK_SEL)], self.p_scr[slot])

    def _softmax(self, slot, m):
        s = self.s_scr[slot]
        m_new = jnp.maximum(m, jnp.max(s, axis=0, keepdims=True))
        self.p_scr[slot] = jnp.exp2(s - m_new).astype(self.p_scr.dtype)
        return m_new, jnp.exp2(m - m_new)

    def start(self, a):
        lanes = self.q_aug.shape[1]
        self.s_scr[0] = self._scores(a)
        self.p_scr[1] = jnp.zeros(self.p_scr.shape[1:], self.p_scr.dtype)
        return (jnp.full((1, lanes), NEG, F32), jnp.ones((1, lanes), F32), jnp.zeros((V_AUG, lanes), F32))

    def trip(self, a, carry, has_next):
        m, alpha, acc = carry
        acc = alpha * acc + self._pv(a - 1, 1)
        m, alpha = self._softmax(0, m)
        self.s_scr[1] = self._scores(a + 1)
        acc = alpha * acc + self._pv(a, 0)
        m, alpha = self._softmax(1, m)
        if has_next:
            self.s_scr[0] = self._scores(a + 2)
        return m, alpha, acc

    @staticmethod
    def _normalised(acc):
        return acc[:HEAD_DIM, :] / acc[HEAD_DIM:HEAD_DIM + 1, :]

    def finish_pair(self, a, carry):
        _, alpha, acc = carry
        return self._normalised(alpha * acc + self._pv(a + 1, 1))

    def finish_single(self, a, carry):
        m, alpha, acc = carry
        acc = alpha * acc + self._pv(a - 1, 1)
        _, alpha = self._softmax(0, m)
        return self._normalised(alpha * acc + self._pv(a, 0))


def _add_tiles(s, tile_fn):
    rows = []
    for part in range(TK_SEL // LANES):
        cols = [s[part * LANES:(part + 1) * LANES, h * TQ:(h + 1) * TQ] + tile_fn(part, h)
                for h in range(HPG)]
        rows.append(jnp.concatenate(cols, axis=1))
    return jnp.concatenate(rows, axis=0)


def _sel_win_kernel(qt_ref, ks_ref, vs_ref, kw_ref, vw_ref, unsel_ref, tab_ref, cvec_ref,
                    oct_ref, gates_ref, y_ref, ss_scr, ps_scr, sw_scr, pw_scr):
    i = pl.program_id(2)
    parts = TK_SEL // LANES

    c = cvec_ref[...]
    c_hi = c.astype(BF16).astype(F32)
    c_lo = c - c_hi
    arow = lax.broadcasted_iota(jnp.int32, (K_AUG - HEAD_DIM, HPG * TQ), 0)
    aug = jnp.where(arow == 0, c_hi, jnp.where(arow == 1, c_lo, 0.0)).astype(BF16)
    q_win = jnp.concatenate([qt_ref[...], aug], axis=0)
    unsel_t = unsel_ref[...]
    q_sel = jnp.concatenate([q_win, jnp.concatenate([unsel_t] * HPG, axis=1)], axis=0)

    def sel_bias(t, s):
        def tile(part, h):
            delta = i - parts * t - part
            row = jnp.where(delta < 0, SEL_MASKED, jnp.minimum(delta, SEL_CONST))
            return tab_ref[row, h]

        return _add_tiles(s, tile)

    def win_bias(t, s):
        def tile(part, h):
            delta = i - parts * t - part
            hidden = (delta < 0) | (delta >= N_WIN_NEAR) | (t < 0)
            return tab_ref[jnp.where(hidden, SEL_MASKED, WIN_BASE + delta), h]

        return _add_tiles(s, tile)

    n_tiles = i // parts + 1
    n_pairs = (n_tiles + 1) // 2
    sel = _FlashStream(q_sel, ks_ref, vs_ref, sel_bias, ss_scr, ps_scr)
    win = _FlashStream(q_win, kw_ref, vw_ref, win_bias, sw_scr, pw_scr)
    win_a0 = n_tiles - WIN_TILES
    last_pair = 2 * (n_pairs - 1)
    carry_s = sel.start(0)
    carry_w = win.start(win_a0)
    carry_s = lax.fori_loop(0, n_pairs - 1, lambda u, c: sel.trip(2 * u, c, has_next=True), carry_s)
    carry_s = sel.trip(last_pair, carry_s, has_next=False)
    carry_w = win.trip(win_a0, carry_w, has_next=True)
    os_t = sel.finish_pair(last_pair, carry_s)
    ow_t = win.finish_single(win_a0 + WIN_TILES - 1, carry_w)

    oc_t = oct_ref[...]
    gt = gates_ref[...]
    for hg in range(HPG):
        c0 = N_BRANCH * hg
        lanes = slice(hg * TQ, (hg + 1) * TQ)
        out_t = (gt[c0:c0 + 1, :] * oc_t[:, lanes] + gt[c0 + 1:c0 + 2, :] * os_t[:, lanes]
                 + gt[c0 + 2:c0 + 3, :] * ow_t[:, lanes])
        y_ref[:, hg * HEAD_DIM:(hg + 1) * HEAD_DIM] = jnp.transpose(out_t).astype(y_ref.dtype)


def _sel_win_attn(q_t, ks, vs_t, kw, vw_t, unsel, tables, cvec, oc_t, gates_t):
    nb = q_t.shape[0]
    s = q_t.shape[3] // HPG
    ks_spec = pl.BlockSpec((None, None, s, KS_AUG), lambda b, g, i: (b, g, 0, 0))
    kw_spec = pl.BlockSpec((None, None, s, K_AUG), lambda b, g, i: (b, g, 0, 0))
    v_spec = pl.BlockSpec((None, None, V_AUG, s), lambda b, g, i: (b, g, 0, 0))
    qt_spec = pl.BlockSpec((None, None, HEAD_DIM, HPG * TQ), lambda b, g, i: (b, g, 0, i))
    gw = HPG * HEAD_DIM
    return pl.pallas_call(
        _sel_win_kernel,
        out_shape=jax.ShapeDtypeStruct((nb, s, NSA_HEADS * HEAD_DIM), BF16),
        grid=(nb, KV_GROUPS, s // TQ),
        in_specs=[
            qt_spec,
            ks_spec, v_spec, kw_spec, v_spec,
            pl.BlockSpec((None, None, SEL_LANES, TQ), lambda b, g, i: (b, g, 0, i)),
            pl.BlockSpec((CMP_BASE, HPG, LANES, TQ), lambda b, g, i: (0, g, 0, 0)),
            pl.BlockSpec((None, 1, HPG * TQ), lambda b, g, i: (g, 0, 0)),
            qt_spec,
            pl.BlockSpec((None, None, GATE_ROWS, TQ), lambda b, g, i: (b, g, 0, i)),
        ],
        out_specs=pl.BlockSpec((None, TQ, gw), lambda b, g, i: (b, i, g)),
        scratch_shapes=[pltpu.VMEM((2, TK_SEL, HPG * TQ), F32), pltpu.VMEM((2, TK_SEL, HPG * TQ), BF16),
                        pltpu.VMEM((2, TK_SEL, HPG * TQ), F32), pltpu.VMEM((2, TK_SEL, HPG * TQ), BF16)],
        compiler_params=_params(("arbitrary", "arbitrary", "arbitrary")),
        name="sel_win_attn",
    )(q_t, ks, vs_t, kw, vw_t, unsel, tables, cvec, oc_t, gates_t)


def kernel(x, mem, mix_norm, a_w_in, a_conv_w, b_w_in, b_q_gain, kv_norm, w_kv_shared, k_gain,
           cmp_pos, cmp_w1, cmp_w2, rel_bias, mem_norm, mem_w_kv, mem_q_gain, mem_k_gain, w_out,
           mlp_norm, w_up, w_down):
    nb, s, d = x.shape
    depth = mix_norm.shape[0]
    n_a = a_w_in.shape[0]
    assert s % (LANES * CMP_STRIDE) == 0 and s // SEL_BLOCK <= SEL_LANES
    assert WIN_BASE + N_WIN_NEAR <= CMP_BASE
    assert WIN_TILES == 3

    mk, mv_t = _mem_kv(mem, mem_norm, mem_w_kv, mem_k_gain)
    tables = _bias_tables(rel_bias)
    cvec = jnp.repeat(rel_bias[REL_BUCKETS - 1].astype(F32) * LOG2E, TQ).reshape(KV_GROUPS, 1, HPG * TQ)
    w_out_bf = w_out.astype(BF16)
    shared = None
    for layer in range(depth):
        if layer < n_a:
            y_tok, y_mem = _mixer_a(x, mix_norm[layer], a_w_in[layer], a_conv_w[layer],
                                    mk[layer], mv_t[layer], mem_q_gain[layer])
        else:
            j = layer - n_a
            ks, vs_t, kw, vw_t, kvc, kvc_t = shared
            q_t, gates_t, y_mem = _b_inproj(x, mix_norm[layer], b_w_in[j], b_q_gain[j],
                                            mk[layer], mv_t[layer], mem_q_gain[layer])
            oc_t, unsel = _cmp_attn(q_t, kvc, kvc_t, tables)
            y_tok = _sel_win_attn(q_t, ks, vs_t, kw, vw_t, unsel, tables, cvec, oc_t, gates_t)
        x = _post(x.reshape(nb * s, d), y_tok.reshape(nb * s, -1), y_mem.reshape(nb * s, -1),
                  w_out_bf, mlp_norm[layer], w_up, w_down, layer).reshape(nb, s, d)
        if layer == n_a - 1:
            ucmp, ks, vs_t, kw, vw_t = _build_kv(x, kv_norm, w_kv_shared, k_gain)
            kvc, kvc_t = _compress(ucmp, cmp_pos, cmp_w1, cmp_w2, k_gain[0])
            shared = (ks, vs_t, kw, vw_t, kvc, kvc_t)
    return x
```

```python
import functools
import math

import numpy as np
import jax
import jax.numpy as jnp
from jax import lax
from jax.experimental import pallas as pl
from jax.experimental.pallas import tpu as pltpu

F32 = jnp.float32
BF16 = jnp.bfloat16

HEAD_DIM = 64
MEM_HEADS = 4
MEM_WIDTH = MEM_HEADS * HEAD_DIM
CONV_CH = 768
NSA_HEADS = 12
KV_GROUPS = 2
HPG = NSA_HEADS // KV_GROUPS
N_BRANCH = 3
CMP_LEN = 32
CMP_STRIDE = 16
SEL_BLOCK = 64
SEL_SHIFT = SEL_BLOCK.bit_length() - 1
SEL_TOPK = 16
N_LOCAL_SEL = 2
WINDOW = 512
REL_BUCKETS = 32
REL_MAX_DIST = 1024
EPS = 1e-6
NEG = -1e30
LOG2E = math.log2(math.e)
MASK_BIG = 2.0 ** 100

LANES = 128
SUBLANES = 8
VMEM_LIMIT_BYTES = 56 * 1024 * 1024

TQ = 128
TK_SEL = 256
SEL_LANES = 128
TM = 512
WIN_TILES = max(((TQ * i - (WINDOW - 1)) % TK_SEL + WINDOW + TQ - 2) // TK_SEL + 1 for i in range(TK_SEL))

N_SEL_NEAR = 8
SEL_CONST = N_SEL_NEAR
SEL_MASKED = N_SEL_NEAR + 1
WIN_BASE = N_SEL_NEAR + 2
N_WIN_NEAR = WINDOW // LANES + 1
CMP_BASE = 16
N_CMP_NEAR = 23
CMP_CONST = N_CMP_NEAR
CMP_MASKED = N_CMP_NEAR + 1
N_CMP_TILES = N_CMP_NEAR + 2
N_TILES = CMP_BASE + N_CMP_TILES


def _rms(xf, g):
    ms = jnp.mean(xf * xf, axis=-1, keepdims=True)
    return xf * lax.rsqrt(ms + EPS) * g


def _dot(a, b):
    return jnp.dot(a, b, preferred_element_type=F32)


def _params(sem):
    return pltpu.CompilerParams(dimension_semantics=sem, vmem_limit_bytes=VMEM_LIMIT_BYTES)


def _mem_kv_kernel(mem_ref, mnorm_ref, w_ref, kg_ref, mk_ref, mvt_ref, *, nb, ml):
    mh = _rms(mem_ref[...], mnorm_ref[...]).astype(BF16)
    kv = _dot(mh, w_ref[...])
    for h in range(MEM_HEADS):
        kh = _rms(kv[:, h * HEAD_DIM:(h + 1) * HEAD_DIM], kg_ref[...]).astype(BF16)
        vh = kv[:, MEM_WIDTH + h * HEAD_DIM:MEM_WIDTH + (h + 1) * HEAD_DIM]
        for b in range(nb):
            mk_ref[b, h] = kh[b * ml:(b + 1) * ml]
            mvt_ref[b, h] = jnp.transpose(vh[b * ml:(b + 1) * ml]).astype(BF16)


def _mem_kv(mem, mem_norm, mem_w_kv, mem_k_gain):
    nb, ml, d = mem.shape
    depth = mem_w_kv.shape[0]
    out = jax.ShapeDtypeStruct((depth, nb, MEM_HEADS, ml, HEAD_DIM), BF16)
    out_t = jax.ShapeDtypeStruct((depth, nb, MEM_HEADS, HEAD_DIM, ml), BF16)
    return pl.pallas_call(
        functools.partial(_mem_kv_kernel, nb=nb, ml=ml),
        out_shape=(out, out_t),
        grid=(depth,),
        in_specs=[
            pl.BlockSpec((nb * ml, d), lambda l: (0, 0)),
            pl.BlockSpec((1, d), lambda l: (0, 0)),
            pl.BlockSpec((None, d, 2 * MEM_WIDTH), lambda l: (l, 0, 0)),
            pl.BlockSpec((None, 1, HEAD_DIM), lambda l: (l, 0, 0)),
        ],
        out_specs=(
            pl.BlockSpec((None, nb, MEM_HEADS, ml, HEAD_DIM), lambda l: (l, 0, 0, 0, 0)),
            pl.BlockSpec((None, nb, MEM_HEADS, HEAD_DIM, ml), lambda l: (l, 0, 0, 0, 0)),
        ),
        compiler_params=_params(("arbitrary",)),
        name="mem_kv",
    )(mem.reshape(nb * ml, d), mem_norm.reshape(1, d), mem_w_kv.astype(BF16),
      mem_k_gain.reshape(depth, 1, HEAD_DIM))


def _rms_t(xt, g_col):
    ms = jnp.mean(xt * xt, axis=0, keepdims=True)
    return xt * lax.rsqrt(ms + EPS) * g_col


def _mem_attn(qm, mk_ref, mvt_ref, qg_col, ymem_ref):
    qm_t = jnp.transpose(qm)
    outs = []
    for h in range(MEM_HEADS):
        qh = _rms_t(qm_t[h * HEAD_DIM:(h + 1) * HEAD_DIM, :], qg_col) * (HEAD_DIM ** -0.5)
        lg = _dot(mk_ref[h], qh.astype(BF16))
        m = jnp.max(lg, axis=0, keepdims=True)
        e = jnp.exp(lg - m)
        l = jnp.sum(e, axis=0, keepdims=True)
        outs.append(_dot(mvt_ref[h], e.astype(BF16)) / l)
    ymem_ref[...] = jnp.transpose(jnp.concatenate(outs, axis=0)).astype(ymem_ref.dtype)


def _bucket_np(d):
    n = np.maximum(d, 0)
    max_exact = REL_BUCKETS // 2
    nf = np.maximum(n, 1).astype(np.float64)
    large = max_exact + (np.log(nf / max_exact) / math.log(REL_MAX_DIST / max_exact)
                         * (REL_BUCKETS - max_exact)).astype(np.int32)
    large = np.minimum(large, REL_BUCKETS - 1)
    return np.where(n < max_exact, n, large).astype(np.int32)


def _bucket_tiles():
    r = np.arange(TQ)[:, None]
    k = np.arange(LANES)[None, :]
    last = REL_BUCKETS - 1
    assert _bucket_np(np.array(N_SEL_NEAR * LANES - (LANES - 1))) == last
    assert _bucket_np(np.array(N_CMP_NEAR * LANES - CMP_STRIDE * (LANES - 1) - (CMP_LEN - 1))) == last
    tiles = []
    for delta in range(N_SEL_NEAR):
        d = delta * LANES + r - k
        tiles.append(np.where(d >= 0, _bucket_np(d), -1).T)
    tiles.append(np.full((TQ, LANES), REL_BUCKETS - 1))
    tiles.append(np.full((TQ, LANES), -1))
    for delta in range(N_WIN_NEAR):
        d = delta * LANES + r - k
        tiles.append(np.where((d >= 0) & (d < WINDOW), _bucket_np(d), -1).T)
    while len(tiles) < CMP_BASE:
        tiles.append(np.full((TQ, LANES), -1))
    for delta in range(N_CMP_NEAR):
        d = delta * LANES + r - CMP_STRIDE * k - (CMP_LEN - 1)
        tiles.append(np.where(d >= 0, _bucket_np(d), -1).T)
    tiles.append(np.full((TQ, LANES), REL_BUCKETS - 1))
    tiles.append(np.full((TQ, LANES), -1))
    return np.stack(tiles).astype(np.int32)


def _bias_tables_kernel(tab_ref, ids_ref, out_ref):
    relative = pl.program_id(0) < CMP_BASE
    n_bits = REL_BUCKETS.bit_length() - 1
    for h in range(NSA_HEADS):
        sub = jnp.where(relative, tab_ref[REL_BUCKETS - 1, h], 0.0)
        leaves = [(jnp.full((SUBLANES, LANES), tab_ref[b, h], F32) - sub) * LOG2E for b in range(REL_BUCKETS)]
        for r0 in range(0, ids_ref.shape[0], SUBLANES):
            ids = ids_ref[r0:r0 + SUBLANES, :]
            level = leaves
            for k in range(n_bits):
                bit = jnp.bitwise_and(ids, 1 << k) != 0
                level = [jnp.where(bit, level[2 * j + 1], level[2 * j]) for j in range(len(level) // 2)]
            out_ref[h, r0:r0 + SUBLANES, :] = jnp.where(ids < 0, NEG, level[0])


def _bias_tables(rel_bias):
    ids = jnp.asarray(_bucket_tiles())
    return pl.pallas_call(
        _bias_tables_kernel,
        out_shape=jax.ShapeDtypeStruct((N_TILES, NSA_HEADS, TQ, LANES), F32),
        grid=(N_TILES,),
        in_specs=[
            pl.BlockSpec(memory_space=pltpu.SMEM),
            pl.BlockSpec((None, TQ, LANES), lambda t: (t, 0, 0)),
        ],
        out_specs=pl.BlockSpec((None, NSA_HEADS, TQ, LANES), lambda t: (t, 0, 0, 0)),
        compiler_params=_params(("arbitrary",)),
        name="bias_tables",
    )(rel_bias.astype(F32), ids)


def _mixer_a_kernel(x_ref, g_ref, w_ref, cw_ref, mk_ref, mvt_ref, qg_ref,
                    ytok_ref, ymem_ref, carry_ref, *, tm):
    @pl.when(pl.program_id(1) == 0)
    def _():
        carry_ref[...] = jnp.zeros_like(carry_ref)

    h = _rms(x_ref[...], g_ref[...]).astype(BF16)
    z = _dot(h, w_ref[...])
    gate_b = z[:, :CONV_CH]
    v = z[:, CONV_CH:2 * CONV_CH] * z[:, 2 * CONV_CH:3 * CONV_CH]
    prev = carry_ref[...]
    last1 = prev[SUBLANES - 1:SUBLANES, :]
    last2 = prev[SUBLANES - 2:SUBLANES - 1, :]
    row = lax.broadcasted_iota(jnp.int32, (tm, CONV_CH), 0)
    v1 = jnp.where(row == 0, last1, pltpu.roll(v, 1, 0))
    v2 = jnp.where(row == 0, last2, jnp.where(row == 1, last1, pltpu.roll(v, 2, 0)))
    cw = cw_ref[...]
    y = gate_b * (cw[0:1, :] * v2 + cw[1:2, :] * v1 + cw[2:3, :] * v)
    carry_ref[...] = v[tm - SUBLANES:, :]
    ytok_ref[...] = y.astype(ytok_ref.dtype)
    _mem_attn(z[:, 3 * CONV_CH:], mk_ref, mvt_ref, qg_ref[...], ymem_ref)


def _mixer_a(x, norm_g, w_in, conv_w, mk, mv_t, mem_q_gain):
    nb, s, d = x.shape
    ml = mk.shape[2]
    n_in = w_in.shape[1]
    return pl.pallas_call(
        functools.partial(_mixer_a_kernel, tm=TM),
        out_shape=(jax.ShapeDtypeStruct((nb, s, CONV_CH), BF16),
                   jax.ShapeDtypeStruct((nb, s, MEM_WIDTH), BF16)),
        grid=(nb, s // TM),
        in_specs=[
            pl.BlockSpec((None, TM, d), lambda b, i: (b, i, 0)),
            pl.BlockSpec((1, d), lambda b, i: (0, 0)),
            pl.BlockSpec((d, n_in), lambda b, i: (0, 0)),
            pl.BlockSpec((3, CONV_CH), lambda b, i: (0, 0)),
            pl.BlockSpec((None, MEM_HEADS, ml, HEAD_DIM), lambda b, i: (b, 0, 0, 0)),
            pl.BlockSpec((None, MEM_HEADS, HEAD_DIM, ml), lambda b, i: (b, 0, 0, 0)),
            pl.BlockSpec((HEAD_DIM, 1), lambda b, i: (0, 0)),
        ],
        out_specs=(
            pl.BlockSpec((None, TM, CONV_CH), lambda b, i: (b, i, 0)),
            pl.BlockSpec((None, TM, MEM_WIDTH), lambda b, i: (b, i, 0)),
        ),
        scratch_shapes=[pltpu.VMEM((SUBLANES, CONV_CH), F32)],
        compiler_params=_params(("arbitrary", "arbitrary")),
        name="mixer_a",
    )(x, norm_g.reshape(1, d), w_in.astype(BF16), conv_w, mk, mv_t, mem_q_gain.reshape(HEAD_DIM, 1))


B_GATE0 = NSA_HEADS * HEAD_DIM
B_QMEM0 = B_GATE0 + KV_GROUPS * LANES
B_WIDTH = B_QMEM0 + MEM_WIDTH


GATE_ROWS = 32


def _b_inproj_kernel(x_ref, g_ref, w_ref, qg_ref, mk_ref, mvt_ref, mqg_ref,
                     qt_ref, gates_ref, ymem_ref):
    h = _rms(x_ref[...], g_ref[...]).astype(BF16)
    z = _dot(h, w_ref[...])
    qg = qg_ref[...]
    tm = z.shape[0]
    for pair in range(NSA_HEADS // 2):
        z_t = jnp.transpose(z[:, pair * LANES:(pair + 1) * LANES])
        for half in range(2):
            qh_t = _rms_t(z_t[half * HEAD_DIM:(half + 1) * HEAD_DIM, :], qg) * (HEAD_DIM ** -0.5 * LOG2E)
            qh_t = qh_t.astype(qt_ref.dtype)
            g, hg = divmod(2 * pair + half, HPG)
            for qt in range(tm // TQ):
                lane0 = (qt * HPG + hg) * TQ
                qt_ref[g, :, lane0:lane0 + TQ] = qh_t[:, qt * TQ:(qt + 1) * TQ]
    for g in range(KV_GROUPS):
        gl = z[:, B_GATE0 + g * LANES:B_GATE0 + (g + 1) * LANES]
        gates_ref[g] = jnp.transpose(1.0 / (1.0 + jnp.exp(-gl)))[:GATE_ROWS, :]
    _mem_attn(z[:, B_QMEM0:], mk_ref, mvt_ref, mqg_ref[...], ymem_ref)


def _b_inproj(x, norm_g, w_in, q_gain, mk, mv_t, mem_q_gain):
    nb, s, d = x.shape
    ml = mk.shape[2]
    nq = NSA_HEADS * HEAD_DIM
    ng = HPG * N_BRANCH
    pad = jnp.zeros((d, LANES - ng), w_in.dtype)
    w = jnp.concatenate([w_in[:, :nq], w_in[:, nq:nq + ng], pad,
                         w_in[:, nq + ng:nq + 2 * ng], pad, w_in[:, nq + 2 * ng:]], axis=1)
    return pl.pallas_call(
        _b_inproj_kernel,
        out_shape=(jax.ShapeDtypeStruct((nb, KV_GROUPS, HEAD_DIM, HPG * s), BF16),
                   jax.ShapeDtypeStruct((nb, KV_GROUPS, GATE_ROWS, s), F32),
                   jax.ShapeDtypeStruct((nb, s, MEM_WIDTH), BF16)),
        grid=(nb, s // TM),
        in_specs=[
            pl.BlockSpec((None, TM, d), lambda b, i: (b, i, 0)),
            pl.BlockSpec((1, d), lambda b, i: (0, 0)),
            pl.BlockSpec((d, B_WIDTH), lambda b, i: (0, 0)),
            pl.BlockSpec((HEAD_DIM, 1), lambda b, i: (0, 0)),
            pl.BlockSpec((None, MEM_HEADS, ml, HEAD_DIM), lambda b, i: (b, 0, 0, 0)),
            pl.BlockSpec((None, MEM_HEADS, HEAD_DIM, ml), lambda b, i: (b, 0, 0, 0)),
            pl.BlockSpec((HEAD_DIM, 1), lambda b, i: (0, 0)),
        ],
        out_specs=(
            pl.BlockSpec((None, KV_GROUPS, HEAD_DIM, HPG * TM), lambda b, i: (b, 0, 0, i)),
            pl.BlockSpec((None, KV_GROUPS, GATE_ROWS, TM), lambda b, i: (b, 0, 0, i)),
            pl.BlockSpec((None, TM, MEM_WIDTH), lambda b, i: (b, i, 0)),
        ),
        compiler_params=_params(("arbitrary", "arbitrary")),
        name="b_inproj",
    )(x, norm_g.reshape(1, d), w.astype(BF16), q_gain.reshape(HEAD_DIM, 1), mk, mv_t,
      mem_q_gain.reshape(HEAD_DIM, 1))


FF_CHUNK = 1024
TM_POST = 1024


def _post_kernel(x_ref, ytok_ref, ymem_ref, woa_ref, wob_ref, g_ref, wup_ref, wdn_ref, o_ref, hm_ref):
    @pl.when(pl.program_id(1) == 0)
    def _():
        x1 = x_ref[...] + _dot(ytok_ref[...], woa_ref[...]) + _dot(ymem_ref[...], wob_ref[...])
        o_ref[...] = x1
        hm_ref[...] = _rms(x1, g_ref[...]).astype(hm_ref.dtype)

    a = _dot(hm_ref[...], wup_ref[...].astype(BF16))
    a = jnp.square(jnp.maximum(a, 0.0)).astype(BF16)
    o_ref[...] += _dot(a, wdn_ref[...].astype(BF16))


def _post(x, y_tok, y_mem, w_out, norm_g, w_up, w_down, layer):
    n, d = x.shape
    ff = w_up.shape[2]
    nt = y_tok.shape[1]
    assert nt % MEM_WIDTH == 0
    tm = min(TM_POST, n)
    const = lambda i, c: (0, 0)
    rows = lambda i, c: (i, 0)
    return pl.pallas_call(
        _post_kernel,
        out_shape=jax.ShapeDtypeStruct((n, d), F32),
        grid=(n // tm, ff // FF_CHUNK),
        in_specs=[
            pl.BlockSpec((tm, d), rows),
            pl.BlockSpec((tm, nt), rows),
            pl.BlockSpec((tm, MEM_WIDTH), rows),
            pl.BlockSpec((None, nt, d), lambda i, c: (layer, 0, 0)),
            pl.BlockSpec((None, MEM_WIDTH, d), lambda i, c: (layer, nt // MEM_WIDTH, 0)),
            pl.BlockSpec((1, d), const),
            pl.BlockSpec((None, d, FF_CHUNK), lambda i, c: (layer, 0, c)),
            pl.BlockSpec((None, FF_CHUNK, d), lambda i, c: (layer, c, 0)),
        ],
        out_specs=pl.BlockSpec((tm, d), rows),
        scratch_shapes=[pltpu.VMEM((tm, d), BF16)],
        compiler_params=_params(("arbitrary", "arbitrary")),
        name="post",
    )(x, y_tok, y_mem, w_out, w_out, norm_g.reshape(1, d), w_up, w_down)


K_AUG = 2 * HEAD_DIM
KS_AUG = K_AUG + SEL_LANES
V_AUG = HEAD_DIM + 16


def _build_kv_kernel(x_ref, g_ref, w_ref, kg_ref, ucmp_ref, ks_ref, vs_ref, kw_ref, vw_ref, raw_scr):
    h = _rms(x_ref[...], g_ref[...]).astype(BF16)
    kv = _dot(h, w_ref[...])
    gw = KV_GROUPS * HEAD_DIM
    tm = kv.shape[0]
    for ty in range(2):
        raw_scr[ty] = kv[:, ty * gw:(ty + 1) * gw]
        for t in range(CMP_STRIDE):
            rows = raw_scr[ty, pl.ds(t, tm // CMP_STRIDE, stride=CMP_STRIDE), :]
            for g in range(KV_GROUPS):
                ucmp_ref[ty, g, :, t * HEAD_DIM:(t + 1) * HEAD_DIM] = (
                    rows[:, g * HEAD_DIM:(g + 1) * HEAD_DIM].astype(ucmp_ref.dtype))
    kg = kg_ref[...]
    ones = (lax.broadcasted_iota(jnp.int32, (tm, K_AUG - HEAD_DIM), 1) < 2).astype(ks_ref.dtype)
    pos = pl.program_id(1) * tm + lax.broadcasted_iota(jnp.int32, (tm, SEL_LANES), 0)
    blk = lax.broadcasted_iota(jnp.int32, (tm, SEL_LANES), 1)
    onehot = jnp.where(blk == jnp.right_shift(pos, SEL_SHIFT), MASK_BIG, 0.0).astype(ks_ref.dtype)
    ones_row = (lax.broadcasted_iota(jnp.int32, (V_AUG - HEAD_DIM, tm), 0) == 0).astype(vs_ref.dtype)
    for g in range(KV_GROUPS):
        sl = lambda t: kv[:, t * gw + g * HEAD_DIM:t * gw + (g + 1) * HEAD_DIM]
        ks_ref[g, :, :HEAD_DIM] = _rms(sl(2), kg[1:2, :]).astype(ks_ref.dtype)
        ks_ref[g, :, HEAD_DIM:K_AUG] = ones
        ks_ref[g, :, K_AUG:] = onehot
        vs_ref[g, :HEAD_DIM, :] = jnp.transpose(sl(3)).astype(vs_ref.dtype)
        vs_ref[g, HEAD_DIM:, :] = ones_row
        kw_ref[g, :, :HEAD_DIM] = _rms(sl(4), kg[2:3, :]).astype(kw_ref.dtype)
        kw_ref[g, :, HEAD_DIM:] = ones
        vw_ref[g, :HEAD_DIM, :] = jnp.transpose(sl(5)).astype(vw_ref.dtype)
        vw_ref[g, HEAD_DIM:, :] = ones_row


def _build_kv(x, kv_norm, w_kv, k_gain):
    nb, s, d = x.shape
    gw = KV_GROUPS * HEAD_DIM
    ks = jax.ShapeDtypeStruct((nb, KV_GROUPS, s, KS_AUG), BF16)
    kw = jax.ShapeDtypeStruct((nb, KV_GROUPS, s, K_AUG), BF16)
    vt = jax.ShapeDtypeStruct((nb, KV_GROUPS, V_AUG, s), BF16)
    ks_spec = pl.BlockSpec((None, KV_GROUPS, TM, KS_AUG), lambda b, i: (b, 0, i, 0))
    kw_spec = pl.BlockSpec((None, KV_GROUPS, TM, K_AUG), lambda b, i: (b, 0, i, 0))
    v_spec = pl.BlockSpec((None, KV_GROUPS, V_AUG, TM), lambda b, i: (b, 0, 0, i))
    return pl.pallas_call(
        _build_kv_kernel,
        out_shape=(jax.ShapeDtypeStruct((2, nb, KV_GROUPS, s // CMP_STRIDE, CMP_STRIDE * HEAD_DIM), BF16),
                   ks, vt, kw, vt),
        grid=(nb, s // TM),
        in_specs=[
            pl.BlockSpec((None, TM, d), lambda b, i: (b, i, 0)),
            pl.BlockSpec((1, d), lambda b, i: (0, 0)),
            pl.BlockSpec((d, 6 * gw), lambda b, i: (0, 0)),
            pl.BlockSpec((N_BRANCH, HEAD_DIM), lambda b, i: (0, 0)),
        ],
        out_specs=(pl.BlockSpec((2, None, KV_GROUPS, TM // CMP_STRIDE, CMP_STRIDE * HEAD_DIM),
                                lambda b, i: (0, b, 0, i, 0)),
                   ks_spec, v_spec, kw_spec, v_spec),
        scratch_shapes=[pltpu.VMEM((2, TM, gw), F32)],
        compiler_params=_params(("arbitrary", "arbitrary")),
        name="build_kv",
    )(x, kv_norm.reshape(1, d), w_kv.astype(BF16), k_gain)


def _compress_kernel(u_ref, w1_ref, w2_ref, pos_ref, kg_ref, out_ref, out_t_ref, *, nc):
    half = CMP_STRIDE * HEAD_DIM
    u = u_ref[...]
    lo = _dot(u, w1_ref[:half, :])
    hi = _dot(u, w1_ref[half:, :])
    posb = _dot(jnp.broadcast_to(pos_ref[...], (8, 2 * half)).astype(BF16), w1_ref[...])[0:1, :]
    h1 = lo + pltpu.roll(hi, nc - 1, 0) + posb
    hid = 0.5 * h1 * (1.0 + jnp.tanh(math.sqrt(2.0 / math.pi) * (h1 + 0.044715 * (h1 * h1 * h1))))
    c = _dot(hid.astype(BF16), w2_ref[...])
    is_key = pl.program_id(0) == 0
    out = jnp.where(is_key, _rms(c, kg_ref[...]), c)
    out_ref[...] = out.astype(out_ref.dtype)
    out_t_ref[...] = jnp.transpose(out).astype(out_t_ref.dtype)


def _compress(u, cmp_pos, cmp_w1, cmp_w2, k_gain0):
    _, nb, _, nc, width = u.shape
    hidden = cmp_w1.shape[2]
    return pl.pallas_call(
        functools.partial(_compress_kernel, nc=nc),
        out_shape=(jax.ShapeDtypeStruct((2, nb, KV_GROUPS, nc, HEAD_DIM), BF16),
                   jax.ShapeDtypeStruct((2, nb, KV_GROUPS, HEAD_DIM, nc), BF16)),
        grid=(2, nb, KV_GROUPS),
        in_specs=[
            pl.BlockSpec((None, None, None, nc, width), lambda t, b, g: (t, b, g, 0, 0)),
            pl.BlockSpec((None, 2 * width, hidden), lambda t, b, g: (t, 0, 0)),
            pl.BlockSpec((None, hidden, HEAD_DIM), lambda t, b, g: (t, 0, 0)),
            pl.BlockSpec((None, 1, 2 * width), lambda t, b, g: (t, 0, 0)),
            pl.BlockSpec((1, HEAD_DIM), lambda t, b, g: (0, 0)),
        ],
        out_specs=(pl.BlockSpec((None, None, None, nc, HEAD_DIM), lambda t, b, g: (t, b, g, 0, 0)),
                   pl.BlockSpec((None, None, None, HEAD_DIM, nc), lambda t, b, g: (t, b, g, 0, 0))),
        compiler_params=_params(("arbitrary", "arbitrary", "arbitrary")),
        name="compress",
    )(u, cmp_w1.astype(BF16), cmp_w2.astype(BF16), cmp_pos.reshape(2, 1, 2 * width),
      k_gain0.reshape(1, HEAD_DIM))


def _overlap_np(nc, n_sel):
    c0 = np.arange(nc)[:, None] * CMP_STRIDE
    s0 = np.arange(SEL_LANES)[None, :] * SEL_BLOCK
    ov = np.minimum(c0 + CMP_LEN, s0 + SEL_BLOCK) - np.maximum(c0, s0)
    ov = np.clip(ov, 0, None).astype(np.float32) / CMP_LEN
    ov[nc - 1:, :] = 0.0
    ov[:, n_sel:] = 0.0
    return ov


def _select_topk(score, ntop):
    blk = lax.broadcasted_iota(jnp.int32, score.shape, 0).astype(F32)
    for _ in range(ntop):
        m = jnp.max(score, axis=0, keepdims=True)
        first = jnp.min(jnp.where(score == m, blk, float(SEL_LANES)), axis=0, keepdims=True)
        score = jnp.where(blk == first, -jnp.inf, score)
    return score == -jnp.inf


CMP_QSUB = 2
TOPK_GROUP = 32


def _cmp_attn_kernel(*refs, nj, n_sel):
    qt_ref, kc_ref, vct_ref, ovt_ref = refs[:4]
    tab_refs = refs[4:4 + nj * CMP_QSUB]
    oct_ref, unsel_ref = refs[4 + nj * CMP_QSUB:]
    i = pl.program_id(2)
    nq = CMP_QSUB * TQ
    def attend(nk):
        rows = nk * LANES
        s = _dot(kc_ref[:rows, :], qt_ref[...])
        s = jnp.concatenate(
            [jnp.concatenate([s[jt * LANES:(jt + 1) * LANES, (qs * HPG + h) * TQ:(qs * HPG + h + 1) * TQ]
                              + tab_refs[qs * nj + jt][h]
                              for qs in range(CMP_QSUB) for h in range(HPG)], axis=1) for jt in range(nk)], axis=0)
        m = jnp.maximum(jnp.max(s, axis=0, keepdims=True), 0.5 * NEG)
        e = jnp.exp2(s - m)
        l = jnp.sum(e, axis=0, keepdims=True)
        p = e * (1.0 / jnp.where(l == 0.0, 1.0, l))
        oct_ref[...] = _dot(vct_ref[:, :rows], p.astype(BF16))

        psums = []
        for qs in range(CMP_QSUB):
            acc = p[:, qs * HPG * TQ:(qs * HPG + 1) * TQ]
            for h in range(1, HPG):
                acc = acc + p[:, (qs * HPG + h) * TQ:(qs * HPG + h + 1) * TQ]
            psums.append(acc)
        psum = jnp.concatenate(psums, axis=1)
        p_hi = psum.astype(BF16)
        p_lo = (psum - p_hi.astype(F32)).astype(BF16)
        return _dot(ovt_ref[:, :rows], p_hi) + _dot(ovt_ref[:, :rows], p_lo)

    last_visible = (nq * (i + 1) - CMP_LEN) // CMP_STRIDE
    imp_t = lax.switch(last_visible // LANES, [functools.partial(attend, nk) for nk in range(1, nj + 1)])

    tq = i * nq + lax.broadcasted_iota(jnp.int32, (SEL_LANES, nq), 1)
    j = lax.broadcasted_iota(jnp.int32, (SEL_LANES, nq), 0)
    back = jnp.right_shift(tq, SEL_SHIFT) - j
    forced = (j == 0) | ((back >= 0) & (back < N_LOCAL_SEL))
    causal = back >= 0
    n_free = min(SEL_TOPK, n_sel) - (N_LOCAL_SEL + 1)
    score = jnp.where(causal & ~forced, imp_t, NEG)

    def pick(n_groups):
        rows = n_groups * TOPK_GROUP
        top = _select_topk(score[:rows, :], n_free).astype(F32)
        if rows == SEL_LANES:
            return top
        return jnp.concatenate([top, jnp.zeros((SEL_LANES - rows, nq), F32)], axis=0)

    causal_blocks = (nq * (i + 1)) // SEL_BLOCK
    picked = lax.switch((causal_blocks - 1) // TOPK_GROUP,
                        [functools.partial(pick, n) for n in range(1, SEL_LANES // TOPK_GROUP + 1)])
    sel = causal & (forced | (picked > 0.0))
    unsel_ref[...] = jnp.where(sel, 0.0, -1.0).astype(unsel_ref.dtype)


def _cmp_attn(q_t, kvc, kvc_t, tables):
    nb = q_t.shape[0]
    s = q_t.shape[3] // HPG
    nc = s // CMP_STRIDE
    nj = nc // LANES
    n_sel = s // SEL_BLOCK
    ov_t = jnp.asarray(_overlap_np(nc, n_sel).T, dtype=BF16)

    def tab_map(qs, jt):
        def index_map(b, g, i):
            didx = CMP_QSUB * i + qs - (LANES * CMP_STRIDE // TQ) * jt
            row = jnp.where(didx < 0, CMP_MASKED, jnp.minimum(didx, CMP_CONST))
            return (CMP_BASE + row, g, 0, 0)
        return index_map

    nq = CMP_QSUB * TQ
    qt_spec = pl.BlockSpec((None, None, HEAD_DIM, HPG * nq), lambda b, g, i: (b, g, 0, i))
    n_tabs = CMP_QSUB * nj
    return pl.pallas_call(
        functools.partial(_cmp_attn_kernel, nj=nj, n_sel=n_sel),
        out_shape=(jax.ShapeDtypeStruct((nb, KV_GROUPS, HEAD_DIM, HPG * s), F32),
                   jax.ShapeDtypeStruct((nb, KV_GROUPS, SEL_LANES, s), BF16)),
        grid=(nb, KV_GROUPS, s // nq),
        in_specs=[
            qt_spec,
            pl.BlockSpec((None, None, None, nc, HEAD_DIM), lambda b, g, i: (0, b, g, 0, 0)),
            pl.BlockSpec((None, None, None, HEAD_DIM, nc), lambda b, g, i: (1, b, g, 0, 0)),
            pl.BlockSpec((SEL_LANES, nc), lambda b, g, i: (0, 0)),
        ] + [pl.BlockSpec((None, HPG, LANES, TQ), tab_map(qs, jt))
             for qs in range(CMP_QSUB) for jt in range(nj)],
        out_specs=(
            qt_spec,
            pl.BlockSpec((None, None, SEL_LANES, nq), lambda b, g, i: (b, g, 0, i)),
        ),
        compiler_params=_params(("arbitrary", "arbitrary", "arbitrary")),
        name="cmp_attn",
    )(q_t, kvc, kvc_t, ov_t, *([tables] * n_tabs))


class _FlashStream:
    def __init__(self, q_aug, k_ref, vt_ref, bias_fn, s_scr, p_scr):
        self.q_aug, self.k_ref, self.vt_ref, self.bias_fn = q_aug, k_ref, vt_ref, bias_fn
        self.s_scr, self.p_scr = s_scr, p_scr

    def _scores(self, t):
        k0 = pl.multiple_of(jnp.maximum(t, 0) * TK_SEL, TK_SEL)
        return self.bias_fn(t, _dot(self.k_ref[pl.ds(k0, TK_SEL), :], self.q_aug))

    def _pv(self, t, slot):
        k0 = pl.multiple_of(jnp.maximum(t, 0) * TK_SEL, TK_SEL)
        return _dot(self.vt_ref[:, pl.ds(k0, TK_SEL)], self.p_scr[slot])

    def _softmax(self, slot, m):
        s = self.s_scr[slot]
        m_new = jnp.maximum(m, jnp.max(s, axis=0, keepdims=True))
        self.p_scr[slot] = jnp.exp2(s - m_new).astype(self.p_scr.dtype)
        return m_new, jnp.exp2(m - m_new)

    def start(self, a, first_trip_static=False):
        lanes = self.q_aug.shape[1]
        self.s_scr[0] = self._scores(a)
        if not first_trip_static:
            self.p_scr[1] = jnp.zeros(self.p_scr.shape[1:], self.p_scr.dtype)
        return (jnp.full((1, lanes), NEG, F32), jnp.ones((1, lanes), F32), jnp.zeros((V_AUG, lanes), F32))

    def trip(self, a, carry, has_next, is_first=False):
        m, alpha, acc = carry
        if not is_first:
            acc = alpha * acc + self._pv(a - 1, 1)
        m, alpha = self._softmax(0, m)
        self.s_scr[1] = self._scores(a + 1)
        acc = alpha * acc + self._pv(a, 0)
        m, alpha = self._softmax(1, m)
        if has_next:
            self.s_scr[0] = self._scores(a + 2)
        return m, alpha, acc

    @staticmethod
    def _normalised(acc):
        return acc[:HEAD_DIM, :] / acc[HEAD_DIM:HEAD_DIM + 1, :]

    def finish_pair(self, a, carry):
        _, alpha, acc = carry
        return self._normalised(alpha * acc + self._pv(a + 1, 1))

    def finish_single(self, a, carry):
        m, alpha, acc = carry
        acc = alpha * acc + self._pv(a - 1, 1)
        _, alpha = self._softmax(0, m)
        return self._normalised(alpha * acc + self._pv(a, 0))


def _add_tiles(s, tile_fn):
    rows = []
    for part in range(TK_SEL // LANES):
        cols = [s[part * LANES:(part + 1) * LANES, h * TQ:(h + 1) * TQ] + tile_fn(part, h)
                for h in range(HPG)]
        rows.append(jnp.concatenate(cols, axis=1))
    return jnp.concatenate(rows, axis=0)


def _sel_win_kernel(qt_ref, ks_ref, vs_ref, kw_ref, vw_ref, unsel_ref, tab_ref, cvec_ref,
                    oct_ref, gates_ref, y_ref, ss_scr, ps_scr, sw_scr, pw_scr):
    i = pl.program_id(2)
    parts = TK_SEL // LANES

    c = cvec_ref[...]
    c_hi = c.astype(BF16).astype(F32)
    c_lo = c - c_hi
    arow = lax.broadcasted_iota(jnp.int32, (K_AUG - HEAD_DIM, HPG * TQ), 0)
    aug = jnp.where(arow == 0, c_hi, jnp.where(arow == 1, c_lo, 0.0)).astype(BF16)
    q_win = jnp.concatenate([qt_ref[...], aug], axis=0)
    unsel_t = unsel_ref[...]
    q_sel = jnp.concatenate([q_win, jnp.concatenate([unsel_t] * HPG, axis=1)], axis=0)

    def sel_bias(t, s):
        def tile(part, h):
            delta = i - parts * t - part
            row = jnp.where(delta < 0, SEL_MASKED, jnp.minimum(delta, SEL_CONST))
            return tab_ref[row, h]

        return _add_tiles(s, tile)

    def win_bias(t, s):
        def tile(part, h):
            delta = i - parts * t - part
            hidden = (delta < 0) | (delta >= N_WIN_NEAR) | (t < 0)
            return tab_ref[jnp.where(hidden, SEL_MASKED, WIN_BASE + delta), h]

        return _add_tiles(s, tile)

    n_tiles = i // parts + 1
    n_pairs = (n_tiles + 1) // 2
    sel = _FlashStream(q_sel, ks_ref, vs_ref, sel_bias, ss_scr, ps_scr)
    win = _FlashStream(q_win, kw_ref, vw_ref, win_bias, sw_scr, pw_scr)
    win_a0 = n_tiles - WIN_TILES
    last_pair = 2 * (n_pairs - 1)
    carry_s = sel.start(0)
    carry_w = win.start(win_a0, first_trip_static=True)
    carry_s = lax.fori_loop(0, n_pairs - 1, lambda u, c: sel.trip(2 * u, c, has_next=True), carry_s)
    carry_s = sel.trip(last_pair, carry_s, has_next=False)
    carry_w = win.trip(win_a0, carry_w, has_next=True, is_first=True)
    os_t = sel.finish_pair(last_pair, carry_s)
    ow_t = win.finish_single(win_a0 + WIN_TILES - 1, carry_w)

    oc_t = oct_ref[...]
    gt = gates_ref[...]
    for hg in range(HPG):
        c0 = N_BRANCH * hg
        lanes = slice(hg * TQ, (hg + 1) * TQ)
        out_t = (gt[c0:c0 + 1, :] * oc_t[:, lanes] + gt[c0 + 1:c0 + 2, :] * os_t[:, lanes]
                 + gt[c0 + 2:c0 + 3, :] * ow_t[:, lanes])
        y_ref[:, hg * HEAD_DIM:(hg + 1) * HEAD_DIM] = jnp.transpose(out_t).astype(y_ref.dtype)


def _sel_win_attn(q_t, ks, vs_t, kw, vw_t, unsel, tables, cvec, oc_t, gates_t):
    nb = q_t.shape[0]
    s = q_t.shape[3] // HPG
    ks_spec = pl.BlockSpec((None, None, s, KS_AUG), lambda b, g, i: (b, g, 0, 0))
    kw_spec = pl.BlockSpec((None, None, s, K_AUG), lambda b, g, i: (b, g, 0, 0))
    v_spec = pl.BlockSpec((None, None, V_AUG, s), lambda b, g, i: (b, g, 0, 0))
    qt_spec = pl.BlockSpec((None, None, HEAD_DIM, HPG * TQ), lambda b, g, i: (b, g, 0, i))
    gw = HPG * HEAD_DIM
    return pl.pallas_call(
        _sel_win_kernel,
        out_shape=jax.ShapeDtypeStruct((nb, s, NSA_HEADS * HEAD_DIM), BF16),
        grid=(nb, KV_GROUPS, s // TQ),
        in_specs=[
            qt_spec,
            ks_spec, v_spec, kw_spec, v_spec,
            pl.BlockSpec((None, None, SEL_LANES, TQ), lambda b, g, i: (b, g, 0, i)),
            pl.BlockSpec((CMP_BASE, HPG, LANES, TQ), lambda b, g, i: (0, g, 0, 0)),
            pl.BlockSpec((None, 1, HPG * TQ), lambda b, g, i: (g, 0, 0)),
            qt_spec,
            pl.BlockSpec((None, None, GATE_ROWS, TQ), lambda b, g, i: (b, g, 0, i)),
        ],
        out_specs=pl.BlockSpec((None, TQ, gw), lambda b, g, i: (b, i, g)),
        scratch_shapes=[pltpu.VMEM((2, TK_SEL, HPG * TQ), F32), pltpu.VMEM((2, TK_SEL, HPG * TQ), BF16),
                        pltpu.VMEM((2, TK_SEL, HPG * TQ), F32), pltpu.VMEM((2, TK_SEL, HPG * TQ), BF16)],
        compiler_params=_params(("arbitrary", "arbitrary", "arbitrary")),
        name="sel_win_attn",
    )(q_t, ks, vs_t, kw, vw_t, unsel, tables, cvec, oc_t, gates_t)


def kernel(x, mem, mix_norm, a_w_in, a_conv_w, b_w_in, b_q_gain, kv_norm, w_kv_shared, k_gain,
           cmp_pos, cmp_w1, cmp_w2, rel_bias, mem_norm, mem_w_kv, mem_q_gain, mem_k_gain, w_out,
           mlp_norm, w_up, w_down):
    nb, s, d = x.shape
    depth = mix_norm.shape[0]
    n_a = a_w_in.shape[0]
    assert s % (LANES * CMP_STRIDE) == 0 and s // SEL_BLOCK <= SEL_LANES
    assert WIN_BASE + N_WIN_NEAR <= CMP_BASE
    assert WIN_TILES == 3

    mk, mv_t = _mem_kv(mem, mem_norm, mem_w_kv, mem_k_gain)
    tables = _bias_tables(rel_bias)
    cvec = jnp.repeat(rel_bias[REL_BUCKETS - 1].astype(F32) * LOG2E, TQ).reshape(KV_GROUPS, 1, HPG * TQ)
    w_out_bf = w_out.astype(BF16)
    shared = None
    for layer in range(depth):
        if layer < n_a:
            y_tok, y_mem = _mixer_a(x, mix_norm[layer], a_w_in[layer], a_conv_w[layer],
                                    mk[layer], mv_t[layer], mem_q_gain[layer])
        else:
            j = layer - n_a
            ks, vs_t, kw, vw_t, kvc, kvc_t = shared
            q_t, gates_t, y_mem = _b_inproj(x, mix_norm[layer], b_w_in[j], b_q_gain[j],
                                            mk[layer], mv_t[layer], mem_q_gain[layer])
            oc_t, unsel = _cmp_attn(q_t, kvc, kvc_t, tables)
            y_tok = _sel_win_attn(q_t, ks, vs_t, kw, vw_t, unsel, tables, cvec, oc_t, gates_t)
        x = _post(x.reshape(nb * s, d), y_tok.reshape(nb * s, -1), y_mem.reshape(nb * s, -1),
                  w_out_bf, mlp_norm[layer], w_up, w_down, layer).reshape(nb, s, d)
        if layer == n_a - 1:
            ucmp, ks, vs_t, kw, vw_t = _build_kv(x, kv_norm, w_kv_shared, k_gain)
            kvc, kvc_t = _compress(ucmp, cmp_pos, cmp_w1, cmp_w2, k_gain[0])
            shared = (ks, vs_t, kw, vw_t, kvc, kvc_t)
    return x
```

```python
import functools
import math

import numpy as np
import jax
import jax.numpy as jnp
from jax import lax
from jax.experimental import pallas as pl
from jax.experimental.pallas import tpu as pltpu

F32 = jnp.float32
BF16 = jnp.bfloat16

HEAD_DIM = 64
MEM_HEADS = 4
MEM_WIDTH = MEM_HEADS * HEAD_DIM
CONV_CH = 768
NSA_HEADS = 12
KV_GROUPS = 2
HPG = NSA_HEADS // KV_GROUPS
N_BRANCH = 3
CMP_LEN = 32
CMP_STRIDE = 16
SEL_BLOCK = 64
SEL_SHIFT = SEL_BLOCK.bit_length() - 1
SEL_TOPK = 16
N_LOCAL_SEL = 2
WINDOW = 512
REL_BUCKETS = 32
REL_MAX_DIST = 1024
EPS = 1e-6
NEG = -1e30
LOG2E = math.log2(math.e)
MASK_BIG = 2.0 ** 100

LANES = 128
SUBLANES = 8
VMEM_LIMIT_BYTES = 56 * 1024 * 1024

TQ = 128
TK_SEL = 256
SEL_LANES = 128
TM = 512
WIN_TILES = max(((TQ * i - (WINDOW - 1)) % TK_SEL + WINDOW + TQ - 2) // TK_SEL + 1 for i in range(TK_SEL))

N_SEL_NEAR = 8
SEL_CONST = N_SEL_NEAR
SEL_MASKED = N_SEL_NEAR + 1
WIN_BASE = N_SEL_NEAR + 2
N_WIN_NEAR = WINDOW // LANES + 1
CMP_BASE = 16
N_CMP_NEAR = 23
CMP_CONST = N_CMP_NEAR
CMP_MASKED = N_CMP_NEAR + 1
N_CMP_TILES = N_CMP_NEAR + 2
N_TILES = CMP_BASE + N_CMP_TILES


def _rms(xf, g):
    ms = jnp.mean(xf * xf, axis=-1, keepdims=True)
    return xf * lax.rsqrt(ms + EPS) * g


def _dot(a, b):
    return jnp.dot(a, b, preferred_element_type=F32)


def _params(sem):
    return pltpu.CompilerParams(dimension_semantics=sem, vmem_limit_bytes=VMEM_LIMIT_BYTES)


def _mem_kv_kernel(mem_ref, mnorm_ref, w_ref, kg_ref, mk_ref, mvt_ref, *, nb, ml):
    mh = _rms(mem_ref[...], mnorm_ref[...]).astype(BF16)
    kv = _dot(mh, w_ref[...])
    for h in range(MEM_HEADS):
        kh = _rms(kv[:, h * HEAD_DIM:(h + 1) * HEAD_DIM], kg_ref[...]).astype(BF16)
        vh = kv[:, MEM_WIDTH + h * HEAD_DIM:MEM_WIDTH + (h + 1) * HEAD_DIM]
        for b in range(nb):
            mk_ref[b, h] = kh[b * ml:(b + 1) * ml]
            mvt_ref[b, h] = jnp.transpose(vh[b * ml:(b + 1) * ml]).astype(BF16)


def _mem_kv(mem, mem_norm, mem_w_kv, mem_k_gain):
    nb, ml, d = mem.shape
    depth = mem_w_kv.shape[0]
    out = jax.ShapeDtypeStruct((depth, nb, MEM_HEADS, ml, HEAD_DIM), BF16)
    out_t = jax.ShapeDtypeStruct((depth, nb, MEM_HEADS, HEAD_DIM, ml), BF16)
    return pl.pallas_call(
        functools.partial(_mem_kv_kernel, nb=nb, ml=ml),
        out_shape=(out, out_t),
        grid=(depth,),
        in_specs=[
            pl.BlockSpec((nb * ml, d), lambda l: (0, 0)),
            pl.BlockSpec((1, d), lambda l: (0, 0)),
            pl.BlockSpec((None, d, 2 * MEM_WIDTH), lambda l: (l, 0, 0)),
            pl.BlockSpec((None, 1, HEAD_DIM), lambda l: (l, 0, 0)),
        ],
        out_specs=(
            pl.BlockSpec((None, nb, MEM_HEADS, ml, HEAD_DIM), lambda l: (l, 0, 0, 0, 0)),
            pl.BlockSpec((None, nb, MEM_HEADS, HEAD_DIM, ml), lambda l: (l, 0, 0, 0, 0)),
        ),
        compiler_params=_params(("arbitrary",)),
        name="mem_kv",
    )(mem.reshape(nb * ml, d), mem_norm.reshape(1, d), mem_w_kv.astype(BF16),
      mem_k_gain.reshape(depth, 1, HEAD_DIM))


def _rms_t(xt, g_col):
    ms = jnp.mean(xt * xt, axis=0, keepdims=True)
    return xt * lax.rsqrt(ms + EPS) * g_col


def _mem_attn(qm, mk_ref, mvt_ref, qg_col, ymem_ref):
    qm_t = jnp.transpose(qm)
    outs = []
    for h in range(MEM_HEADS):
        qh = _rms_t(qm_t[h * HEAD_DIM:(h + 1) * HEAD_DIM, :], qg_col) * (HEAD_DIM ** -0.5)
        lg = _dot(mk_ref[h], qh.astype(BF16))
        m = jnp.max(lg, axis=0, keepdims=True)
        e = jnp.exp(lg - m)
        l = jnp.sum(e, axis=0, keepdims=True)
        outs.append(_dot(mvt_ref[h], e.astype(BF16)) / l)
    ymem_ref[...] = jnp.transpose(jnp.concatenate(outs, axis=0)).astype(ymem_ref.dtype)


def _bucket_np(d):
    n = np.maximum(d, 0)
    max_exact = REL_BUCKETS // 2
    nf = np.maximum(n, 1).astype(np.float64)
    large = max_exact + (np.log(nf / max_exact) / math.log(REL_MAX_DIST / max_exact)
                         * (REL_BUCKETS - max_exact)).astype(np.int32)
    large = np.minimum(large, REL_BUCKETS - 1)
    return np.where(n < max_exact, n, large).astype(np.int32)


def _bucket_tiles():
    r = np.arange(TQ)[:, None]
    k = np.arange(LANES)[None, :]
    last = REL_BUCKETS - 1
    assert _bucket_np(np.array(N_SEL_NEAR * LANES - (LANES - 1))) == last
    assert _bucket_np(np.array(N_CMP_NEAR * LANES - CMP_STRIDE * (LANES - 1) - (CMP_LEN - 1))) == last
    tiles = []
    for delta in range(N_SEL_NEAR):
        d = delta * LANES + r - k
        tiles.append(np.where(d >= 0, _bucket_np(d), -1).T)
    tiles.append(np.full((TQ, LANES), REL_BUCKETS - 1))
    tiles.append(np.full((TQ, LANES), -1))
    for delta in range(N_WIN_NEAR):
        d = delta * LANES + r - k
        tiles.append(np.where((d >= 0) & (d < WINDOW), _bucket_np(d), -1).T)
    while len(tiles) < CMP_BASE:
        tiles.append(np.full((TQ, LANES), -1))
    for delta in range(N_CMP_NEAR):
        d = delta * LANES + r - CMP_STRIDE * k - (CMP_LEN - 1)
        tiles.append(np.where(d >= 0, _bucket_np(d), -1).T)
    tiles.append(np.full((TQ, LANES), REL_BUCKETS - 1))
    tiles.append(np.full((TQ, LANES), -1))
    return np.stack(tiles).astype(np.int32)


def _bias_tables_kernel(tab_ref, ids_ref, out_ref):
    relative = pl.program_id(0) < CMP_BASE
    n_bits = REL_BUCKETS.bit_length() - 1
    for h in range(NSA_HEADS):
        sub = jnp.where(relative, tab_ref[REL_BUCKETS - 1, h], 0.0)
        leaves = [(jnp.full((SUBLANES, LANES), tab_ref[b, h], F32) - sub) * LOG2E for b in range(REL_BUCKETS)]
        for r0 in range(0, ids_ref.shape[0], SUBLANES):
            ids = ids_ref[r0:r0 + SUBLANES, :]
            level = leaves
            for k in range(n_bits):
                bit = jnp.bitwise_and(ids, 1 << k) != 0
                level = [jnp.where(bit, level[2 * j + 1], level[2 * j]) for j in range(len(level) // 2)]
            out_ref[h, r0:r0 + SUBLANES, :] = jnp.where(ids < 0, NEG, level[0])


def _bias_tables(rel_bias):
    ids = jnp.asarray(_bucket_tiles())
    return pl.pallas_call(
        _bias_tables_kernel,
        out_shape=jax.ShapeDtypeStruct((N_TILES, NSA_HEADS, TQ, LANES), F32),
        grid=(N_TILES,),
        in_specs=[
            pl.BlockSpec(memory_space=pltpu.SMEM),
            pl.BlockSpec((None, TQ, LANES), lambda t: (t, 0, 0)),
        ],
        out_specs=pl.BlockSpec((None, NSA_HEADS, TQ, LANES), lambda t: (t, 0, 0, 0)),
        compiler_params=_params(("arbitrary",)),
        name="bias_tables",
    )(rel_bias.astype(F32), ids)


def _mixer_a_kernel(x_ref, g_ref, w_ref, cw_ref, mk_ref, mvt_ref, qg_ref,
                    ytok_ref, ymem_ref, carry_ref, *, tm):
    @pl.when(pl.program_id(1) == 0)
    def _():
        carry_ref[...] = jnp.zeros_like(carry_ref)

    h = _rms(x_ref[...], g_ref[...]).astype(BF16)
    z = _dot(h, w_ref[...])
    gate_b = z[:, :CONV_CH]
    v = z[:, CONV_CH:2 * CONV_CH] * z[:, 2 * CONV_CH:3 * CONV_CH]
    prev = carry_ref[...]
    last1 = prev[SUBLANES - 1:SUBLANES, :]
    last2 = prev[SUBLANES - 2:SUBLANES - 1, :]
    row = lax.broadcasted_iota(jnp.int32, (tm, CONV_CH), 0)
    v1 = jnp.where(row == 0, last1, pltpu.roll(v, 1, 0))
    v2 = jnp.where(row == 0, last2, jnp.where(row == 1, last1, pltpu.roll(v, 2, 0)))
    cw = cw_ref[...]
    y = gate_b * (cw[0:1, :] * v2 + cw[1:2, :] * v1 + cw[2:3, :] * v)
    carry_ref[...] = v[tm - SUBLANES:, :]
    ytok_ref[...] = y.astype(ytok_ref.dtype)
    _mem_attn(z[:, 3 * CONV_CH:], mk_ref, mvt_ref, qg_ref[...], ymem_ref)


def _mixer_a(x, norm_g, w_in, conv_w, mk, mv_t, mem_q_gain):
    nb, s, d = x.shape
    ml = mk.shape[2]
    n_in = w_in.shape[1]
    return pl.pallas_call(
        functools.partial(_mixer_a_kernel, tm=TM),
        out_shape=(jax.ShapeDtypeStruct((nb, s, CONV_CH), BF16),
                   jax.ShapeDtypeStruct((nb, s, MEM_WIDTH), BF16)),
        grid=(nb, s // TM),
        in_specs=[
            pl.BlockSpec((None, TM, d), lambda b, i: (b, i, 0)),
            pl.BlockSpec((1, d), lambda b, i: (0, 0)),
            pl.BlockSpec((d, n_in), lambda b, i: (0, 0)),
            pl.BlockSpec((3, CONV_CH), lambda b, i: (0, 0)),
            pl.BlockSpec((None, MEM_HEADS, ml, HEAD_DIM), lambda b, i: (b, 0, 0, 0)),
            pl.BlockSpec((None, MEM_HEADS, HEAD_DIM, ml), lambda b, i: (b, 0, 0, 0)),
            pl.BlockSpec((HEAD_DIM, 1), lambda b, i: (0, 0)),
        ],
        out_specs=(
            pl.BlockSpec((None, TM, CONV_CH), lambda b, i: (b, i, 0)),
            pl.BlockSpec((None, TM, MEM_WIDTH), lambda b, i: (b, i, 0)),
        ),
        scratch_shapes=[pltpu.VMEM((SUBLANES, CONV_CH), F32)],
        compiler_params=_params(("arbitrary", "arbitrary")),
        name="mixer_a",
    )(x, norm_g.reshape(1, d), w_in.astype(BF16), conv_w, mk, mv_t, mem_q_gain.reshape(HEAD_DIM, 1))


B_GATE0 = NSA_HEADS * HEAD_DIM
B_QMEM0 = B_GATE0 + KV_GROUPS * LANES
B_WIDTH = B_QMEM0 + MEM_WIDTH


GATE_ROWS = 32


def _b_inproj_kernel(x_ref, g_ref, w_ref, qg_ref, mk_ref, mvt_ref, mqg_ref,
                     qt_ref, gates_ref, ymem_ref):
    h = _rms(x_ref[...], g_ref[...]).astype(BF16)
    z = _dot(h, w_ref[...])
    qg = qg_ref[...]
    tm = z.shape[0]
    for pair in range(NSA_HEADS // 2):
        z_t = jnp.transpose(z[:, pair * LANES:(pair + 1) * LANES])
        for half in range(2):
            qh_t = _rms_t(z_t[half * HEAD_DIM:(half + 1) * HEAD_DIM, :], qg) * (HEAD_DIM ** -0.5 * LOG2E)
            qh_t = qh_t.astype(qt_ref.dtype)
            g, hg = divmod(2 * pair + half, HPG)
            for qt in range(tm // TQ):
                lane0 = (qt * HPG + hg) * TQ
                qt_ref[g, :, lane0:lane0 + TQ] = qh_t[:, qt * TQ:(qt + 1) * TQ]
    for g in range(KV_GROUPS):
        gl = z[:, B_GATE0 + g * LANES:B_GATE0 + (g + 1) * LANES]
        gates_ref[g] = jnp.transpose(1.0 / (1.0 + jnp.exp(-gl)))[:GATE_ROWS, :]
    _mem_attn(z[:, B_QMEM0:], mk_ref, mvt_ref, mqg_ref[...], ymem_ref)


def _b_inproj(x, norm_g, w_in, q_gain, mk, mv_t, mem_q_gain):
    nb, s, d = x.shape
    ml = mk.shape[2]
    nq = NSA_HEADS * HEAD_DIM
    ng = HPG * N_BRANCH
    pad = jnp.zeros((d, LANES - ng), w_in.dtype)
    w = jnp.concatenate([w_in[:, :nq], w_in[:, nq:nq + ng], pad,
                         w_in[:, nq + ng:nq + 2 * ng], pad, w_in[:, nq + 2 * ng:]], axis=1)
    return pl.pallas_call(
        _b_inproj_kernel,
        out_shape=(jax.ShapeDtypeStruct((nb, KV_GROUPS, HEAD_DIM, HPG * s), BF16),
                   jax.ShapeDtypeStruct((nb, KV_GROUPS, GATE_ROWS, s), F32),
                   jax.ShapeDtypeStruct((nb, s, MEM_WIDTH), BF16)),
        grid=(nb, s // TM),
        in_specs=[
            pl.BlockSpec((None, TM, d), lambda b, i: (b, i, 0)),
            pl.BlockSpec((1, d), lambda b, i: (0, 0)),
            pl.BlockSpec((d, B_WIDTH), lambda b, i: (0, 0)),
            pl.BlockSpec((HEAD_DIM, 1), lambda b, i: (0, 0)),
            pl.BlockSpec((None, MEM_HEADS, ml, HEAD_DIM), lambda b, i: (b, 0, 0, 0)),
            pl.BlockSpec((None, MEM_HEADS, HEAD_DIM, ml), lambda b, i: (b, 0, 0, 0)),
            pl.BlockSpec((HEAD_DIM, 1), lambda b, i: (0, 0)),
        ],
        out_specs=(
            pl.BlockSpec((None, KV_GROUPS, HEAD_DIM, HPG * TM), lambda b, i: (b, 0, 0, i)),
            pl.BlockSpec((None, KV_GROUPS, GATE_ROWS, TM), lambda b, i: (b, 0, 0, i)),
            pl.BlockSpec((None, TM, MEM_WIDTH), lambda b, i: (b, i, 0)),
        ),
        compiler_params=_params(("arbitrary", "arbitrary")),
        name="b_inproj",
    )(x, norm_g.reshape(1, d), w.astype(BF16), q_gain.reshape(HEAD_DIM, 1), mk, mv_t,
      mem_q_gain.reshape(HEAD_DIM, 1))


FF_CHUNK = 1024
TM_POST = 1024


def _post_kernel(x_ref, ytok_ref, ymem_ref, woa_ref, wob_ref, g_ref, wup_ref, wdn_ref, o_ref, hm_ref):
    @pl.when(pl.program_id(1) == 0)
    def _():
        x1 = x_ref[...] + _dot(ytok_ref[...], woa_ref[...]) + _dot(ymem_ref[...], wob_ref[...])
        o_ref[...] = x1
        hm_ref[...] = _rms(x1, g_ref[...]).astype(hm_ref.dtype)

    a = _dot(hm_ref[...], wup_ref[...].astype(BF16))
    a = jnp.square(jnp.maximum(a, 0.0)).astype(BF16)
    o_ref[...] += _dot(a, wdn_ref[...].astype(BF16))


def _post(x, y_tok, y_mem, w_out, norm_g, w_up, w_down, layer):
    n, d = x.shape
    ff = w_up.shape[2]
    nt = y_tok.shape[1]
    assert nt % MEM_WIDTH == 0
    tm = min(TM_POST, n)
    const = lambda i, c: (0, 0)
    rows = lambda i, c: (i, 0)
    return pl.pallas_call(
        _post_kernel,
        out_shape=jax.ShapeDtypeStruct((n, d), F32),
        grid=(n // tm, ff // FF_CHUNK),
        in_specs=[
            pl.BlockSpec((tm, d), rows),
            pl.BlockSpec((tm, nt), rows),
            pl.BlockSpec((tm, MEM_WIDTH), rows),
            pl.BlockSpec((None, nt, d), lambda i, c: (layer, 0, 0)),
            pl.BlockSpec((None, MEM_WIDTH, d), lambda i, c: (layer, nt // MEM_WIDTH, 0)),
            pl.BlockSpec((1, d), const),
            pl.BlockSpec((None, d, FF_CHUNK), lambda i, c: (layer, 0, c)),
            pl.BlockSpec((None, FF_CHUNK, d), lambda i, c: (layer, c, 0)),
        ],
        out_specs=pl.BlockSpec((tm, d), rows),
        scratch_shapes=[pltpu.VMEM((tm, d), BF16)],
        compiler_params=_params(("arbitrary", "arbitrary")),
        name="post",
    )(x, y_tok, y_mem, w_out, w_out, norm_g.reshape(1, d), w_up, w_down)


K_AUG = 2 * HEAD_DIM
KS_AUG = K_AUG + SEL_LANES
V_AUG = HEAD_DIM + 16


def _build_kv_kernel(x_ref, g_ref, w_ref, kg_ref, ucmp_ref, ks_ref, vs_ref, kw_ref, vw_ref, raw_scr):
    h = _rms(x_ref[...], g_ref[...]).astype(BF16)
    kv = _dot(h, w_ref[...])
    gw = KV_GROUPS * HEAD_DIM
    tm = kv.shape[0]
    for ty in range(2):
        raw_scr[ty] = kv[:, ty * gw:(ty + 1) * gw]
        for t in range(CMP_STRIDE):
            rows = raw_scr[ty, pl.ds(t, tm // CMP_STRIDE, stride=CMP_STRIDE), :]
            for g in range(KV_GROUPS):
                ucmp_ref[ty, g, :, t * HEAD_DIM:(t + 1) * HEAD_DIM] = (
                    rows[:, g * HEAD_DIM:(g + 1) * HEAD_DIM].astype(ucmp_ref.dtype))
    kg = kg_ref[...]
    ones = (lax.broadcasted_iota(jnp.int32, (tm, K_AUG - HEAD_DIM), 1) < 2).astype(ks_ref.dtype)
    pos = pl.program_id(1) * tm + lax.broadcasted_iota(jnp.int32, (tm, SEL_LANES), 0)
    blk = lax.broadcasted_iota(jnp.int32, (tm, SEL_LANES), 1)
    onehot = jnp.where(blk == jnp.right_shift(pos, SEL_SHIFT), MASK_BIG, 0.0).astype(ks_ref.dtype)
    ones_row = (lax.broadcasted_iota(jnp.int32, (V_AUG - HEAD_DIM, tm), 0) == 0).astype(vs_ref.dtype)
    for g in range(KV_GROUPS):
        sl = lambda t: kv[:, t * gw + g * HEAD_DIM:t * gw + (g + 1) * HEAD_DIM]
        ks_ref[g, :, :HEAD_DIM] = _rms(sl(2), kg[1:2, :]).astype(ks_ref.dtype)
        ks_ref[g, :, HEAD_DIM:K_AUG] = ones
        ks_ref[g, :, K_AUG:] = onehot
        vs_ref[g, :HEAD_DIM, :] = jnp.transpose(sl(3)).astype(vs_ref.dtype)
        vs_ref[g, HEAD_DIM:, :] = ones_row
        kw_ref[g, :, :HEAD_DIM] = _rms(sl(4), kg[2:3, :]).astype(kw_ref.dtype)
        kw_ref[g, :, HEAD_DIM:] = ones
        vw_ref[g, :HEAD_DIM, :] = jnp.transpose(sl(5)).astype(vw_ref.dtype)
        vw_ref[g, HEAD_DIM:, :] = ones_row


def _build_kv(x, kv_norm, w_kv, k_gain):
    nb, s, d = x.shape
    gw = KV_GROUPS * HEAD_DIM
    ks = jax.ShapeDtypeStruct((nb, KV_GROUPS, s, KS_AUG), BF16)
    kw = jax.ShapeDtypeStruct((nb, KV_GROUPS, s, K_AUG), BF16)
    vt = jax.ShapeDtypeStruct((nb, KV_GROUPS, V_AUG, s), BF16)
    ks_spec = pl.BlockSpec((None, KV_GROUPS, TM, KS_AUG), lambda b, i: (b, 0, i, 0))
    kw_spec = pl.BlockSpec((None, KV_GROUPS, TM, K_AUG), lambda b, i: (b, 0, i, 0))
    v_spec = pl.BlockSpec((None, KV_GROUPS, V_AUG, TM), lambda b, i: (b, 0, 0, i))
    return pl.pallas_call(
        _build_kv_kernel,
        out_shape=(jax.ShapeDtypeStruct((2, nb, KV_GROUPS, s // CMP_STRIDE, CMP_STRIDE * HEAD_DIM), BF16),
                   ks, vt, kw, vt),
        grid=(nb, s // TM),
        in_specs=[
            pl.BlockSpec((None, TM, d), lambda b, i: (b, i, 0)),
            pl.BlockSpec((1, d), lambda b, i: (0, 0)),
            pl.BlockSpec((d, 6 * gw), lambda b, i: (0, 0)),
            pl.BlockSpec((N_BRANCH, HEAD_DIM), lambda b, i: (0, 0)),
        ],
        out_specs=(pl.BlockSpec((2, None, KV_GROUPS, TM // CMP_STRIDE, CMP_STRIDE * HEAD_DIM),
                                lambda b, i: (0, b, 0, i, 0)),
                   ks_spec, v_spec, kw_spec, v_spec),
        scratch_shapes=[pltpu.VMEM((2, TM, gw), F32)],
        compiler_params=_params(("arbitrary", "arbitrary")),
        name="build_kv",
    )(x, kv_norm.reshape(1, d), w_kv.astype(BF16), k_gain)


def _compress_kernel(u_ref, w1_ref, w2_ref, pos_ref, kg_ref, out_ref, out_t_ref, *, nc):
    half = CMP_STRIDE * HEAD_DIM
    u = u_ref[...]
    lo = _dot(u, w1_ref[:half, :])
    hi = _dot(u, w1_ref[half:, :])
    posb = _dot(jnp.broadcast_to(pos_ref[...], (8, 2 * half)).astype(BF16), w1_ref[...])[0:1, :]
    h1 = lo + pltpu.roll(hi, nc - 1, 0) + posb
    hid = 0.5 * h1 * (1.0 + jnp.tanh(math.sqrt(2.0 / math.pi) * (h1 + 0.044715 * (h1 * h1 * h1))))
    c = _dot(hid.astype(BF16), w2_ref[...])
    is_key = pl.program_id(0) == 0
    out = jnp.where(is_key, _rms(c, kg_ref[...]), c)
    out_ref[...] = out.astype(out_ref.dtype)
    out_t_ref[...] = jnp.transpose(out).astype(out_t_ref.dtype)


def _compress(u, cmp_pos, cmp_w1, cmp_w2, k_gain0):
    _, nb, _, nc, width = u.shape
    hidden = cmp_w1.shape[2]
    return pl.pallas_call(
        functools.partial(_compress_kernel, nc=nc),
        out_shape=(jax.ShapeDtypeStruct((2, nb, KV_GROUPS, nc, HEAD_DIM), BF16),
                   jax.ShapeDtypeStruct((2, nb, KV_GROUPS, HEAD_DIM, nc), BF16)),
        grid=(2, nb, KV_GROUPS),
        in_specs=[
            pl.BlockSpec((None, None, None, nc, width), lambda t, b, g: (t, b, g, 0, 0)),
            pl.BlockSpec((None, 2 * width, hidden), lambda t, b, g: (t, 0, 0)),
            pl.BlockSpec((None, hidden, HEAD_DIM), lambda t, b, g: (t, 0, 0)),
            pl.BlockSpec((None, 1, 2 * width), lambda t, b, g: (t, 0, 0)),
            pl.BlockSpec((1, HEAD_DIM), lambda t, b, g: (0, 0)),
        ],
        out_specs=(pl.BlockSpec((None, None, None, nc, HEAD_DIM), lambda t, b, g: (t, b, g, 0, 0)),
                   pl.BlockSpec((None, None, None, HEAD_DIM, nc), lambda t, b, g: (t, b, g, 0, 0))),
        compiler_params=_params(("arbitrary", "arbitrary", "arbitrary")),
        name="compress",
    )(u, cmp_w1.astype(BF16), cmp_w2.astype(BF16), cmp_pos.reshape(2, 1, 2 * width),
      k_gain0.reshape(1, HEAD_DIM))


def _overlap_np(nc, n_sel):
    c0 = np.arange(nc)[:, None] * CMP_STRIDE
    s0 = np.arange(SEL_LANES)[None, :] * SEL_BLOCK
    ov = np.minimum(c0 + CMP_LEN, s0 + SEL_BLOCK) - np.maximum(c0, s0)
    ov = np.clip(ov, 0, None).astype(np.float32) / CMP_LEN
    ov[nc - 1:, :] = 0.0
    ov[:, n_sel:] = 0.0
    return ov


def _select_topk(score, ntop):
    blk = lax.broadcasted_iota(jnp.int32, score.shape, 0).astype(F32)
    for _ in range(ntop):
        m = jnp.max(score, axis=0, keepdims=True)
        first = jnp.min(jnp.where(score == m, blk, float(SEL_LANES)), axis=0, keepdims=True)
        score = jnp.where(blk == first, -jnp.inf, score)
    return score == -jnp.inf


CMP_QSUB = 4
TOPK_GROUP = 32


def _cmp_attn_kernel(*refs, nj, n_sel):
    qt_ref, kc_ref, vct_ref, ovt_ref = refs[:4]
    tab_refs = refs[4:4 + nj * CMP_QSUB]
    oct_ref, unsel_ref = refs[4 + nj * CMP_QSUB:]
    i = pl.program_id(2)
    nq = CMP_QSUB * TQ
    def attend(nk):
        rows = nk * LANES
        s = _dot(kc_ref[:rows, :], qt_ref[...])
        s = jnp.concatenate(
            [jnp.concatenate([s[jt * LANES:(jt + 1) * LANES, (qs * HPG + h) * TQ:(qs * HPG + h + 1) * TQ]
                              + tab_refs[qs * nj + jt][h]
                              for qs in range(CMP_QSUB) for h in range(HPG)], axis=1) for jt in range(nk)], axis=0)
        m = jnp.maximum(jnp.max(s, axis=0, keepdims=True), 0.5 * NEG)
        e = jnp.exp2(s - m)
        l = jnp.sum(e, axis=0, keepdims=True)
        p = e * (1.0 / jnp.where(l == 0.0, 1.0, l))
        oct_ref[...] = _dot(vct_ref[:, :rows], p.astype(BF16))

        psums = []
        for qs in range(CMP_QSUB):
            acc = p[:, qs * HPG * TQ:(qs * HPG + 1) * TQ]
            for h in range(1, HPG):
                acc = acc + p[:, (qs * HPG + h) * TQ:(qs * HPG + h + 1) * TQ]
            psums.append(acc)
        psum = jnp.concatenate(psums, axis=1)
        p_hi = psum.astype(BF16)
        p_lo = (psum - p_hi.astype(F32)).astype(BF16)
        return _dot(ovt_ref[:, :rows], p_hi) + _dot(ovt_ref[:, :rows], p_lo)

    last_visible = (nq * (i + 1) - CMP_LEN) // CMP_STRIDE
    imp_t = lax.switch(last_visible // LANES, [functools.partial(attend, nk) for nk in range(1, nj + 1)])

    tq = i * nq + lax.broadcasted_iota(jnp.int32, (SEL_LANES, nq), 1)
    j = lax.broadcasted_iota(jnp.int32, (SEL_LANES, nq), 0)
    back = jnp.right_shift(tq, SEL_SHIFT) - j
    forced = (j == 0) | ((back >= 0) & (back < N_LOCAL_SEL))
    causal = back >= 0
    n_free = min(SEL_TOPK, n_sel) - (N_LOCAL_SEL + 1)
    score = jnp.where(causal & ~forced, imp_t, NEG)

    def pick(n_groups):
        rows = n_groups * TOPK_GROUP
        top = _select_topk(score[:rows, :], n_free).astype(F32)
        if rows == SEL_LANES:
            return top
        return jnp.concatenate([top, jnp.zeros((SEL_LANES - rows, nq), F32)], axis=0)

    causal_blocks = (nq * (i + 1)) // SEL_BLOCK
    picked = lax.switch((causal_blocks - 1) // TOPK_GROUP,
                        [functools.partial(pick, n) for n in range(1, SEL_LANES // TOPK_GROUP + 1)])
    sel = causal & (forced | (picked > 0.0))
    unsel_ref[...] = jnp.where(sel, 0.0, -1.0).astype(unsel_ref.dtype)


def _cmp_attn(q_t, kvc, kvc_t, tables):
    nb = q_t.shape[0]
    s = q_t.shape[3] // HPG
    nc = s // CMP_STRIDE
    nj = nc // LANES
    n_sel = s // SEL_BLOCK
    ov_t = jnp.asarray(_overlap_np(nc, n_sel).T, dtype=BF16)

    def tab_map(qs, jt):
        def index_map(b, g, i):
            didx = CMP_QSUB * i + qs - (LANES * CMP_STRIDE // TQ) * jt
            row = jnp.where(didx < 0, CMP_MASKED, jnp.minimum(didx, CMP_CONST))
            return (CMP_BASE + row, g, 0, 0)
        return index_map

    nq = CMP_QSUB * TQ
    qt_spec = pl.BlockSpec((None, None, HEAD_DIM, HPG * nq), lambda b, g, i: (b, g, 0, i))
    n_tabs = CMP_QSUB * nj
    return pl.pallas_call(
        functools.partial(_cmp_attn_kernel, nj=nj, n_sel=n_sel),
        out_shape=(jax.ShapeDtypeStruct((nb, KV_GROUPS, HEAD_DIM, HPG * s), F32),
                   jax.ShapeDtypeStruct((nb, KV_GROUPS, SEL_LANES, s), BF16)),
        grid=(nb, KV_GROUPS, s // nq),
        in_specs=[
            qt_spec,
            pl.BlockSpec((None, None, None, nc, HEAD_DIM), lambda b, g, i: (0, b, g, 0, 0)),
            pl.BlockSpec((None, None, None, HEAD_DIM, nc), lambda b, g, i: (1, b, g, 0, 0)),
            pl.BlockSpec((SEL_LANES, nc), lambda b, g, i: (0, 0)),
        ] + [pl.BlockSpec((None, HPG, LANES, TQ), tab_map(qs, jt))
             for qs in range(CMP_QSUB) for jt in range(nj)],
        out_specs=(
            qt_spec,
            pl.BlockSpec((None, None, SEL_LANES, nq), lambda b, g, i: (b, g, 0, i)),
        ),
        compiler_params=_params(("arbitrary", "arbitrary", "arbitrary")),
        name="cmp_attn",
    )(q_t, kvc, kvc_t, ov_t, *([tables] * n_tabs))


class _FlashStream:
    def __init__(self, q_aug, k_ref, vt_ref, bias_fn, s_scr, p_scr):
        self.q_aug, self.k_ref, self.vt_ref, self.bias_fn = q_aug, k_ref, vt_ref, bias_fn
        self.s_scr, self.p_scr = s_scr, p_scr

    def _scores(self, t):
        k0 = pl.multiple_of(jnp.maximum(t, 0) * TK_SEL, TK_SEL)
        return self.bias_fn(t, _dot(self.k_ref[pl.ds(k0, TK_SEL), :], self.q_aug))

    def _pv(self, t, slot):
        k0 = pl.multiple_of(jnp.maximum(t, 0) * TK_SEL, TK_SEL)
        return _dot(self.vt_ref[:, pl.ds(k0, TK_SEL)], self.p_scr[slot])

    def _softmax(self, slot, m):
        s = self.s_scr[slot]
        m_new = jnp.maximum(m, jnp.max(s, axis=0, keepdims=True))
        self.p_scr[slot] = jnp.exp2(s - m_new).astype(self.p_scr.dtype)
        return m_new, jnp.exp2(m - m_new)

    def start(self, a, first_trip_static=False):
        lanes = self.q_aug.shape[1]
        self.s_scr[0] = self._scores(a)
        if not first_trip_static:
            self.p_scr[1] = jnp.zeros(self.p_scr.shape[1:], self.p_scr.dtype)
        return (jnp.full((1, lanes), NEG, F32), jnp.ones((1, lanes), F32), jnp.zeros((V_AUG, lanes), F32))

    def trip(self, a, carry, has_next, is_first=False):
        m, alpha, acc = carry
        if not is_first:
            acc = alpha * acc + self._pv(a - 1, 1)
        m, alpha = self._softmax(0, m)
        self.s_scr[1] = self._scores(a + 1)
        acc = alpha * acc + self._pv(a, 0)
        m, alpha = self._softmax(1, m)
        if has_next:
            self.s_scr[0] = self._scores(a + 2)
        return m, alpha, acc

    @staticmethod
    def _normalised(acc):
        return acc[:HEAD_DIM, :] / acc[HEAD_DIM:HEAD_DIM + 1, :]

    def finish_pair(self, a, carry):
        _, alpha, acc = carry
        return self._normalised(alpha * acc + self._pv(a + 1, 1))

    def finish_single(self, a, carry):
        m, alpha, acc = carry
        acc = alpha * acc + self._pv(a - 1, 1)
        _, alpha = self._softmax(0, m)
        return self._normalised(alpha * acc + self._pv(a, 0))


def _add_tiles(s, tile_fn):
    rows = []
    for part in range(TK_SEL // LANES):
        cols = [s[part * LANES:(part + 1) * LANES, h * TQ:(h + 1) * TQ] + tile_fn(part, h)
                for h in range(HPG)]
        rows.append(jnp.concatenate(cols, axis=1))
    return jnp.concatenate(rows, axis=0)


def _sel_win_kernel(qt_ref, ks_ref, vs_ref, kw_ref, vw_ref, unsel_ref, tab_ref, cvec_ref,
                    oct_ref, gates_ref, y_ref, ss_scr, ps_scr, sw_scr, pw_scr):
    i = pl.program_id(2)
    parts = TK_SEL // LANES

    c = cvec_ref[...]
    c_hi = c.astype(BF16).astype(F32)
    c_lo = c - c_hi
    arow = lax.broadcasted_iota(jnp.int32, (K_AUG - HEAD_DIM, HPG * TQ), 0)
    aug = jnp.where(arow == 0, c_hi, jnp.where(arow == 1, c_lo, 0.0)).astype(BF16)
    q_win = jnp.concatenate([qt_ref[...], aug], axis=0)
    unsel_t = unsel_ref[...]
    q_sel = jnp.concatenate([q_win, jnp.concatenate([unsel_t] * HPG, axis=1)], axis=0)

    def sel_bias(t, s):
        def tile(part, h):
            delta = i - parts * t - part
            row = jnp.where(delta < 0, SEL_MASKED, jnp.minimum(delta, SEL_CONST))
            return tab_ref[row, h]

        return _add_tiles(s, tile)

    def win_bias(t, s):
        def tile(part, h):
            delta = i - parts * t - part
            hidden = (delta < 0) | (delta >= N_WIN_NEAR) | (t < 0)
            return tab_ref[jnp.where(hidden, SEL_MASKED, WIN_BASE + delta), h]

        return _add_tiles(s, tile)

    n_tiles = i // parts + 1
    n_pairs = (n_tiles + 1) // 2
    sel = _FlashStream(q_sel, ks_ref, vs_ref, sel_bias, ss_scr, ps_scr)
    win = _FlashStream(q_win, kw_ref, vw_ref, win_bias, sw_scr, pw_scr)
    win_a0 = n_tiles - WIN_TILES
    last_pair = 2 * (n_pairs - 1)
    carry_s = sel.start(0)
    carry_w = win.start(win_a0, first_trip_static=True)
    carry_s = lax.fori_loop(0, n_pairs - 1, lambda u, c: sel.trip(2 * u, c, has_next=True), carry_s)
    carry_s = sel.trip(last_pair, carry_s, has_next=False)
    carry_w = win.trip(win_a0, carry_w, has_next=True, is_first=True)
    os_t = sel.finish_pair(last_pair, carry_s)
    ow_t = win.finish_single(win_a0 + WIN_TILES - 1, carry_w)

    oc_t = oct_ref[...]
    gt = gates_ref[...]
    for hg in range(HPG):
        c0 = N_BRANCH * hg
        lanes = slice(hg * TQ, (hg + 1) * TQ)
        out_t = (gt[c0:c0 + 1, :] * oc_t[:, lanes] + gt[c0 + 1:c0 + 2, :] * os_t[:, lanes]
                 + gt[c0 + 2:c0 + 3, :] * ow_t[:, lanes])
        y_ref[:, hg * HEAD_DIM:(hg + 1) * HEAD_DIM] = jnp.transpose(out_t).astype(y_ref.dtype)


def _sel_win_attn(q_t, ks, vs_t, kw, vw_t, unsel, tables, cvec, oc_t, gates_t):
    nb = q_t.shape[0]
    s = q_t.shape[3] // HPG
    ks_spec = pl.BlockSpec((None, None, s, KS_AUG), lambda b, g, i: (b, g, 0, 0))
    kw_spec = pl.BlockSpec((None, None, s, K_AUG), lambda b, g, i: (b, g, 0, 0))
    v_spec = pl.BlockSpec((None, None, V_AUG, s), lambda b, g, i: (b, g, 0, 0))
    qt_spec = pl.BlockSpec((None, None, HEAD_DIM, HPG * TQ), lambda b, g, i: (b, g, 0, i))
    gw = HPG * HEAD_DIM
    return pl.pallas_call(
        _sel_win_kernel,
        out_shape=jax.ShapeDtypeStruct((nb, s, NSA_HEADS * HEAD_DIM), BF16),
        grid=(nb, KV_GROUPS, s // TQ),
        in_specs=[
            qt_spec,
            ks_spec, v_spec, kw_spec, v_spec,
            pl.BlockSpec((None, None, SEL_LANES, TQ), lambda b, g, i: (b, g, 0, i)),
            pl.BlockSpec((CMP_BASE, HPG, LANES, TQ), lambda b, g, i: (0, g, 0, 0)),
            pl.BlockSpec((None, 1, HPG * TQ), lambda b, g, i: (g, 0, 0)),
            qt_spec,
            pl.BlockSpec((None, None, GATE_ROWS, TQ), lambda b, g, i: (b, g, 0, i)),
        ],
        out_specs=pl.BlockSpec((None, TQ, gw), lambda b, g, i: (b, i, g)),
        scratch_shapes=[pltpu.VMEM((2, TK_SEL, HPG * TQ), F32), pltpu.VMEM((2, TK_SEL, HPG * TQ), BF16),
                        pltpu.VMEM((2, TK_SEL, HPG * TQ), F32), pltpu.VMEM((2, TK_SEL, HPG * TQ), BF16)],
        compiler_params=_params(("arbitrary", "arbitrary", "arbitrary")),
        name="sel_win_attn",
    )(q_t, ks, vs_t, kw, vw_t, unsel, tables, cvec, oc_t, gates_t)


def kernel(x, mem, mix_norm, a_w_in, a_conv_w, b_w_in, b_q_gain, kv_norm, w_kv_shared, k_gain,
           cmp_pos, cmp_w1, cmp_w2, rel_bias, mem_norm, mem_w_kv, mem_q_gain, mem_k_gain, w_out,
           mlp_norm, w_up, w_down):
    nb, s, d = x.shape
    depth = mix_norm.shape[0]
    n_a = a_w_in.shape[0]
    assert s % (LANES * CMP_STRIDE) == 0 and s // SEL_BLOCK <= SEL_LANES
    assert WIN_BASE + N_WIN_NEAR <= CMP_BASE
    assert WIN_TILES == 3

    mk, mv_t = _mem_kv(mem, mem_norm, mem_w_kv, mem_k_gain)
    tables = _bias_tables(rel_bias)
    cvec = jnp.repeat(rel_bias[REL_BUCKETS - 1].astype(F32) * LOG2E, TQ).reshape(KV_GROUPS, 1, HPG * TQ)
    w_out_bf = w_out.astype(BF16)
    shared = None
    for layer in range(depth):
        if layer < n_a:
            y_tok, y_mem = _mixer_a(x, mix_norm[layer], a_w_in[layer], a_conv_w[layer],
                                    mk[layer], mv_t[layer], mem_q_gain[layer])
        else:
            j = layer - n_a
            ks, vs_t, kw, vw_t, kvc, kvc_t = shared
            q_t, gates_t, y_mem = _b_inproj(x, mix_norm[layer], b_w_in[j], b_q_gain[j],
                                            mk[layer], mv_t[layer], mem_q_gain[layer])
            oc_t, unsel = _cmp_attn(q_t, kvc, kvc_t, tables)
            y_tok = _sel_win_attn(q_t, ks, vs_t, kw, vw_t, unsel, tables, cvec, oc_t, gates_t)
        x = _post(x.reshape(nb * s, d), y_tok.reshape(nb * s, -1), y_mem.reshape(nb * s, -1),
                  w_out_bf, mlp_norm[layer], w_up, w_down, layer).reshape(nb, s, d)
        if layer == n_a - 1:
            ucmp, ks, vs_t, kw, vw_t = _build_kv(x, kv_norm, w_kv_shared, k_gain)
            kvc, kvc_t = _compress(ucmp, cmp_pos, cmp_w1, cmp_w2, k_gain[0])
            shared = (ks, vs_t, kw, vw_t, kvc, kvc_t)
    return x
```

```python
import functools
import math

import numpy as np
import jax
import jax.numpy as jnp
from jax import lax
from jax.experimental import pallas as pl
from jax.experimental.pallas import tpu as pltpu

F32 = jnp.float32
BF16 = jnp.bfloat16

HEAD_DIM = 64
MEM_HEADS = 4
MEM_WIDTH = MEM_HEADS * HEAD_DIM
CONV_CH = 768
NSA_HEADS = 12
KV_GROUPS = 2
HPG = NSA_HEADS // KV_GROUPS
N_BRANCH = 3
CMP_LEN = 32
CMP_STRIDE = 16
SEL_BLOCK = 64
SEL_SHIFT = SEL_BLOCK.bit_length() - 1
SEL_TOPK = 16
N_LOCAL_SEL = 2
WINDOW = 512
REL_BUCKETS = 32
REL_MAX_DIST = 1024
EPS = 1e-6
NEG = -1e30
LOG2E = math.log2(math.e)
MASK_BIG = 2.0 ** 100

LANES = 128
SUBLANES = 8
VMEM_LIMIT_BYTES = 56 * 1024 * 1024

TQ = 128
TK_SEL = 256
SEL_LANES = 128
TM = 1024
WIN_TILES = max(((TQ * i - (WINDOW - 1)) % TK_SEL + WINDOW + TQ - 2) // TK_SEL + 1 for i in range(TK_SEL))

N_SEL_NEAR = 8
SEL_CONST = N_SEL_NEAR
SEL_MASKED = N_SEL_NEAR + 1
WIN_BASE = N_SEL_NEAR + 2
N_WIN_NEAR = WINDOW // LANES + 1
CMP_BASE = 16
N_CMP_NEAR = 23
CMP_CONST = N_CMP_NEAR
CMP_MASKED = N_CMP_NEAR + 1
N_CMP_TILES = N_CMP_NEAR + 2
N_TILES = CMP_BASE + N_CMP_TILES


def _rms(xf, g):
    ms = jnp.mean(xf * xf, axis=-1, keepdims=True)
    return xf * lax.rsqrt(ms + EPS) * g


def _dot(a, b):
    return jnp.dot(a, b, preferred_element_type=F32)


def _params(sem):
    return pltpu.CompilerParams(dimension_semantics=sem, vmem_limit_bytes=VMEM_LIMIT_BYTES)


def _mem_kv_kernel(mem_ref, mnorm_ref, w_ref, kg_ref, mk_ref, mvt_ref, *, nb, ml):
    mh = _rms(mem_ref[...], mnorm_ref[...]).astype(BF16)
    kv = _dot(mh, w_ref[...])
    for h in range(MEM_HEADS):
        kh = _rms(kv[:, h * HEAD_DIM:(h + 1) * HEAD_DIM], kg_ref[...]).astype(BF16)
        vh = kv[:, MEM_WIDTH + h * HEAD_DIM:MEM_WIDTH + (h + 1) * HEAD_DIM]
        for b in range(nb):
            mk_ref[b, h] = kh[b * ml:(b + 1) * ml]
            mvt_ref[b, h] = jnp.transpose(vh[b * ml:(b + 1) * ml]).astype(BF16)


def _mem_kv(mem, mem_norm, mem_w_kv, mem_k_gain):
    nb, ml, d = mem.shape
    depth = mem_w_kv.shape[0]
    out = jax.ShapeDtypeStruct((depth, nb, MEM_HEADS, ml, HEAD_DIM), BF16)
    out_t = jax.ShapeDtypeStruct((depth, nb, MEM_HEADS, HEAD_DIM, ml), BF16)
    return pl.pallas_call(
        functools.partial(_mem_kv_kernel, nb=nb, ml=ml),
        out_shape=(out, out_t),
        grid=(depth,),
        in_specs=[
            pl.BlockSpec((nb * ml, d), lambda l: (0, 0)),
            pl.BlockSpec((1, d), lambda l: (0, 0)),
            pl.BlockSpec((None, d, 2 * MEM_WIDTH), lambda l: (l, 0, 0)),
            pl.BlockSpec((None, 1, HEAD_DIM), lambda l: (l, 0, 0)),
        ],
        out_specs=(
            pl.BlockSpec((None, nb, MEM_HEADS, ml, HEAD_DIM), lambda l: (l, 0, 0, 0, 0)),
            pl.BlockSpec((None, nb, MEM_HEADS, HEAD_DIM, ml), lambda l: (l, 0, 0, 0, 0)),
        ),
        compiler_params=_params(("arbitrary",)),
        name="mem_kv",
    )(mem.reshape(nb * ml, d), mem_norm.reshape(1, d), mem_w_kv.astype(BF16),
      mem_k_gain.reshape(depth, 1, HEAD_DIM))


def _rms_t(xt, g_col):
    ms = jnp.mean(xt * xt, axis=0, keepdims=True)
    return xt * lax.rsqrt(ms + EPS) * g_col


def _mem_attn(qm, mk_ref, mvt_ref, qg_col, ymem_ref):
    qm_t = jnp.transpose(qm)
    outs = []
    for h in range(MEM_HEADS):
        qh = _rms_t(qm_t[h * HEAD_DIM:(h + 1) * HEAD_DIM, :], qg_col) * (HEAD_DIM ** -0.5)
        lg = _dot(mk_ref[h], qh.astype(BF16))
        m = jnp.max(lg, axis=0, keepdims=True)
        e = jnp.exp(lg - m)
        l = jnp.sum(e, axis=0, keepdims=True)
        outs.append(_dot(mvt_ref[h], e.astype(BF16)) / l)
    ymem_ref[...] = jnp.transpose(jnp.concatenate(outs, axis=0)).astype(ymem_ref.dtype)


def _bucket_np(d):
    n = np.maximum(d, 0)
    max_exact = REL_BUCKETS // 2
    nf = np.maximum(n, 1).astype(np.float64)
    large = max_exact + (np.log(nf / max_exact) / math.log(REL_MAX_DIST / max_exact)
                         * (REL_BUCKETS - max_exact)).astype(np.int32)
    large = np.minimum(large, REL_BUCKETS - 1)
    return np.where(n < max_exact, n, large).astype(np.int32)


def _bucket_tiles():
    r = np.arange(TQ)[:, None]
    k = np.arange(LANES)[None, :]
    last = REL_BUCKETS - 1
    assert _bucket_np(np.array(N_SEL_NEAR * LANES - (LANES - 1))) == last
    assert _bucket_np(np.array(N_CMP_NEAR * LANES - CMP_STRIDE * (LANES - 1) - (CMP_LEN - 1))) == last
    tiles = []
    for delta in range(N_SEL_NEAR):
        d = delta * LANES + r - k
        tiles.append(np.where(d >= 0, _bucket_np(d), -1).T)
    tiles.append(np.full((TQ, LANES), REL_BUCKETS - 1))
    tiles.append(np.full((TQ, LANES), -1))
    for delta in range(N_WIN_NEAR):
        d = delta * LANES + r - k
        tiles.append(np.where((d >= 0) & (d < WINDOW), _bucket_np(d), -1).T)
    while len(tiles) < CMP_BASE:
        tiles.append(np.full((TQ, LANES), -1))
    for delta in range(N_CMP_NEAR):
        d = delta * LANES + r - CMP_STRIDE * k - (CMP_LEN - 1)
        tiles.append(np.where(d >= 0, _bucket_np(d), -1).T)
    tiles.append(np.full((TQ, LANES), REL_BUCKETS - 1))
    tiles.append(np.full((TQ, LANES), -1))
    return np.stack(tiles).astype(np.int32)


def _bias_tables_kernel(tab_ref, ids_ref, out_ref):
    relative = pl.program_id(0) < CMP_BASE
    n_bits = REL_BUCKETS.bit_length() - 1
    for h in range(NSA_HEADS):
        sub = jnp.where(relative, tab_ref[REL_BUCKETS - 1, h], 0.0)
        leaves = [(jnp.full((SUBLANES, LANES), tab_ref[b, h], F32) - sub) * LOG2E for b in range(REL_BUCKETS)]
        for r0 in range(0, ids_ref.shape[0], SUBLANES):
            ids = ids_ref[r0:r0 + SUBLANES, :]
            level = leaves
            for k in range(n_bits):
                bit = jnp.bitwise_and(ids, 1 << k) != 0
                level = [jnp.where(bit, level[2 * j + 1], level[2 * j]) for j in range(len(level) // 2)]
            out_ref[h, r0:r0 + SUBLANES, :] = jnp.where(ids < 0, NEG, level[0])


def _bias_tables(rel_bias):
    ids = jnp.asarray(_bucket_tiles())
    return pl.pallas_call(
        _bias_tables_kernel,
        out_shape=jax.ShapeDtypeStruct((N_TILES, NSA_HEADS, TQ, LANES), F32),
        grid=(N_TILES,),
        in_specs=[
            pl.BlockSpec(memory_space=pltpu.SMEM),
            pl.BlockSpec((None, TQ, LANES), lambda t: (t, 0, 0)),
        ],
        out_specs=pl.BlockSpec((None, NSA_HEADS, TQ, LANES), lambda t: (t, 0, 0, 0)),
        compiler_params=_params(("arbitrary",)),
        name="bias_tables",
    )(rel_bias.astype(F32), ids)


def _mixer_a_kernel(x_ref, g_ref, w_ref, cw_ref, mk_ref, mvt_ref, qg_ref,
                    ytok_ref, ymem_ref, carry_ref, *, tm):
    @pl.when(pl.program_id(1) == 0)
    def _():
        carry_ref[...] = jnp.zeros_like(carry_ref)

    h = _rms(x_ref[...], g_ref[...]).astype(BF16)
    z = _dot(h, w_ref[...])
    gate_b = z[:, :CONV_CH]
    v = z[:, CONV_CH:2 * CONV_CH] * z[:, 2 * CONV_CH:3 * CONV_CH]
    prev = carry_ref[...]
    last1 = prev[SUBLANES - 1:SUBLANES, :]
    last2 = prev[SUBLANES - 2:SUBLANES - 1, :]
    row = lax.broadcasted_iota(jnp.int32, (tm, CONV_CH), 0)
    v1 = jnp.where(row == 0, last1, pltpu.roll(v, 1, 0))
    v2 = jnp.where(row == 0, last2, jnp.where(row == 1, last1, pltpu.roll(v, 2, 0)))
    cw = cw_ref[...]
    y = gate_b * (cw[0:1, :] * v2 + cw[1:2, :] * v1 + cw[2:3, :] * v)
    carry_ref[...] = v[tm - SUBLANES:, :]
    ytok_ref[...] = y.astype(ytok_ref.dtype)
    _mem_attn(z[:, 3 * CONV_CH:], mk_ref, mvt_ref, qg_ref[...], ymem_ref)


def _mixer_a(x, norm_g, w_in, conv_w, mk, mv_t, mem_q_gain):
    nb, s, d = x.shape
    ml = mk.shape[2]
    n_in = w_in.shape[1]
    return pl.pallas_call(
        functools.partial(_mixer_a_kernel, tm=TM),
        out_shape=(jax.ShapeDtypeStruct((nb, s, CONV_CH), BF16),
                   jax.ShapeDtypeStruct((nb, s, MEM_WIDTH), BF16)),
        grid=(nb, s // TM),
        in_specs=[
            pl.BlockSpec((None, TM, d), lambda b, i: (b, i, 0)),
            pl.BlockSpec((1, d), lambda b, i: (0, 0)),
            pl.BlockSpec((d, n_in), lambda b, i: (0, 0)),
            pl.BlockSpec((3, CONV_CH), lambda b, i: (0, 0)),
            pl.BlockSpec((None, MEM_HEADS, ml, HEAD_DIM), lambda b, i: (b, 0, 0, 0)),
            pl.BlockSpec((None, MEM_HEADS, HEAD_DIM, ml), lambda b, i: (b, 0, 0, 0)),
            pl.BlockSpec((HEAD_DIM, 1), lambda b, i: (0, 0)),
        ],
        out_specs=(
            pl.BlockSpec((None, TM, CONV_CH), lambda b, i: (b, i, 0)),
            pl.BlockSpec((None, TM, MEM_WIDTH), lambda b, i: (b, i, 0)),
        ),
        scratch_shapes=[pltpu.VMEM((SUBLANES, CONV_CH), F32)],
        compiler_params=_params(("arbitrary", "arbitrary")),
        name="mixer_a",
    )(x, norm_g.reshape(1, d), w_in.astype(BF16), conv_w, mk, mv_t, mem_q_gain.reshape(HEAD_DIM, 1))


B_GATE0 = NSA_HEADS * HEAD_DIM
B_QMEM0 = B_GATE0 + KV_GROUPS * LANES
B_WIDTH = B_QMEM0 + MEM_WIDTH


GATE_ROWS = 32


def _b_inproj_kernel(x_ref, g_ref, w_ref, qg_ref, mk_ref, mvt_ref, mqg_ref,
                     qt_ref, gates_ref, ymem_ref):
    h = _rms(x_ref[...], g_ref[...]).astype(BF16)
    z = _dot(h, w_ref[...])
    qg = qg_ref[...]
    tm = z.shape[0]
    for pair in range(NSA_HEADS // 2):
        z_t = jnp.transpose(z[:, pair * LANES:(pair + 1) * LANES])
        for half in range(2):
            qh_t = _rms_t(z_t[half * HEAD_DIM:(half + 1) * HEAD_DIM, :], qg) * (HEAD_DIM ** -0.5 * LOG2E)
            qh_t = qh_t.astype(qt_ref.dtype)
            g, hg = divmod(2 * pair + half, HPG)
            for qt in range(tm // TQ):
                lane0 = (qt * HPG + hg) * TQ
                qt_ref[g, :, lane0:lane0 + TQ] = qh_t[:, qt * TQ:(qt + 1) * TQ]
    for g in range(KV_GROUPS):
        gl = z[:, B_GATE0 + g * LANES:B_GATE0 + (g + 1) * LANES]
        gates_ref[g] = jnp.transpose(1.0 / (1.0 + jnp.exp(-gl)))[:GATE_ROWS, :]
    _mem_attn(z[:, B_QMEM0:], mk_ref, mvt_ref, mqg_ref[...], ymem_ref)


def _b_inproj(x, norm_g, w_in, q_gain, mk, mv_t, mem_q_gain):
    nb, s, d = x.shape
    ml = mk.shape[2]
    nq = NSA_HEADS * HEAD_DIM
    ng = HPG * N_BRANCH
    pad = jnp.zeros((d, LANES - ng), w_in.dtype)
    w = jnp.concatenate([w_in[:, :nq], w_in[:, nq:nq + ng], pad,
                         w_in[:, nq + ng:nq + 2 * ng], pad, w_in[:, nq + 2 * ng:]], axis=1)
    return pl.pallas_call(
        _b_inproj_kernel,
        out_shape=(jax.ShapeDtypeStruct((nb, KV_GROUPS, HEAD_DIM, HPG * s), BF16),
                   jax.ShapeDtypeStruct((nb, KV_GROUPS, GATE_ROWS, s), F32),
                   jax.ShapeDtypeStruct((nb, s, MEM_WIDTH), BF16)),
        grid=(nb, s // TM),
        in_specs=[
            pl.BlockSpec((None, TM, d), lambda b, i: (b, i, 0)),
            pl.BlockSpec((1, d), lambda b, i: (0, 0)),
            pl.BlockSpec((d, B_WIDTH), lambda b, i: (0, 0)),
            pl.BlockSpec((HEAD_DIM, 1), lambda b, i: (0, 0)),
            pl.BlockSpec((None, MEM_HEADS, ml, HEAD_DIM), lambda b, i: (b, 0, 0, 0)),
            pl.BlockSpec((None, MEM_HEADS, HEAD_DIM, ml), lambda b, i: (b, 0, 0, 0)),
            pl.BlockSpec((HEAD_DIM, 1), lambda b, i: (0, 0)),
        ],
        out_specs=(
            pl.BlockSpec((None, KV_GROUPS, HEAD_DIM, HPG * TM), lambda b, i: (b, 0, 0, i)),
            pl.BlockSpec((None, KV_GROUPS, GATE_ROWS, TM), lambda b, i: (b, 0, 0, i)),
            pl.BlockSpec((None, TM, MEM_WIDTH), lambda b, i: (b, i, 0)),
        ),
        compiler_params=_params(("arbitrary", "arbitrary")),
        name="b_inproj",
    )(x, norm_g.reshape(1, d), w.astype(BF16), q_gain.reshape(HEAD_DIM, 1), mk, mv_t,
      mem_q_gain.reshape(HEAD_DIM, 1))


FF_CHUNK = 1024
TM_POST = 1024


def _post_kernel(x_ref, ytok_ref, ymem_ref, woa_ref, wob_ref, g_ref, wup_ref, wdn_ref, o_ref, hm_ref):
    @pl.when(pl.program_id(1) == 0)
    def _():
        x1 = x_ref[...] + _dot(ytok_ref[...], woa_ref[...]) + _dot(ymem_ref[...], wob_ref[...])
        o_ref[...] = x1
        hm_ref[...] = _rms(x1, g_ref[...]).astype(hm_ref.dtype)

    a = _dot(hm_ref[...], wup_ref[...].astype(BF16))
    a = jnp.square(jnp.maximum(a, 0.0)).astype(BF16)
    o_ref[...] += _dot(a, wdn_ref[...].astype(BF16))


def _post(x, y_tok, y_mem, w_out, norm_g, w_up, w_down, layer):
    n, d = x.shape
    ff = w_up.shape[2]
    nt = y_tok.shape[1]
    assert nt % MEM_WIDTH == 0
    tm = min(TM_POST, n)
    const = lambda i, c: (0, 0)
    rows = lambda i, c: (i, 0)
    return pl.pallas_call(
        _post_kernel,
        out_shape=jax.ShapeDtypeStruct((n, d), F32),
        grid=(n // tm, ff // FF_CHUNK),
        in_specs=[
            pl.BlockSpec((tm, d), rows),
            pl.BlockSpec((tm, nt), rows),
            pl.BlockSpec((tm, MEM_WIDTH), rows),
            pl.BlockSpec((None, nt, d), lambda i, c: (layer, 0, 0)),
            pl.BlockSpec((None, MEM_WIDTH, d), lambda i, c: (layer, nt // MEM_WIDTH, 0)),
            pl.BlockSpec((1, d), const),
            pl.BlockSpec((None, d, FF_CHUNK), lambda i, c: (layer, 0, c)),
            pl.BlockSpec((None, FF_CHUNK, d), lambda i, c: (layer, c, 0)),
        ],
        out_specs=pl.BlockSpec((tm, d), rows),
        scratch_shapes=[pltpu.VMEM((tm, d), BF16)],
        compiler_params=_params(("arbitrary", "arbitrary")),
        name="post",
    )(x, y_tok, y_mem, w_out, w_out, norm_g.reshape(1, d), w_up, w_down)


K_AUG = 2 * HEAD_DIM
KS_AUG = K_AUG + SEL_LANES
V_AUG = HEAD_DIM + 16


def _build_kv_kernel(x_ref, g_ref, w_ref, kg_ref, ucmp_ref, ks_ref, vs_ref, kw_ref, vw_ref, raw_scr):
    h = _rms(x_ref[...], g_ref[...]).astype(BF16)
    kv = _dot(h, w_ref[...])
    gw = KV_GROUPS * HEAD_DIM
    tm = kv.shape[0]
    for ty in range(2):
        raw_scr[ty] = kv[:, ty * gw:(ty + 1) * gw]
        for t in range(CMP_STRIDE):
            rows = raw_scr[ty, pl.ds(t, tm // CMP_STRIDE, stride=CMP_STRIDE), :]
            for g in range(KV_GROUPS):
                ucmp_ref[ty, g, :, t * HEAD_DIM:(t + 1) * HEAD_DIM] = (
                    rows[:, g * HEAD_DIM:(g + 1) * HEAD_DIM].astype(ucmp_ref.dtype))
    kg = kg_ref[...]
    ones = (lax.broadcasted_iota(jnp.int32, (tm, K_AUG - HEAD_DIM), 1) < 2).astype(ks_ref.dtype)
    pos = pl.program_id(1) * tm + lax.broadcasted_iota(jnp.int32, (tm, SEL_LANES), 0)
    blk = lax.broadcasted_iota(jnp.int32, (tm, SEL_LANES), 1)
    onehot = jnp.where(blk == jnp.right_shift(pos, SEL_SHIFT), MASK_BIG, 0.0).astype(ks_ref.dtype)
    ones_row = (lax.broadcasted_iota(jnp.int32, (V_AUG - HEAD_DIM, tm), 0) == 0).astype(vs_ref.dtype)
    for g in range(KV_GROUPS):
        sl = lambda t: kv[:, t * gw + g * HEAD_DIM:t * gw + (g + 1) * HEAD_DIM]
        ks_ref[g, :, :HEAD_DIM] = _rms(sl(2), kg[1:2, :]).astype(ks_ref.dtype)
        ks_ref[g, :, HEAD_DIM:K_AUG] = ones
        ks_ref[g, :, K_AUG:] = onehot
        vs_ref[g, :HEAD_DIM, :] = jnp.transpose(sl(3)).astype(vs_ref.dtype)
        vs_ref[g, HEAD_DIM:, :] = ones_row
        kw_ref[g, :, :HEAD_DIM] = _rms(sl(4), kg[2:3, :]).astype(kw_ref.dtype)
        kw_ref[g, :, HEAD_DIM:] = ones
        vw_ref[g, :HEAD_DIM, :] = jnp.transpose(sl(5)).astype(vw_ref.dtype)
        vw_ref[g, HEAD_DIM:, :] = ones_row


def _build_kv(x, kv_norm, w_kv, k_gain):
    nb, s, d = x.shape
    gw = KV_GROUPS * HEAD_DIM
    ks = jax.ShapeDtypeStruct((nb, KV_GROUPS, s, KS_AUG), BF16)
    kw = jax.ShapeDtypeStruct((nb, KV_GROUPS, s, K_AUG), BF16)
    vt = jax.ShapeDtypeStruct((nb, KV_GROUPS, V_AUG, s), BF16)
    ks_spec = pl.BlockSpec((None, KV_GROUPS, TM, KS_AUG), lambda b, i: (b, 0, i, 0))
    kw_spec = pl.BlockSpec((None, KV_GROUPS, TM, K_AUG), lambda b, i: (b, 0, i, 0))
    v_spec = pl.BlockSpec((None, KV_GROUPS, V_AUG, TM), lambda b, i: (b, 0, 0, i))
    return pl.pallas_call(
        _build_kv_kernel,
        out_shape=(jax.ShapeDtypeStruct((2, nb, KV_GROUPS, s // CMP_STRIDE, CMP_STRIDE * HEAD_DIM), BF16),
                   ks, vt, kw, vt),
        grid=(nb, s // TM),
        in_specs=[
            pl.BlockSpec((None, TM, d), lambda b, i: (b, i, 0)),
            pl.BlockSpec((1, d), lambda b, i: (0, 0)),
            pl.BlockSpec((d, 6 * gw), lambda b, i: (0, 0)),
            pl.BlockSpec((N_BRANCH, HEAD_DIM), lambda b, i: (0, 0)),
        ],
        out_specs=(pl.BlockSpec((2, None, KV_GROUPS, TM // CMP_STRIDE, CMP_STRIDE * HEAD_DIM),
                                lambda b, i: (0, b, 0, i, 0)),
                   ks_spec, v_spec, kw_spec, v_spec),
        scratch_shapes=[pltpu.VMEM((2, TM, gw), F32)],
        compiler_params=_params(("arbitrary", "arbitrary")),
        name="build_kv",
    )(x, kv_norm.reshape(1, d), w_kv.astype(BF16), k_gain)


def _compress_kernel(u_ref, w1_ref, w2_ref, pos_ref, kg_ref, out_ref, out_t_ref, *, nc):
    half = CMP_STRIDE * HEAD_DIM
    u = u_ref[...]
    lo = _dot(u, w1_ref[:half, :])
    hi = _dot(u, w1_ref[half:, :])
    posb = _dot(jnp.broadcast_to(pos_ref[...], (8, 2 * half)).astype(BF16), w1_ref[...])[0:1, :]
    h1 = lo + pltpu.roll(hi, nc - 1, 0) + posb
    hid = 0.5 * h1 * (1.0 + jnp.tanh(math.sqrt(2.0 / math.pi) * (h1 + 0.044715 * (h1 * h1 * h1))))
    c = _dot(hid.astype(BF16), w2_ref[...])
    is_key = pl.program_id(0) == 0
    out = jnp.where(is_key, _rms(c, kg_ref[...]), c)
    out_ref[...] = out.astype(out_ref.dtype)
    out_t_ref[...] = jnp.transpose(out).astype(out_t_ref.dtype)


def _compress(u, cmp_pos, cmp_w1, cmp_w2, k_gain0):
    _, nb, _, nc, width = u.shape
    hidden = cmp_w1.shape[2]
    return pl.pallas_call(
        functools.partial(_compress_kernel, nc=nc),
        out_shape=(jax.ShapeDtypeStruct((2, nb, KV_GROUPS, nc, HEAD_DIM), BF16),
                   jax.ShapeDtypeStruct((2, nb, KV_GROUPS, HEAD_DIM, nc), BF16)),
        grid=(2, nb, KV_GROUPS),
        in_specs=[
            pl.BlockSpec((None, None, None, nc, width), lambda t, b, g: (t, b, g, 0, 0)),
            pl.BlockSpec((None, 2 * width, hidden), lambda t, b, g: (t, 0, 0)),
            pl.BlockSpec((None, hidden, HEAD_DIM), lambda t, b, g: (t, 0, 0)),
            pl.BlockSpec((None, 1, 2 * width), lambda t, b, g: (t, 0, 0)),
            pl.BlockSpec((1, HEAD_DIM), lambda t, b, g: (0, 0)),
        ],
        out_specs=(pl.BlockSpec((None, None, None, nc, HEAD_DIM), lambda t, b, g: (t, b, g, 0, 0)),
                   pl.BlockSpec((None, None, None, HEAD_DIM, nc), lambda t, b, g: (t, b, g, 0, 0))),
        compiler_params=_params(("arbitrary", "arbitrary", "arbitrary")),
        name="compress",
    )(u, cmp_w1.astype(BF16), cmp_w2.astype(BF16), cmp_pos.reshape(2, 1, 2 * width),
      k_gain0.reshape(1, HEAD_DIM))


def _overlap_np(nc, n_sel):
    c0 = np.arange(nc)[:, None] * CMP_STRIDE
    s0 = np.arange(SEL_LANES)[None, :] * SEL_BLOCK
    ov = np.minimum(c0 + CMP_LEN, s0 + SEL_BLOCK) - np.maximum(c0, s0)
    ov = np.clip(ov, 0, None).astype(np.float32) / CMP_LEN
    ov[nc - 1:, :] = 0.0
    ov[:, n_sel:] = 0.0
    return ov


def _select_topk(score, ntop):
    blk = lax.broadcasted_iota(jnp.int32, score.shape, 0).astype(F32)
    for _ in range(ntop):
        m = jnp.max(score, axis=0, keepdims=True)
        first = jnp.min(jnp.where(score == m, blk, float(SEL_LANES)), axis=0, keepdims=True)
        score = jnp.where(blk == first, -jnp.inf, score)
    return score == -jnp.inf


CMP_QSUB = 4
TOPK_GROUP = 32


def _cmp_attn_kernel(*refs, nj, n_sel):
    qt_ref, kc_ref, vct_ref, ovt_ref = refs[:4]
    tab_refs = refs[4:4 + nj * CMP_QSUB]
    oct_ref, unsel_ref = refs[4 + nj * CMP_QSUB:]
    i = pl.program_id(2)
    nq = CMP_QSUB * TQ
    def attend(nk):
        rows = nk * LANES
        s = _dot(kc_ref[:rows, :], qt_ref[...])
        s = jnp.concatenate(
            [jnp.concatenate([s[jt * LANES:(jt + 1) * LANES, (qs * HPG + h) * TQ:(qs * HPG + h + 1) * TQ]
                              + tab_refs[qs * nj + jt][h]
                              for qs in range(CMP_QSUB) for h in range(HPG)], axis=1) for jt in range(nk)], axis=0)
        m = jnp.maximum(jnp.max(s, axis=0, keepdims=True), 0.5 * NEG)
        e = jnp.exp2(s - m)
        l = jnp.sum(e, axis=0, keepdims=True)
        p = e * (1.0 / jnp.where(l == 0.0, 1.0, l))
        oct_ref[...] = _dot(vct_ref[:, :rows], p.astype(BF16))

        psums = []
        for qs in range(CMP_QSUB):
            acc = p[:, qs * HPG * TQ:(qs * HPG + 1) * TQ]
            for h in range(1, HPG):
                acc = acc + p[:, (qs * HPG + h) * TQ:(qs * HPG + h + 1) * TQ]
            psums.append(acc)
        psum = jnp.concatenate(psums, axis=1)
        p_hi = psum.astype(BF16)
        p_lo = (psum - p_hi.astype(F32)).astype(BF16)
        return _dot(ovt_ref[:, :rows], p_hi) + _dot(ovt_ref[:, :rows], p_lo)

    last_visible = (nq * (i + 1) - CMP_LEN) // CMP_STRIDE
    imp_t = lax.switch(last_visible // LANES, [functools.partial(attend, nk) for nk in range(1, nj + 1)])

    tq = i * nq + lax.broadcasted_iota(jnp.int32, (SEL_LANES, nq), 1)
    j = lax.broadcasted_iota(jnp.int32, (SEL_LANES, nq), 0)
    back = jnp.right_shift(tq, SEL_SHIFT) - j
    forced = (j == 0) | ((back >= 0) & (back < N_LOCAL_SEL))
    causal = back >= 0
    n_free = min(SEL_TOPK, n_sel) - (N_LOCAL_SEL + 1)
    score = jnp.where(causal & ~forced, imp_t, NEG)

    def pick(n_groups):
        rows = n_groups * TOPK_GROUP
        top = _select_topk(score[:rows, :], n_free).astype(F32)
        if rows == SEL_LANES:
            return top
        return jnp.concatenate([top, jnp.zeros((SEL_LANES - rows, nq), F32)], axis=0)

    causal_blocks = (nq * (i + 1)) // SEL_BLOCK
    picked = lax.switch((causal_blocks - 1) // TOPK_GROUP,
                        [functools.partial(pick, n) for n in range(1, SEL_LANES // TOPK_GROUP + 1)])
    sel = causal & (forced | (picked > 0.0))
    unsel_ref[...] = jnp.where(sel, 0.0, -1.0).astype(unsel_ref.dtype)


def _cmp_attn(q_t, kvc, kvc_t, tables):
    nb = q_t.shape[0]
    s = q_t.shape[3] // HPG
    nc = s // CMP_STRIDE
    nj = nc // LANES
    n_sel = s // SEL_BLOCK
    ov_t = jnp.asarray(_overlap_np(nc, n_sel).T, dtype=BF16)

    def tab_map(qs, jt):
        def index_map(b, g, i):
            didx = CMP_QSUB * i + qs - (LANES * CMP_STRIDE // TQ) * jt
            row = jnp.where(didx < 0, CMP_MASKED, jnp.minimum(didx, CMP_CONST))
            return (CMP_BASE + row, g, 0, 0)
        return index_map

    nq = CMP_QSUB * TQ
    qt_spec = pl.BlockSpec((None, None, HEAD_DIM, HPG * nq), lambda b, g, i: (b, g, 0, i))
    n_tabs = CMP_QSUB * nj
    return pl.pallas_call(
        functools.partial(_cmp_attn_kernel, nj=nj, n_sel=n_sel),
        out_shape=(jax.ShapeDtypeStruct((nb, KV_GROUPS, HEAD_DIM, HPG * s), F32),
                   jax.ShapeDtypeStruct((nb, KV_GROUPS, SEL_LANES, s), BF16)),
        grid=(nb, KV_GROUPS, s // nq),
        in_specs=[
            qt_spec,
            pl.BlockSpec((None, None, None, nc, HEAD_DIM), lambda b, g, i: (0, b, g, 0, 0)),
            pl.BlockSpec((None, None, None, HEAD_DIM, nc), lambda b, g, i: (1, b, g, 0, 0)),
            pl.BlockSpec((SEL_LANES, nc), lambda b, g, i: (0, 0)),
        ] + [pl.BlockSpec((None, HPG, LANES, TQ), tab_map(qs, jt))
             for qs in range(CMP_QSUB) for jt in range(nj)],
        out_specs=(
            qt_spec,
            pl.BlockSpec((None, None, SEL_LANES, nq), lambda b, g, i: (b, g, 0, i)),
        ),
        compiler_params=_params(("arbitrary", "arbitrary", "arbitrary")),
        name="cmp_attn",
    )(q_t, kvc, kvc_t, ov_t, *([tables] * n_tabs))


class _FlashStream:
    def __init__(self, q_aug, k_ref, vt_ref, bias_fn, s_scr, p_scr):
        self.q_aug, self.k_ref, self.vt_ref, self.bias_fn = q_aug, k_ref, vt_ref, bias_fn
        self.s_scr, self.p_scr = s_scr, p_scr

    def _scores(self, t):
        k0 = pl.multiple_of(jnp.maximum(t, 0) * TK_SEL, TK_SEL)
        return self.bias_fn(t, _dot(self.k_ref[pl.ds(k0, TK_SEL), :], self.q_aug))

    def _pv(self, t, slot):
        k0 = pl.multiple_of(jnp.maximum(t, 0) * TK_SEL, TK_SEL)
        return _dot(self.vt_ref[:, pl.ds(k0, TK_SEL)], self.p_scr[slot])

    def _softmax(self, slot, m):
        s = self.s_scr[slot]
        m_new = jnp.maximum(m, jnp.max(s, axis=0, keepdims=True))
        self.p_scr[slot] = jnp.exp2(s - m_new).astype(self.p_scr.dtype)
        return m_new, jnp.exp2(m - m_new)

    def start(self, a, first_trip_static=False):
        lanes = self.q_aug.shape[1]
        self.s_scr[0] = self._scores(a)
        if not first_trip_static:
            self.p_scr[1] = jnp.zeros(self.p_scr.shape[1:], self.p_scr.dtype)
        return (jnp.full((1, lanes), NEG, F32), jnp.ones((1, lanes), F32), jnp.zeros((V_AUG, lanes), F32))

    def trip(self, a, carry, has_next, is_first=False):
        m, alpha, acc = carry
        if not is_first:
            acc = alpha * acc + self._pv(a - 1, 1)
        m, alpha = self._softmax(0, m)
        self.s_scr[1] = self._scores(a + 1)
        acc = alpha * acc + self._pv(a, 0)
        m, alpha = self._softmax(1, m)
        if has_next:
            self.s_scr[0] = self._scores(a + 2)
        return m, alpha, acc

    @staticmethod
    def _normalised(acc):
        return acc[:HEAD_DIM, :] / acc[HEAD_DIM:HEAD_DIM + 1, :]

    def finish_pair(self, a, carry):
        _, alpha, acc = carry
        return self._normalised(alpha * acc + self._pv(a + 1, 1))

    def finish_single(self, a, carry):
        m, alpha, acc = carry
        acc = alpha * acc + self._pv(a - 1, 1)
        _, alpha = self._softmax(0, m)
        return self._normalised(alpha * acc + self._pv(a, 0))


def _add_tiles(s, tile_fn):
    rows = []
    for part in range(TK_SEL // LANES):
        cols = [s[part * LANES:(part + 1) * LANES, h * TQ:(h + 1) * TQ] + tile_fn(part, h)
                for h in range(HPG)]
        rows.append(jnp.concatenate(cols, axis=1))
    return jnp.concatenate(rows, axis=0)


def _sel_win_kernel(qt_ref, ks_ref, vs_ref, kw_ref, vw_ref, unsel_ref, tab_ref, cvec_ref,
                    oct_ref, gates_ref, y_ref, ss_scr, ps_scr, sw_scr, pw_scr):
    i = pl.program_id(2)
    parts = TK_SEL // LANES

    c = cvec_ref[...]
    c_hi = c.astype(BF16).astype(F32)
    c_lo = c - c_hi
    arow = lax.broadcasted_iota(jnp.int32, (K_AUG - HEAD_DIM, HPG * TQ), 0)
    aug = jnp.where(arow == 0, c_hi, jnp.where(arow == 1, c_lo, 0.0)).astype(BF16)
    q_win = jnp.concatenate([qt_ref[...], aug], axis=0)
    unsel_t = unsel_ref[...]
    q_sel = jnp.concatenate([q_win, jnp.concatenate([unsel_t] * HPG, axis=1)], axis=0)

    def sel_bias(t, s):
        def tile(part, h):
            delta = i - parts * t - part
            row = jnp.where(delta < 0, SEL_MASKED, jnp.minimum(delta, SEL_CONST))
            return tab_ref[row, h]

        return _add_tiles(s, tile)

    def win_bias(t, s):
        def tile(part, h):
            delta = i - parts * t - part
            hidden = (delta < 0) | (delta >= N_WIN_NEAR) | (t < 0)
            return tab_ref[jnp.where(hidden, SEL_MASKED, WIN_BASE + delta), h]

        return _add_tiles(s, tile)

    n_tiles = i // parts + 1
    n_pairs = (n_tiles + 1) // 2
    sel = _FlashStream(q_sel, ks_ref, vs_ref, sel_bias, ss_scr, ps_scr)
    win = _FlashStream(q_win, kw_ref, vw_ref, win_bias, sw_scr, pw_scr)
    win_a0 = n_tiles - WIN_TILES
    last_pair = 2 * (n_pairs - 1)
    carry_s = sel.start(0)
    carry_w = win.start(win_a0, first_trip_static=True)
    carry_s = lax.fori_loop(0, n_pairs - 1, lambda u, c: sel.trip(2 * u, c, has_next=True), carry_s)
    carry_s = sel.trip(last_pair, carry_s, has_next=False)
    carry_w = win.trip(win_a0, carry_w, has_next=True, is_first=True)
    os_t = sel.finish_pair(last_pair, carry_s)
    ow_t = win.finish_single(win_a0 + WIN_TILES - 1, carry_w)

    oc_t = oct_ref[...]
    gt = gates_ref[...]
    for hg in range(HPG):
        c0 = N_BRANCH * hg
        lanes = slice(hg * TQ, (hg + 1) * TQ)
        out_t = (gt[c0:c0 + 1, :] * oc_t[:, lanes] + gt[c0 + 1:c0 + 2, :] * os_t[:, lanes]
                 + gt[c0 + 2:c0 + 3, :] * ow_t[:, lanes])
        y_ref[:, hg * HEAD_DIM:(hg + 1) * HEAD_DIM] = jnp.transpose(out_t).astype(y_ref.dtype)


def _sel_win_attn(q_t, ks, vs_t, kw, vw_t, unsel, tables, cvec, oc_t, gates_t):
    nb = q_t.shape[0]
    s = q_t.shape[3] // HPG
    ks_spec = pl.BlockSpec((None, None, s, KS_AUG), lambda b, g, i: (b, g, 0, 0))
    kw_spec = pl.BlockSpec((None, None, s, K_AUG), lambda b, g, i: (b, g, 0, 0))
    v_spec = pl.BlockSpec((None, None, V_AUG, s), lambda b, g, i: (b, g, 0, 0))
    qt_spec = pl.BlockSpec((None, None, HEAD_DIM, HPG * TQ), lambda b, g, i: (b, g, 0, i))
    gw = HPG * HEAD_DIM
    return pl.pallas_call(
        _sel_win_kernel,
        out_shape=jax.ShapeDtypeStruct((nb, s, NSA_HEADS * HEAD_DIM), BF16),
        grid=(nb, KV_GROUPS, s // TQ),
        in_specs=[
            qt_spec,
            ks_spec, v_spec, kw_spec, v_spec,
            pl.BlockSpec((None, None, SEL_LANES, TQ), lambda b, g, i: (b, g, 0, i)),
            pl.BlockSpec((CMP_BASE, HPG, LANES, TQ), lambda b, g, i: (0, g, 0, 0)),
            pl.BlockSpec((None, 1, HPG * TQ), lambda b, g, i: (g, 0, 0)),
            qt_spec,
            pl.BlockSpec((None, None, GATE_ROWS, TQ), lambda b, g, i: (b, g, 0, i)),
        ],
        out_specs=pl.BlockSpec((None, TQ, gw), lambda b, g, i: (b, i, g)),
        scratch_shapes=[pltpu.VMEM((2, TK_SEL, HPG * TQ), F32), pltpu.VMEM((2, TK_SEL, HPG * TQ), BF16),
                        pltpu.VMEM((2, TK_SEL, HPG * TQ), F32), pltpu.VMEM((2, TK_SEL, HPG * TQ), BF16)],
        compiler_params=_params(("arbitrary", "arbitrary", "arbitrary")),
        name="sel_win_attn",
    )(q_t, ks, vs_t, kw, vw_t, unsel, tables, cvec, oc_t, gates_t)


def kernel(x, mem, mix_norm, a_w_in, a_conv_w, b_w_in, b_q_gain, kv_norm, w_kv_shared, k_gain,
           cmp_pos, cmp_w1, cmp_w2, rel_bias, mem_norm, mem_w_kv, mem_q_gain, mem_k_gain, w_out,
           mlp_norm, w_up, w_down):
    nb, s, d = x.shape
    depth = mix_norm.shape[0]
    n_a = a_w_in.shape[0]
    assert s % (LANES * CMP_STRIDE) == 0 and s // SEL_BLOCK <= SEL_LANES
    assert s % TM == 0 and s % (CMP_QSUB * TQ) == 0
    assert WIN_BASE + N_WIN_NEAR <= CMP_BASE
    assert WIN_TILES == 3

    mk, mv_t = _mem_kv(mem, mem_norm, mem_w_kv, mem_k_gain)
    tables = _bias_tables(rel_bias)
    cvec = jnp.repeat(rel_bias[REL_BUCKETS - 1].astype(F32) * LOG2E, TQ).reshape(KV_GROUPS, 1, HPG * TQ)
    w_out_bf = w_out.astype(BF16)
    shared = None
    for layer in range(depth):
        if layer < n_a:
            y_tok, y_mem = _mixer_a(x, mix_norm[layer], a_w_in[layer], a_conv_w[layer],
                                    mk[layer], mv_t[layer], mem_q_gain[layer])
        else:
            j = layer - n_a
            ks, vs_t, kw, vw_t, kvc, kvc_t = shared
            q_t, gates_t, y_mem = _b_inproj(x, mix_norm[layer], b_w_in[j], b_q_gain[j],
                                            mk[layer], mv_t[layer], mem_q_gain[layer])
            oc_t, unsel = _cmp_attn(q_t, kvc, kvc_t, tables)
            y_tok = _sel_win_attn(q_t, ks, vs_t, kw, vw_t, unsel, tables, cvec, oc_t, gates_t)
        x = _post(x.reshape(nb * s, d), y_tok.reshape(nb * s, -1), y_mem.reshape(nb * s, -1),
                  w_out_bf, mlp_norm[layer], w_up, w_down, layer).reshape(nb, s, d)
        if layer == n_a - 1:
            ucmp, ks, vs_t, kw, vw_t = _build_kv(x, kv_norm, w_kv_shared, k_gain)
            kvc, kvc_t = _compress(ucmp, cmp_pos, cmp_w1, cmp_w2, k_gain[0])
            shared = (ks, vs_t, kw, vw_t, kvc, kvc_t)
    return x
```
